```python
import jax, jax.numpy as jnp
from jax import lax
import numpy as np

D_MODEL = 2048
BATCH = 8
SEQ = 4096
DEPTH = 4

NORM_EPS = 1e-6
D_FF = 5632
D_MIX = D_MODEL

LRU_WIDTH = D_MIX // 4
LRU_BLOCKS = 8
LRU_BLOCK_SIZE = LRU_WIDTH // LRU_BLOCKS
CONV_WIDTH = 4
LRU_C = 8.0

HEAD_DIM = 128
NSA_HEADS = (D_MIX // 2) // HEAD_DIM
NSA_KV_HEADS = 2
NSA_GROUP = NSA_HEADS // NSA_KV_HEADS
NSA_WIDTH = NSA_HEADS * HEAD_DIM
NSA_KV_WIDTH = NSA_KV_HEADS * HEAD_DIM
ROPE_DIM = HEAD_DIM // 4
ROPE_THETA = 500000.0
CMP_BLOCK = 32
CMP_STRIDE = 16
SEL_BLOCK = 64
N_SELECT = 16
WINDOW = 512
Q_BLOCK = 128
MASK_VALUE = 1e30

GLA_DV = 128
GLA_HEADS = (D_MIX // 4) // GLA_DV
GLA_DK = GLA_DV // 2
GLA_WIDTH = GLA_HEADS * GLA_DV
GLA_QK_WIDTH = GLA_HEADS * GLA_DK
GLA_RANK = 16
GLA_TAU = 16.0
GLA_CHUNK = 64

IN_SPLITS = (LRU_WIDTH, LRU_WIDTH, NSA_WIDTH) + (NSA_KV_WIDTH,) * 6 + (
    NSA_HEADS * 3, GLA_QK_WIDTH, GLA_QK_WIDTH, GLA_WIDTH, GLA_WIDTH, GLA_RANK)
D_IN = sum(IN_SPLITS)

kernel_name = 'hymba_style_rglru_nsa_gla_macaron'


def rms_norm(x, g):
    xf = x.astype(jnp.float32)
    y = xf * lax.rsqrt(jnp.mean(xf * xf, axis=-1, keepdims=True) + NORM_EPS)
    return (y * g.astype(jnp.float32)).astype(x.dtype)


def swiglu(h, w_gate, w_up, w_down):
    return (jax.nn.silu(h @ w_gate) * (h @ w_up)) @ w_down


def rope_tables(T):
    inv = 1.0 / (ROPE_THETA ** (jnp.arange(0, ROPE_DIM, 2, dtype=jnp.float32) / ROPE_DIM))
    ang = jnp.arange(T, dtype=jnp.float32)[:, None] * inv[None, :]
    return jnp.cos(ang), jnp.sin(ang)


def partial_rope(x, cos, sin):
    half = ROPE_DIM // 2
    x1, x2, xp = x[..., :half], x[..., half:ROPE_DIM], x[..., ROPE_DIM:]
    rot = jnp.concatenate([x1 * cos - x2 * sin, x2 * cos + x1 * sin], axis=-1)
    return jnp.concatenate([rot.astype(x.dtype), xp], axis=-1)


def split_cols(p):
    idx = np.cumsum(np.array(IN_SPLITS))[:-1].tolist()
    return jnp.split(p, idx, axis=-1)


def block_diag_linear(x, w, b):
    B, T, W = x.shape
    xb = x.reshape(B, T, LRU_BLOCKS, LRU_BLOCK_SIZE)
    return jnp.einsum('btni,nio->btno', xb, w).reshape(B, T, W) + b


def rg_lru(x, ga_w, ga_b, gx_w, gx_b, lam):
    f32 = jnp.float32
    r = jax.nn.sigmoid(block_diag_linear(x, ga_w, ga_b).astype(f32))
    i = jax.nn.sigmoid(block_diag_linear(x, gx_w, gx_b).astype(f32))
    log_a = -LRU_C * r * jax.nn.softplus(-lam.astype(f32))
    a = jnp.exp(log_a)
    mult = jnp.sqrt(-jnp.expm1(2.0 * log_a))
    bx = mult * i * x.astype(f32)

    def step(h, inp):
        a_t, b_t = inp
        h = a_t * h + b_t
        return h, h

    h0 = jnp.zeros((x.shape[0], x.shape[2]), f32)
    _, hs = lax.scan(step, h0, (a.swapaxes(0, 1), bx.swapaxes(0, 1)))
    return hs.swapaxes(0, 1).astype(x.dtype)


def recurrent_block(u, y_in, conv_w, conv_b, ga_w, ga_b, gx_w, gx_b, lam):
    xc = lax.conv_general_dilated(
        u, conv_w[:, None, :], window_strides=(1,), padding=[(CONV_WIDTH - 1, 0)],
        dimension_numbers=('NWC', 'WIO', 'NWC'), feature_group_count=u.shape[-1]) + conv_b
    h = rg_lru(xc, ga_w, ga_b, gx_w, gx_b, lam)
    return h * jax.nn.gelu(y_in)


def compress(k, pe, w1, w2):
    T = k.shape[2]
    nc = (T - CMP_BLOCK) // CMP_STRIDE + 1
    idx = np.arange(nc)[:, None] * CMP_STRIDE + np.arange(CMP_BLOCK)[None, :]
    blocks = k[:, :, idx] + pe
    flat = blocks.reshape(blocks.shape[0], blocks.shape[1], nc, CMP_BLOCK * HEAD_DIM)
    return jax.nn.gelu(flat @ w1) @ w2


def gather_blocks(blocks, idx):
    return jax.vmap(jax.vmap(lambda bl, ix: bl[ix]))(blocks, idx)


def masked_softmax(s, mask):
    return jax.nn.softmax(jnp.where(mask, s, -MASK_VALUE), axis=-1)


def nsa_mixer(q, k_cmp, v_cmp, k_sel, v_sel, k_win, v_win, gate_logits, gate_b,
              q_norm, kc_norm, ks_norm, kw_norm, pe_k, w1_k, w2_k, pe_v, w1_v, w2_v, cos, sin):
    B, T, _ = q.shape
    KV, G, HD = NSA_KV_HEADS, NSA_GROUP, HEAD_DIM
    f32 = jnp.float32
    scale = HEAD_DIM ** -0.5

    def to_heads(t, n):
        return t.reshape(B, T, n, HD).transpose(0, 2, 1, 3)

    qh = rms_norm(to_heads(q, NSA_HEADS), q_norm)
    q_c = qh.reshape(B, KV, G, T, HD)
    q_r = partial_rope(qh, cos, sin).reshape(B, KV, G, T, HD)

    kc = rms_norm(compress(to_heads(k_cmp, KV), pe_k, w1_k, w2_k), kc_norm)
    vc = compress(to_heads(v_cmp, KV), pe_v, w1_v, w2_v)
    nc = kc.shape[2]

    ns = T // SEL_BLOCK
    n_sel = min(N_SELECT, ns)
    ks = partial_rope(rms_norm(to_heads(k_sel, KV), ks_norm), cos, sin)
    ks_blocks = ks.reshape(B, KV, ns, SEL_BLOCK, HD)
    vs_blocks = to_heads(v_sel, KV).reshape(B, KV, ns, SEL_BLOCK, HD)

    pad = ((0, 0), (0, 0), (WINDOW, 0), (0, 0))
    kw_pad = jnp.pad(partial_rope(rms_norm(to_heads(k_win, KV), kw_norm), cos, sin), pad)
    vw_pad = jnp.pad(to_heads(v_win, KV), pad)

    gates = jax.nn.sigmoid((gate_logits + gate_b).astype(f32)).astype(q.dtype)
    gates = gates.reshape(B, T, KV, G, 3).transpose(0, 2, 3, 1, 4)

    cmp_start = np.arange(nc) * CMP_STRIDE
    sel_start = np.arange(ns) * SEL_BLOCK
    cmp_end = jnp.asarray(cmp_start + CMP_BLOCK - 1)
    overlap = jnp.asarray(((cmp_start[:, None] < sel_start[None, :] + SEL_BLOCK) &
                           (cmp_start[:, None] + CMP_BLOCK > sel_start[None, :])).astype(np.float32))
    sel_ids = jnp.arange(ns)

    def q_block(qb):
        s = qb * Q_BLOCK
        t = s + jnp.arange(Q_BLOCK)
        qc = lax.dynamic_slice_in_dim(q_c, s, Q_BLOCK, axis=3)
        qr = lax.dynamic_slice_in_dim(q_r, s, Q_BLOCK, axis=3)
        g = lax.dynamic_slice_in_dim(gates, s, Q_BLOCK, axis=3)

        sc = jnp.einsum('bkgqd,bknd->bkgqn', qc, kc).astype(f32) * scale
        cmask = cmp_end[None, :] <= t[:, None]
        pc = jnp.where(cmask, masked_softmax(sc, cmask), 0.0)
        o_cmp = jnp.einsum('bkgqn,bknd->bkgqd', pc.astype(vc.dtype), vc)

        imp = jnp.einsum('bkgqn,ns->bkqs', pc, overlap)
        cur = t // SEL_BLOCK
        forced = (sel_ids[None, :] == 0) | (sel_ids[None, :] == cur[:, None]) | (sel_ids[None, :] == cur[:, None] - 1)
        valid = sel_ids[None, :] * SEL_BLOCK <= t[:, None]
        imp = jnp.where(forced, MASK_VALUE, jnp.where(valid, imp, -MASK_VALUE))
        _, sel = lax.top_k(imp, n_sel)
        k_g = gather_blocks(ks_blocks, sel)
        v_g = gather_blocks(vs_blocks, sel)
        ss = jnp.einsum('bkgqd,bkqnjd->bkgqnj', qr, k_g).astype(f32) * scale
        kpos_sel = sel[..., None] * SEL_BLOCK + jnp.arange(SEL_BLOCK)
        smask = (kpos_sel <= t[:, None, None])[:, :, None]
        ss = jnp.where(smask, ss, -MASK_VALUE)
        ps = jax.nn.softmax(ss.reshape(ss.shape[:4] + (n_sel * SEL_BLOCK,)), axis=-1).reshape(ss.shape)
        o_sel = jnp.einsum('bkgqnj,bkqnjd->bkgqd', ps.astype(v_g.dtype), v_g)

        k_w = lax.dynamic_slice_in_dim(kw_pad, s, WINDOW + Q_BLOCK, axis=2)
        v_w = lax.dynamic_slice_in_dim(vw_pad, s, WINDOW + Q_BLOCK, axis=2)
        kpos = s - WINDOW + jnp.arange(WINDOW + Q_BLOCK)
        wmask = (kpos[None, :] <= t[:, None]) & (kpos[None, :] > t[:, None] - WINDOW) & (kpos[None, :] >= 0)
        sw = jnp.einsum('bkgqd,bkjd->bkgqj', qr, k_w).astype(f32) * scale
        pw = masked_softmax(sw, wmask)
        o_win = jnp.einsum('bkgqj,bkjd->bkgqd', pw.astype(v_w.dtype), v_w)

        return g[..., 0:1] * o_cmp + g[..., 1:2] * o_sel + g[..., 2:3] * o_win

    outs = lax.map(q_block, jnp.arange(T // Q_BLOCK))
    return outs.transpose(1, 0, 4, 2, 3, 5).reshape(B, T, NSA_WIDTH)


def gla_mixer(q, k, v, g, a_lr, a_w2, a_b, out_norm):
    B, T, _ = q.shape
    n = T // GLA_CHUNK
    f32 = jnp.float32
    log_alpha = jax.nn.log_sigmoid((a_lr @ a_w2 + a_b).astype(f32)) / GLA_TAU

    def chunks(t, d):
        return t.reshape(B, n, GLA_CHUNK, GLA_HEADS, d).transpose(1, 0, 3, 2, 4).astype(f32)

    qc = chunks(q, GLA_DK) * GLA_DK ** -0.5
    kc = chunks(k, GLA_DK)
    vc = chunks(v, GLA_DV)
    lc = chunks(log_alpha, GLA_DK)
    causal = jnp.tril(jnp.ones((GLA_CHUNK, GLA_CHUNK), dtype=bool))[:, :, None]

    def step(S, inp):
        q_, k_, v_, la = inp
        b = jnp.cumsum(la, axis=2)
        o_inter = jnp.einsum('bhid,bhde->bhie', q_ * jnp.exp(b), S)
        decay = jnp.exp(jnp.where(causal, b[:, :, :, None] - b[:, :, None], -jnp.inf))
        A = jnp.einsum('bhid,bhjd,bhijd->bhij', q_, k_, decay)
        o = o_inter + jnp.einsum('bhij,bhje->bhie', A, v_)
        b_last = b[:, :, -1]
        S = jnp.exp(b_last)[..., None] * S + jnp.einsum(
            'bhjd,bhje->bhde', k_ * jnp.exp(b_last[:, :, None] - b), v_)
        return S, o

    S0 = jnp.zeros((B, GLA_HEADS, GLA_DK, GLA_DV), f32)
    _, o = lax.scan(step, S0, (qc, kc, vc, lc))
    o = o.transpose(1, 0, 3, 2, 4).reshape(B, T, GLA_HEADS, GLA_DV)
    o = rms_norm(o, out_norm).reshape(B, T, GLA_WIDTH)
    return (o * jax.nn.silu(g.astype(f32))).astype(q.dtype)


def setup_inputs(seed: int = 0) -> dict:
    key = jax.random.key(seed)
    keys = iter(jax.random.split(key, 64))
    L = DEPTH

    def nrm(shape, scale):
        return scale * jax.random.normal(next(keys), shape, jnp.float32)

    def gain(shape):
        return 1.0 + 0.02 * jax.random.normal(next(keys), shape, jnp.float32)

    u = jax.random.uniform(next(keys), (L, LRU_WIDTH), jnp.float32, 0.9 ** 2, 0.999 ** 2)
    lru_lambda = -jnp.log(jnp.expm1(-0.5 * jnp.log(u)))
    return {
        'x': nrm((BATCH, SEQ, D_MODEL), 1.0),
        'ffn1_norm': gain((L, D_MODEL)),
        'ffn1_w_gate': nrm((L, D_MODEL, D_FF), D_MODEL ** -0.5),
        'ffn1_w_up': nrm((L, D_MODEL, D_FF), D_MODEL ** -0.5),
        'ffn1_w_down': nrm((L, D_FF, D_MODEL), D_FF ** -0.5),
        'mix_norm': gain((L, D_MODEL)),
        'w_in': nrm((L, D_MODEL, D_IN), D_MODEL ** -0.5),
        'lru_conv_w': nrm((L, CONV_WIDTH, LRU_WIDTH), CONV_WIDTH ** -0.5),
        'lru_conv_b': nrm((L, LRU_WIDTH), 0.02),
        'lru_gate_a_w': nrm((L, LRU_BLOCKS, LRU_BLOCK_SIZE, LRU_BLOCK_SIZE), LRU_BLOCK_SIZE ** -0.5),
        'lru_gate_a_b': nrm((L, LRU_WIDTH), 0.02),
        'lru_gate_x_w': nrm((L, LRU_BLOCKS, LRU_BLOCK_SIZE, LRU_BLOCK_SIZE), LRU_BLOCK_SIZE ** -0.5),
        'lru_gate_x_b': nrm((L, LRU_WIDTH), 0.02),
        'lru_lambda': lru_lambda,
        'lru_out_norm': gain((L, LRU_WIDTH)),
        'nsa_q_norm': gain((L, HEAD_DIM)),
        'nsa_k_cmp_norm': gain((L, HEAD_DIM)),
        'nsa_k_sel_norm': gain((L, HEAD_DIM)),
        'nsa_k_win_norm': gain((L, HEAD_DIM)),
        'nsa_cmp_pe_k': nrm((L, CMP_BLOCK, HEAD_DIM), 0.1),
        'nsa_cmp_w1_k': nrm((L, CMP_BLOCK * HEAD_DIM, HEAD_DIM), (CMP_BLOCK * HEAD_DIM) ** -0.5),
        'nsa_cmp_w2_k': nrm((L, HEAD_DIM, HEAD_DIM), HEAD_DIM ** -0.5),
        'nsa_cmp_pe_v': nrm((L, CMP_BLOCK, HEAD_DIM), 0.1),
        'nsa_cmp_w1_v': nrm((L, CMP_BLOCK * HEAD_DIM, HEAD_DIM), (CMP_BLOCK * HEAD_DIM) ** -0.5),
        'nsa_cmp_w2_v': nrm((L, HEAD_DIM, HEAD_DIM), HEAD_DIM ** -0.5),
        'nsa_gate_b': nrm((L, NSA_HEADS * 3), 0.02),
        'nsa_out_norm': gain((L, NSA_WIDTH)),
        'gla_a_w2': nrm((L, GLA_RANK, GLA_QK_WIDTH), GLA_RANK ** -0.5),
        'gla_a_b': nrm((L, GLA_QK_WIDTH), 0.02),
        'gla_out_norm': gain((L, GLA_DV)),
        'w_out': nrm((L, D_MIX, D_MODEL), D_MIX ** -0.5),
        'ffn2_norm': gain((L, D_MODEL)),
        'ffn2_w_gate': nrm((L, D_MODEL, D_FF), D_MODEL ** -0.5),
        'ffn2_w_up': nrm((L, D_MODEL, D_FF), D_MODEL ** -0.5),
        'ffn2_w_down': nrm((L, D_FF, D_MODEL), D_FF ** -0.5),
    }


def reference(x, ffn1_norm, ffn1_w_gate, ffn1_w_up, ffn1_w_down, mix_norm, w_in,
              lru_conv_w, lru_conv_b, lru_gate_a_w, lru_gate_a_b, lru_gate_x_w, lru_gate_x_b,
              lru_lambda, lru_out_norm, nsa_q_norm, nsa_k_cmp_norm, nsa_k_sel_norm, nsa_k_win_norm,
              nsa_cmp_pe_k, nsa_cmp_w1_k, nsa_cmp_w2_k, nsa_cmp_pe_v, nsa_cmp_w1_v, nsa_cmp_w2_v,
              nsa_gate_b, nsa_out_norm, gla_a_w2, gla_a_b, gla_out_norm, w_out,
              ffn2_norm, ffn2_w_gate, ffn2_w_up, ffn2_w_down):
    T = x.shape[1]
    cos, sin = rope_tables(T)
    for l in range(DEPTH):
        x = x + 0.5 * swiglu(rms_norm(x, ffn1_norm[l]), ffn1_w_gate[l], ffn1_w_up[l], ffn1_w_down[l])

        h = rms_norm(x, mix_norm[l])
        (lru_x, lru_y, nq, nkc, nvc, nks, nvs, nkw, nvw, ngate,
         gq, gk, gv, gg, ga) = split_cols(h @ w_in[l])
        y_a = recurrent_block(lru_x, lru_y, lru_conv_w[l], lru_conv_b[l], lru_gate_a_w[l],
                              lru_gate_a_b[l], lru_gate_x_w[l], lru_gate_x_b[l], lru_lambda[l])
        y_b = nsa_mixer(nq, nkc, nvc, nks, nvs, nkw, nvw, ngate, nsa_gate_b[l],
                        nsa_q_norm[l], nsa_k_cmp_norm[l], nsa_k_sel_norm[l], nsa_k_win_norm[l],
                        nsa_cmp_pe_k[l], nsa_cmp_w1_k[l], nsa_cmp_w2_k[l],
                        nsa_cmp_pe_v[l], nsa_cmp_w1_v[l], nsa_cmp_w2_v[l], cos, sin)
        y_c = gla_mixer(gq, gk, gv, gg, ga, gla_a_w2[l], gla_a_b[l], gla_out_norm[l])
        mix = jnp.concatenate([rms_norm(y_a, lru_out_norm[l]), rms_norm(y_b, nsa_out_norm[l]), y_c], axis=-1)
        x = x + mix @ w_out[l]

        x = x + 0.5 * swiglu(rms_norm(x, ffn2_norm[l]), ffn2_w_gate[l], ffn2_w_up[l], ffn2_w_down[l])
    return x
```

```python
import functools

import numpy as np
import jax
import jax.numpy as jnp
from jax import lax
from jax.experimental import pallas as pl
from jax.experimental.pallas import tpu as pltpu

F32 = jnp.float32
BF16 = jnp.bfloat16

NORM_EPS = 1e-6
D_MODEL = 2048
D_FF = 5632

LRU_WIDTH = 512
LRU_BLOCKS = 8
LRU_BLOCK_SIZE = LRU_WIDTH // LRU_BLOCKS
CONV_WIDTH = 4
LRU_C = 8.0

HEAD_DIM = 128
NSA_HEADS = 8
NSA_KV_HEADS = 2
NSA_GROUP = NSA_HEADS // NSA_KV_HEADS
NSA_WIDTH = NSA_HEADS * HEAD_DIM
NSA_KV_WIDTH = NSA_KV_HEADS * HEAD_DIM
ROPE_DIM = HEAD_DIM // 4
ROPE_HALF = ROPE_DIM // 2
ROPE_THETA = 500000.0
CMP_BLOCK = 32
CMP_STRIDE = 16
SEL_BLOCK = 64
N_SELECT = 16
WINDOW = 512
MASK_VALUE = 1e30

GLA_DV = 128
GLA_HEADS = 4
GLA_DK = 64
GLA_WIDTH = GLA_HEADS * GLA_DV
GLA_QK_WIDTH = GLA_HEADS * GLA_DK
GLA_RANK = 16
GLA_TAU = 16.0
GLA_STEP = 16

LANES = 128

OFF_LRU_X = 0
OFF_LRU_Y = OFF_LRU_X + LRU_WIDTH
OFF_Q = OFF_LRU_Y + LRU_WIDTH
OFF_KC = OFF_Q + NSA_WIDTH
OFF_VC = OFF_KC + NSA_KV_WIDTH
OFF_KS = OFF_VC + NSA_KV_WIDTH
OFF_VS = OFF_KS + NSA_KV_WIDTH
OFF_KW = OFF_VS + NSA_KV_WIDTH
OFF_VW = OFF_KW + NSA_KV_WIDTH
OFF_GQ = OFF_VW + NSA_KV_WIDTH
OFF_GK = OFF_GQ + GLA_HEADS * LANES
OFF_GV = OFF_GK + GLA_HEADS * LANES
OFF_GG = OFF_GV + GLA_WIDTH
OFF_MISC = OFF_GG + GLA_WIDTH
N_PACKED = OFF_MISC + LANES
MISC_GATE = 0
MISC_GA = NSA_HEADS * 3

VMEM_LIMIT = 56 * 1024 * 1024


def _cparams(sem):
    return pltpu.CompilerParams(dimension_semantics=sem, vmem_limit_bytes=VMEM_LIMIT)


def _sigmoid(x):
    return 1.0 / (1.0 + jnp.exp(-x))


def _softplus(x):
    return jnp.maximum(x, 0.0) + jnp.log1p(jnp.exp(-jnp.abs(x)))


def _gelu_tanh(x):
    c = np.float32(np.sqrt(2.0 / np.pi))
    return x * (0.5 * (1.0 + jnp.tanh(c * (x + 0.044715 * (x * x * x)))))


def _rms(x, g):
    return x * lax.rsqrt(jnp.mean(x * x, axis=-1, keepdims=True) + NORM_EPS) * g


def _dot(a, b):
    return jnp.dot(a, b, preferred_element_type=F32)


def _dot_nt(a, b):
    return lax.dot_general(a, b, (((1,), (1,)), ((), ())), preferred_element_type=F32)


def _dot_tn(a, b):
    return lax.dot_general(a, b, (((0,), (0,)), ((), ())), preferred_element_type=F32)


def _ffn_kernel(x_ref, g_ref, wg_ref, wu_ref, wd_ref, o_ref, h_ref):
    j = pl.program_id(1)

    @pl.when(j == 0)
    def _():
        h_ref[...] = _rms(x_ref[...], g_ref[...]).astype(BF16)
        o_ref[...] = jnp.zeros_like(o_ref)

    h = h_ref[...]
    gate = _dot(h, wg_ref[...])
    up = _dot(h, wu_ref[...])
    act = (gate * _sigmoid(gate)) * up
    o_ref[...] += _dot(act.astype(BF16), wd_ref[...])

    @pl.when(j == pl.num_programs(1) - 1)
    def _():
        o_ref[...] = x_ref[...] + 0.5 * o_ref[...]


def _ffn(x, g, wg, wu, wd, tm, tf):
    m, d = x.shape
    f = wg.shape[1]
    return pl.pallas_call(
        _ffn_kernel,
        grid=(m // tm, f // tf),
        in_specs=[
            pl.BlockSpec((tm, d), lambda i, j: (i, 0)),
            pl.BlockSpec((1, d), lambda i, j: (0, 0)),
            pl.BlockSpec((d, tf), lambda i, j: (0, j)),
            pl.BlockSpec((d, tf), lambda i, j: (0, j)),
            pl.BlockSpec((tf, d), lambda i, j: (j, 0)),
        ],
        out_specs=pl.BlockSpec((tm, d), lambda i, j: (i, 0)),
        out_shape=jax.ShapeDtypeStruct((m, d), F32),
        scratch_shapes=[pltpu.VMEM((tm, d), BF16)],
        compiler_params=_cparams(("parallel", "arbitrary")),
        name="ffn",
    )(x, g, wg, wu, wd)


def _mix_in_kernel(x_ref, g_ref, w_ref, o_ref, h_ref):
    @pl.when(pl.program_id(1) == 0)
    def _():
        h_ref[...] = _rms(x_ref[...], g_ref[...]).astype(BF16)

    o_ref[...] = _dot(h_ref[...], w_ref[...])


def _mix_in(x, g, w, tm, tn):
    m, d = x.shape
    n = w.shape[1]
    return pl.pallas_call(
        _mix_in_kernel,
        grid=(m // tm, n // tn),
        in_specs=[
            pl.BlockSpec((tm, d), lambda i, j: (i, 0)),
            pl.BlockSpec((1, d), lambda i, j: (0, 0)),
            pl.BlockSpec((d, tn), lambda i, j: (0, j)),
        ],
        out_specs=pl.BlockSpec((tm, tn), lambda i, j: (i, j)),
        out_shape=jax.ShapeDtypeStruct((m, n), F32),
        scratch_shapes=[pltpu.VMEM((tm, d), BF16)],
        compiler_params=_cparams(("parallel", "arbitrary")),
        name="mix_in",
    )(x, g, w)


def _mix_out_kernel(x_ref, ya_ref, yb_ref, yc_ref, gb_ref, wa_ref, wb_ref, wc_ref, o_ref):
    nb = _rms(yb_ref[...], gb_ref[...]).astype(BF16)
    acc = _dot(ya_ref[...], wa_ref[...])
    acc += _dot(nb, wb_ref[...])
    acc += _dot(yc_ref[...], wc_ref[...])
    o_ref[...] = x_ref[...] + acc


def _mix_out(x, ya, yb, yc, gb, w_out, tm):
    m, d = x.shape
    wa, wb, wc = LRU_WIDTH, NSA_WIDTH, GLA_WIDTH
    return pl.pallas_call(
        _mix_out_kernel,
        grid=(m // tm,),
        in_specs=[
            pl.BlockSpec((tm, d), lambda i: (i, 0)),
            pl.BlockSpec((tm, wa), lambda i: (i, 0)),
            pl.BlockSpec((tm, wb), lambda i: (i, 0)),
            pl.BlockSpec((tm, wc), lambda i: (i, 0)),
            pl.BlockSpec((1, wb), lambda i: (0, 0)),
            pl.BlockSpec((wa, d), lambda i: (0, 0)),
            pl.BlockSpec((wb, d), lambda i: (0, 0)),
            pl.BlockSpec((wc, d), lambda i: (0, 0)),
        ],
        out_specs=pl.BlockSpec((tm, d), lambda i: (i, 0)),
        out_shape=jax.ShapeDtypeStruct((m, d), F32),
        compiler_params=_cparams(("parallel",)),
        name="mix_out",
    )(x, ya, yb, yc, gb, w_out[:wa], w_out[wa:wa + wb], w_out[wa + wb:])


def _lru_kernel(u_ref, y_ref, cw_ref, cb_ref, wg_ref, bg_ref, lam_ref, gn_ref, o_ref,
                ext_ref, a_ref, b_ref, hc_ref):
    tt, w = u_ref.shape

    @pl.when(pl.program_id(1) == 0)
    def _():
        ext_ref[0:8, :] = jnp.zeros((8, w), F32)
        hc_ref[...] = jnp.zeros_like(hc_ref)

    u = u_ref[...]
    ext_ref[8:8 + tt, :] = u
    xc = cb_ref[...] + cw_ref[CONV_WIDTH - 1:CONV_WIDTH, :] * u
    for k in range(CONV_WIDTH - 1):
        sh = CONV_WIDTH - 1 - k
        xc = xc + cw_ref[k:k + 1, :] * ext_ref[8 - sh:8 - sh + tt, :]
    ext_ref[0:8, :] = u[tt - 8:tt, :]

    gates = _dot(xc.astype(BF16), wg_ref[...]) + bg_ref[...]
    r = _sigmoid(gates[:, :w])
    i = _sigmoid(gates[:, w:])
    log_a = (-LRU_C) * r * _softplus(-lam_ref[...])
    a = jnp.exp(log_a)
    a_ref[...] = a
    b_ref[...] = jnp.sqrt(-jnp.tanh(log_a) * (a * a + 1.0)) * i * xc

    row = lax.broadcasted_iota(jnp.int32, (8, w), 0)

    def body(gi, hprev):
        off = pl.multiple_of(gi * 8, 8)
        a8 = a_ref[pl.ds(off, 8), :]
        b8 = b_ref[pl.ds(off, 8), :]
        for s in (1, 2, 4):
            keep = row >= s
            b8 = b8 + a8 * jnp.where(keep, pltpu.roll(b8, s, 0), 0.0)
            a8 = a8 * jnp.where(keep, pltpu.roll(a8, s, 0), 1.0)
        h8 = b8 + a8 * hprev
        b_ref[pl.ds(off, 8), :] = h8
        return jnp.broadcast_to(h8[7:8, :], (8, w))

    hc_ref[...] = lax.fori_loop(0, tt // 8, body, hc_ref[...])

    ya = b_ref[...] * _gelu_tanh(y_ref[...])
    o_ref[...] = _rms(ya, gn_ref[...]).astype(o_ref.dtype)


def _lru(p, batch, seq, cw, cb, wg, bg, lam, gn, tt):
    w = LRU_WIDTH
    nt = seq // tt
    return pl.pallas_call(
        _lru_kernel,
        grid=(batch, nt),
        in_specs=[
            pl.BlockSpec((tt, w), lambda b, t: (b * nt + t, OFF_LRU_X // w)),
            pl.BlockSpec((tt, w), lambda b, t: (b * nt + t, OFF_LRU_Y // w)),
            pl.BlockSpec((CONV_WIDTH, w), lambda b, t: (0, 0)),
            pl.BlockSpec((1, w), lambda b, t: (0, 0)),
            pl.BlockSpec((w, 2 * w), lambda b, t: (0, 0)),
            pl.BlockSpec((1, 2 * w), lambda b, t: (0, 0)),
            pl.BlockSpec((1, w), lambda b, t: (0, 0)),
            pl.BlockSpec((1, w), lambda b, t: (0, 0)),
        ],
        out_specs=pl.BlockSpec((tt, w), lambda b, t: (b * nt + t, 0)),
        out_shape=jax.ShapeDtypeStruct((batch * seq, w), BF16),
        scratch_shapes=[
            pltpu.VMEM((tt + 8, w), F32),
            pltpu.VMEM((tt, w), F32),
            pltpu.VMEM((tt, w), F32),
            pltpu.VMEM((8, w), F32),
        ],
        compiler_params=_cparams(("parallel", "arbitrary")),
        name="lru",
    )(p, p, cw, cb, wg, bg, lam, gn)


def _rope(x, cosf, sina, sinb):
    return x * cosf + pltpu.roll(x, LANES - ROPE_HALF, 1) * sina + pltpu.roll(x, ROPE_HALF, 1) * sinb


def _nsa_prep_kernel(q_ref, ks_ref, vs_ref, kw_ref, vw_ref, cos_ref, sina_ref, sinb_ref,
                     qn_ref, ksn_ref, kwn_ref,
                     qc_ref, qr_ref, kso_ref, vso_ref, kwo_ref, vwo_ref):
    cosf, sina, sinb = cos_ref[...], sina_ref[...], sinb_ref[...]
    for h in range(NSA_HEADS):
        sl = slice(h * HEAD_DIM, (h + 1) * HEAD_DIM)
        qh = _rms(q_ref[:, sl], qn_ref[...])
        qc_ref[:, sl] = qh.astype(BF16)
        qr_ref[:, sl] = _rope(qh, cosf, sina, sinb).astype(BF16)
    for h in range(NSA_KV_HEADS):
        sl = slice(h * HEAD_DIM, (h + 1) * HEAD_DIM)
        kso_ref[:, sl] = _rope(_rms(ks_ref[:, sl], ksn_ref[...]), cosf, sina, sinb).astype(BF16)
        kwo_ref[:, sl] = _rope(_rms(kw_ref[:, sl], kwn_ref[...]), cosf, sina, sinb).astype(BF16)
    vso_ref[...] = vs_ref[...].astype(BF16)
    vwo_ref[...] = vw_ref[...].astype(BF16)


def _nsa_prep(p, seq, cosf, sina, sinb, qn, ksn, kwn, tt):
    m = p.shape[0]
    nt = seq // tt
    kvw = NSA_KV_WIDTH

    def col(off, width):
        return pl.BlockSpec((tt, width), lambda i: (i, off // width))

    def tab():
        return pl.BlockSpec((tt, LANES), lambda i: (i % nt, 0))

    def vec():
        return pl.BlockSpec((1, HEAD_DIM), lambda i: (0, 0))

    return pl.pallas_call(
        _nsa_prep_kernel,
        grid=(m // tt,),
        in_specs=[col(OFF_Q, NSA_WIDTH), col(OFF_KS, kvw), col(OFF_VS, kvw), col(OFF_KW, kvw), col(OFF_VW, kvw),
                  tab(), tab(), tab(), vec(), vec(), vec()],
        out_specs=[pl.BlockSpec((tt, NSA_WIDTH), lambda i: (i, 0)),
                   pl.BlockSpec((tt, NSA_WIDTH), lambda i: (i, 0)),
                   pl.BlockSpec((tt, kvw), lambda i: (i, 0)),
                   pl.BlockSpec((tt, kvw), lambda i: (i, 0)),
                   pl.BlockSpec((tt, kvw), lambda i: (i, 0)),
                   pl.BlockSpec((tt, kvw), lambda i: (i, 0))],
        out_shape=[jax.ShapeDtypeStruct((m, NSA_WIDTH), BF16),
                   jax.ShapeDtypeStruct((m, NSA_WIDTH), BF16),
                   jax.ShapeDtypeStruct((m, kvw), BF16),
                   jax.ShapeDtypeStruct((m, kvw), BF16),
                   jax.ShapeDtypeStruct((m, kvw), BF16),
                   jax.ShapeDtypeStruct((m, kvw), BF16)],
        compiler_params=_cparams(("parallel",)),
        name="nsa_prep",
    )(p, p, p, p, p, cosf, sina, sinb, qn, ksn, kwn)


def _nsa_cmp_kernel(k_ref, v_ref, pek_ref, w1k_ref, w2k_ref, pev_ref, w1v_ref, w2v_ref, kn_ref,
                    kc_ref, vc_ref):
    ng = k_ref.shape[0] // CMP_STRIDE
    rowid = lax.broadcasted_iota(jnp.int32, (ng, HEAD_DIM), 0)

    def compress(x_ref, pe_ref, w1_ref, w2_ref):
        first = jnp.zeros((ng, HEAD_DIM), F32)
        second = jnp.zeros((ng, HEAD_DIM), F32)
        for j in range(CMP_STRIDE):
            xj = x_ref[pl.ds(j, ng, stride=CMP_STRIDE), :]
            lo = (xj + pe_ref[j:j + 1, :]).astype(BF16)
            hi = (xj + pe_ref[CMP_STRIDE + j:CMP_STRIDE + j + 1, :]).astype(BF16)
            first += _dot(lo, w1_ref[j * HEAD_DIM:(j + 1) * HEAD_DIM, :])
            second += _dot(hi, w1_ref[(CMP_STRIDE + j) * HEAD_DIM:(CMP_STRIDE + j + 1) * HEAD_DIM, :])
        pre = first + pltpu.roll(second, ng - 1, 0)
        return _dot(_gelu_tanh(pre).astype(BF16), w2_ref[...])

    kc = _rms(compress(k_ref, pek_ref, w1k_ref, w2k_ref), kn_ref[...])
    vc = compress(v_ref, pev_ref, w1v_ref, w2v_ref)
    valid = rowid < ng - 1
    kc_ref[0, 0] = jnp.where(valid, kc, 0.0).astype(BF16)
    vc_ref[0, 0] = jnp.where(valid, vc, 0.0).astype(BF16)


def _nsa_cmp(p, batch, seq, pek, w1k, w2k, pev, w1v, w2v, kn):
    ng = seq // CMP_STRIDE
    hd = HEAD_DIM
    kvh = NSA_KV_HEADS

    def full(shape):
        return pl.BlockSpec(shape, lambda b, h: (0,) * len(shape))

    return pl.pallas_call(
        _nsa_cmp_kernel,
        grid=(batch, kvh),
        in_specs=[
            pl.BlockSpec((seq, hd), lambda b, h: (b, OFF_KC // hd + h)),
            pl.BlockSpec((seq, hd), lambda b, h: (b, OFF_VC // hd + h)),
            full((CMP_BLOCK, hd)), full((CMP_BLOCK * hd, hd)), full((hd, hd)),
            full((CMP_BLOCK, hd)), full((CMP_BLOCK * hd, hd)), full((hd, hd)),
            full((1, hd)),
        ],
        out_specs=[pl.BlockSpec((1, 1, ng, hd), lambda b, h: (b, h, 0, 0)),
                   pl.BlockSpec((1, 1, ng, hd), lambda b, h: (b, h, 0, 0))],
        out_shape=[jax.ShapeDtypeStruct((batch, kvh, ng, hd), BF16),
                   jax.ShapeDtypeStruct((batch, kvh, ng, hd), BF16)],
        compiler_params=_cparams(("parallel", "parallel")),
        name="nsa_cmp",
    )(p, p, pek, w1k, w2k, pev, w1v, w2v, kn)


def _nsa_attn_kernel(qc_ref, qr_ref, kc_ref, vc_ref, ks_ref, vs_ref, kw_ref, vw_ref, gl_ref, gb_ref, ovl_ref,
                     o_ref, imp_ref, *, tq, tk, n_sel):
    seq = ks_ref.shape[0]
    ncp = kc_ref.shape[2]
    ns = seq // SEL_BLOCK
    grp = NSA_GROUP
    hd = HEAD_DIM
    kv = pl.program_id(1)
    s0 = pl.program_id(2) * tq
    scale = np.float32(hd ** -0.5)
    wspan = WINDOW + tq

    qc = jnp.concatenate([qc_ref[:, g * hd:(g + 1) * hd] for g in range(grp)], axis=0)
    qr = jnp.concatenate([qr_ref[:, g * hd:(g + 1) * hd] for g in range(grp)], axis=0)
    t_col = s0 + lax.broadcasted_iota(jnp.int32, (tq, 1), 0)

    sc = (_dot_nt(qc, kc_ref[0, 0]) * scale).reshape(grp, tq, ncp)
    cend = lax.broadcasted_iota(jnp.int32, (1, ncp), 1) * CMP_STRIDE + (CMP_BLOCK - 1)
    cmask = (cend <= t_col)[None]
    scm = jnp.where(cmask, sc, -MASK_VALUE)
    e = jnp.where(cmask, jnp.exp(scm - jnp.max(scm, axis=-1, keepdims=True)), 0.0)
    den = jnp.sum(e, axis=-1, keepdims=True)
    pc = e / jnp.where(den > 0.0, den, 1.0)
    o_cmp = _dot(pc.reshape(grp * tq, ncp).astype(BF16), vc_ref[0, 0])

    pcs = pc[0]
    for g in range(1, grp):
        pcs = pcs + pc[g]
    pcs_hi = pcs.astype(BF16)
    pcs_lo = (pcs - pcs_hi.astype(F32)).astype(BF16)
    imp = _dot_nt(ovl_ref[...], pcs_hi) + _dot_nt(ovl_ref[...], pcs_lo)
    jrow = lax.broadcasted_iota(jnp.int32, (ns, tq), 0)
    cur = (s0 + lax.broadcasted_iota(jnp.int32, (ns, tq), 1)) // SEL_BLOCK
    forced = (jrow == 0) | (jrow == cur) | (jrow == cur - 1)
    imp = jnp.where(forced, MASK_VALUE, jnp.where(jrow <= cur, imp, -MASK_VALUE))
    imp_ref[...] = imp

    def rank_body(i, cnt):
        other = jnp.broadcast_to(imp_ref[pl.ds(i, 1), :], (ns, tq))
        return cnt + jnp.where(jrow > i, jnp.where(other >= imp, 1.0, 0.0), jnp.where(other > imp, 1.0, 0.0))

    rank = lax.fori_loop(0, ns, rank_body, jnp.zeros((ns, tq), F32))
    sel = jnp.where(rank < n_sel, 1.0, 0.0).T.astype(BF16)

    blk = lax.broadcasted_iota(jnp.int32, (ns, tk), 0)

    def sel_body(kt, carry):
        m, l, acc = carry
        koff = pl.multiple_of(kt * tk, tk)
        k = ks_ref[pl.ds(koff, tk), :]
        v = vs_ref[pl.ds(koff, tk), :]
        s = (_dot_nt(qr, k) * scale).reshape(grp, tq, tk)
        kpos = koff + lax.broadcasted_iota(jnp.int32, (1, tk), 1)
        expand = jnp.where(kpos // SEL_BLOCK == blk, 1.0, 0.0).astype(BF16)
        mask = ((_dot(sel, expand) > 0.5) & (kpos <= t_col))[None]
        s = jnp.where(mask, s, -MASK_VALUE)
        m_new = jnp.maximum(m, jnp.max(s, axis=-1, keepdims=True))
        pr = jnp.where(mask, jnp.exp(s - m_new), 0.0)
        alpha = jnp.exp(m - m_new)
        l = alpha * l + jnp.sum(pr, axis=-1, keepdims=True)
        pv = _dot(pr.reshape(grp * tq, tk).astype(BF16), v).reshape(grp, tq, hd)
        return m_new, l, alpha * acc + pv

    n_kt = (s0 + tq + tk - 1) // tk
    init = (jnp.full((grp, tq, 1), -MASK_VALUE, F32), jnp.zeros((grp, tq, 1), F32), jnp.zeros((grp, tq, hd), F32))
    _, l_sel, acc_sel = lax.fori_loop(0, n_kt, sel_body, init)
    o_sel = acc_sel / l_sel

    start = pl.multiple_of(jnp.maximum(s0 - WINDOW, 0), tq)
    kpos = start + lax.broadcasted_iota(jnp.int32, (1, wspan), 1)
    wmask = ((kpos <= t_col) & (kpos > t_col - WINDOW))[None]
    sw = (_dot_nt(qr, kw_ref[pl.ds(start, wspan), :]) * scale).reshape(grp, tq, wspan)
    sw = jnp.where(wmask, sw, -MASK_VALUE)
    ew = jnp.where(wmask, jnp.exp(sw - jnp.max(sw, axis=-1, keepdims=True)), 0.0)
    lw = jnp.sum(ew, axis=-1, keepdims=True)
    o_win = _dot(ew.reshape(grp * tq, wspan).astype(BF16), vw_ref[pl.ds(start, wspan), :]).reshape(grp, tq, hd) / lw

    gates = _sigmoid(gl_ref[...] + gb_ref[...])
    lane = lax.broadcasted_iota(jnp.int32, (tq, LANES), 1)
    o_cmp = o_cmp.reshape(grp, tq, hd)
    for g in range(grp):
        base = MISC_GATE + (kv * grp + g) * 3
        gcol = [jnp.sum(jnp.where(lane == base + br, gates, 0.0), axis=-1, keepdims=True) for br in range(3)]
        o_ref[:, g * hd:(g + 1) * hd] = gcol[0] * o_cmp[g] + gcol[1] * o_sel[g] + gcol[2] * o_win[g]


def _nsa_attn(p, qc, qr, kc, vc, ks, vs, kw, vw, gate_b, ovl_t, batch, seq, tq, tk):
    hd = HEAD_DIM
    gw = NSA_GROUP * hd
    nq = seq // tq
    ns = seq // SEL_BLOCK
    ncp = kc.shape[2]
    n_sel = min(N_SELECT, ns)
    assert seq % tk == 0 and tk % SEL_BLOCK == 0 and seq >= WINDOW + tq and WINDOW % tq == 0

    def qspec():
        return pl.BlockSpec((tq, gw), lambda b, h, i: (b * nq + i, h))

    def cspec():
        return pl.BlockSpec((1, 1, ncp, hd), lambda b, h, i: (b, h, 0, 0))

    def kspec():
        return pl.BlockSpec((seq, hd), lambda b, h, i: (b, h))

    return pl.pallas_call(
        functools.partial(_nsa_attn_kernel, tq=tq, tk=tk, n_sel=n_sel),
        grid=(batch, NSA_KV_HEADS, nq),
        in_specs=[qspec(), qspec(), cspec(), cspec(), kspec(), kspec(), kspec(), kspec(),
                  pl.BlockSpec((tq, LANES), lambda b, h, i: (b * nq + i, OFF_MISC // LANES)),
                  pl.BlockSpec((1, LANES), lambda b, h, i: (0, 0)),
                  pl.BlockSpec((ns, ncp), lambda b, h, i: (0, 0))],
        out_specs=pl.BlockSpec((tq, gw), lambda b, h, i: (b * nq + i, h)),
        out_shape=jax.ShapeDtypeStruct((batch * seq, NSA_WIDTH), F32),
        scratch_shapes=[pltpu.VMEM((ns, tq), F32)],
        compiler_params=_cparams(("parallel", "parallel", "arbitrary")),
        name="nsa_attn",
    )(qc, qr, kc, vc, ks, vs, kw, vw, p, gate_b, ovl_t)


def _gla_kernel(q_ref, k_ref, v_ref, g_ref, misc_ref, a2_ref, ab_ref, gn_ref, o_ref,
                st_ref, qe_ref, kd_ref, vb_ref, ds_ref, oi_ref):
    tt, wd = q_ref.shape
    step = GLA_STEP
    ng = tt // step
    nh = GLA_HEADS
    hl = LANES

    @pl.when(pl.program_id(1) == 0)
    def _():
        st_ref[...] = jnp.zeros_like(st_ref)

    x = _dot(misc_ref[...].astype(BF16), a2_ref[...]) + ab_ref[...]
    la = (-_softplus(-x)) * np.float32(1.0 / GLA_TAU)
    pos = lax.broadcasted_iota(jnp.int32, (tt, wd), 0) % step
    b = la
    s = 1
    while s < step:
        b = b + jnp.where(pos >= s, pltpu.roll(b, s, 0), 0.0)
        s *= 2

    q3 = (q_ref[...] * np.float32(GLA_DK ** -0.5)).reshape(ng, step, wd)
    k3 = k_ref[...].reshape(ng, step, wd)
    v3 = v_ref[...].reshape(ng, step, wd)
    b3 = b.reshape(ng, step, wd)
    pos3 = pos.reshape(ng, step, wd)
    b_last = b3[:, step - 1:step, :]
    qe_ref[...] = (q3 * jnp.exp(b3)).reshape(tt, wd).astype(BF16)
    kd_ref[...] = (k3 * jnp.exp(b_last - b3)).reshape(tt, wd).astype(BF16)
    vb_ref[...] = v_ref[...].astype(BF16)
    ds_ref[...] = jnp.exp(b_last)

    ones = jnp.ones((hl, hl), BF16)
    odiag = jnp.zeros((tt, wd), F32)
    for j in range(step):
        dec = jnp.exp(jnp.where(pos3 >= j, b3 - b3[:, j:j + 1, :], -MASK_VALUE))
        term = (q3 * k3[:, j:j + 1, :] * dec).reshape(tt, wd).astype(BF16)
        vj = jnp.broadcast_to(v3[:, j:j + 1, :], (ng, step, wd)).reshape(tt, wd)
        aij = jnp.concatenate([_dot(term[:, h * hl:(h + 1) * hl], ones) for h in range(nh)], axis=1)
        odiag = odiag + aij * vj

    def body(gi, carry):
        r0 = pl.multiple_of(gi * step, step)
        dsg = ds_ref[gi]
        for h in range(nh):
            sl = slice(h * hl, (h + 1) * hl)
            st = st_ref[h]
            oi_ref[pl.ds(r0, step), sl] = _dot_nt(qe_ref[pl.ds(r0, step), sl], st.astype(BF16))
            upd = _dot_tn(vb_ref[pl.ds(r0, step), sl], kd_ref[pl.ds(r0, step), sl])
            st_ref[h] = st * dsg[:, sl] + upd
        return carry

    lax.fori_loop(0, ng, body, 0)

    o = oi_ref[...] + odiag
    gate = g_ref[...]
    gate = gate * _sigmoid(gate)
    for h in range(nh):
        sl = slice(h * hl, (h + 1) * hl)
        o_ref[:, sl] = (_rms(o[:, sl], gn_ref[...]) * gate[:, sl]).astype(o_ref.dtype)


def _gla(p, batch, seq, a2, ab, gn, tt):
    wd = GLA_HEADS * LANES
    nt = seq // tt
    ng = tt // GLA_STEP

    def col(off, width):
        return pl.BlockSpec((tt, width), lambda b, t: (b * nt + t, off // width))

    return pl.pallas_call(
        _gla_kernel,
        grid=(batch, nt),
        in_specs=[col(OFF_GQ, wd), col(OFF_GK, wd), col(OFF_GV, wd), col(OFF_GG, wd), col(OFF_MISC, LANES),
                  pl.BlockSpec((LANES, wd), lambda b, t: (0, 0)),
                  pl.BlockSpec((1, wd), lambda b, t: (0, 0)),
                  pl.BlockSpec((1, GLA_DV), lambda b, t: (0, 0))],
        out_specs=pl.BlockSpec((tt, wd), lambda b, t: (b * nt + t, 0)),
        out_shape=jax.ShapeDtypeStruct((batch * seq, wd), BF16),
        scratch_shapes=[
            pltpu.VMEM((GLA_HEADS, GLA_DV, LANES), F32),
            pltpu.VMEM((tt, wd), BF16),
            pltpu.VMEM((tt, wd), BF16),
            pltpu.VMEM((tt, wd), BF16),
            pltpu.VMEM((ng, 1, wd), F32),
            pltpu.VMEM((tt, wd), F32),
        ],
        compiler_params=_cparams(("parallel", "arbitrary")),
        name="gla",
    )(p, p, p, p, p, a2, ab, gn)


def _pad_heads(w, heads, width):
    lead = w.shape[:-1]
    w = w.reshape(lead + (heads, width))
    w = jnp.pad(w, [(0, 0)] * len(lead) + [(0, 0), (0, LANES - width)])
    return w.reshape(lead + (heads * LANES,))


def _pack_w_in(w_in):
    d = w_in.shape[0]
    o_gate = OFF_VW + NSA_KV_WIDTH
    o_gq = o_gate + NSA_HEADS * 3
    o_gk = o_gq + GLA_QK_WIDTH
    o_gv = o_gk + GLA_QK_WIDTH
    o_ga = o_gv + 2 * GLA_WIDTH
    misc = jnp.concatenate([w_in[:, o_gate:o_gq], w_in[:, o_ga:o_ga + GLA_RANK],
                            jnp.zeros((d, LANES - NSA_HEADS * 3 - GLA_RANK), w_in.dtype)], axis=1)
    packed = jnp.concatenate([w_in[:, :o_gate],
                              _pad_heads(w_in[:, o_gq:o_gk], GLA_HEADS, GLA_DK),
                              _pad_heads(w_in[:, o_gk:o_gv], GLA_HEADS, GLA_DK),
                              w_in[:, o_gv:o_ga], misc], axis=1)
    assert packed.shape[1] == N_PACKED
    return packed.astype(BF16)


def _block_diag(w):
    nb, bi, bo = w.shape
    eye = jnp.eye(nb, dtype=w.dtype)
    return (eye[:, None, :, None] * w[:, :, None, :]).reshape(nb * bi, nb * bo)


def _rope_lane_tables(seq):
    inv = 1.0 / (ROPE_THETA ** (jnp.arange(0, ROPE_DIM, 2, dtype=F32) / ROPE_DIM))
    ang = jnp.arange(seq, dtype=F32)[:, None] * inv[None, :]
    cos, sin = jnp.cos(ang), jnp.sin(ang)
    rest = LANES - ROPE_DIM
    zeros_h = jnp.zeros((seq, ROPE_HALF), F32)
    cosf = jnp.concatenate([cos, cos, jnp.ones((seq, rest), F32)], axis=1)
    sina = jnp.concatenate([-sin, zeros_h, jnp.zeros((seq, rest), F32)], axis=1)
    sinb = jnp.concatenate([zeros_h, sin, jnp.zeros((seq, rest), F32)], axis=1)
    return cosf, sina, sinb


def _overlap_t(seq, ncp):
    ns = seq // SEL_BLOCK
    nc = (seq - CMP_BLOCK) // CMP_STRIDE + 1
    cs = np.arange(ncp) * CMP_STRIDE
    ss = np.arange(ns) * SEL_BLOCK
    ov = (cs[None, :] < ss[:, None] + SEL_BLOCK) & (cs[None, :] + CMP_BLOCK > ss[:, None]) & (np.arange(ncp) < nc)[None, :]
    return jnp.asarray(ov.astype(np.float32), dtype=BF16)


def _row(v):
    return v.reshape(1, -1)


def _mixers(p, batch, seq, l, prm, tables, ovl_t):
    cosf, sina, sinb = tables
    wg = jnp.concatenate([_block_diag(prm['lru_gate_a_w'][l]), _block_diag(prm['lru_gate_x_w'][l])], axis=1).astype(BF16)
    bg = jnp.concatenate([prm['lru_gate_a_b'][l], prm['lru_gate_x_b'][l]]).reshape(1, -1)
    ya = _lru(p, batch, seq, prm['lru_conv_w'][l], _row(prm['lru_conv_b'][l]), wg, bg,
              _row(prm['lru_lambda'][l]), _row(prm['lru_out_norm'][l]), tt=min(512, seq))
    qc, qr, ks, vs, kw, vw = _nsa_prep(p, seq, cosf, sina, sinb, _row(prm['nsa_q_norm'][l]),
                                       _row(prm['nsa_k_sel_norm'][l]), _row(prm['nsa_k_win_norm'][l]),
                                       tt=min(512, seq))
    kc, vc = _nsa_cmp(p, batch, seq, prm['nsa_cmp_pe_k'][l], prm['nsa_cmp_w1_k'][l].astype(BF16),
                      prm['nsa_cmp_w2_k'][l].astype(BF16), prm['nsa_cmp_pe_v'][l],
                      prm['nsa_cmp_w1_v'][l].astype(BF16), prm['nsa_cmp_w2_v'][l].astype(BF16),
                      _row(prm['nsa_k_cmp_norm'][l]))
    gate_b = jnp.pad(prm['nsa_gate_b'][l], (MISC_GATE, LANES - MISC_GATE - NSA_HEADS * 3)).reshape(1, LANES)
    yb = _nsa_attn(p, qc, qr, kc, vc, ks, vs, kw, vw, gate_b, ovl_t, batch, seq, tq=128, tk=min(512, seq))
    a2 = jnp.zeros((LANES, GLA_HEADS * LANES), F32).at[MISC_GA:MISC_GA + GLA_RANK].set(
        _pad_heads(prm['gla_a_w2'][l], GLA_HEADS, GLA_DK)).astype(BF16)
    ab = _pad_heads(prm['gla_a_b'][l], GLA_HEADS, GLA_DK).reshape(1, -1)
    yc = _gla(p, batch, seq, a2, ab, _row(prm['gla_out_norm'][l]), tt=min(256, seq))
    return ya, yb, yc


def kernel(x, ffn1_norm, ffn1_w_gate, ffn1_w_up, ffn1_w_down, mix_norm, w_in, lru_conv_w, lru_conv_b, lru_gate_a_w, lru_gate_a_b, lru_gate_x_w, lru_gate_x_b, lru_lambda, lru_out_norm, nsa_q_norm, nsa_k_cmp_norm, nsa_k_sel_norm, nsa_k_win_norm, nsa_cmp_pe_k, nsa_cmp_w1_k, nsa_cmp_w2_k, nsa_cmp_pe_v, nsa_cmp_w1_v, nsa_cmp_w2_v, nsa_gate_b, nsa_out_norm, gla_a_w2, gla_a_b, gla_out_norm, w_out, ffn2_norm, ffn2_w_gate, ffn2_w_up, ffn2_w_down):
    prm = dict(lru_conv_w=lru_conv_w, lru_conv_b=lru_conv_b, lru_gate_a_w=lru_gate_a_w, lru_gate_a_b=lru_gate_a_b,
               lru_gate_x_w=lru_gate_x_w, lru_gate_x_b=lru_gate_x_b, lru_lambda=lru_lambda, lru_out_norm=lru_out_norm,
               nsa_q_norm=nsa_q_norm, nsa_k_cmp_norm=nsa_k_cmp_norm, nsa_k_sel_norm=nsa_k_sel_norm,
               nsa_k_win_norm=nsa_k_win_norm, nsa_cmp_pe_k=nsa_cmp_pe_k, nsa_cmp_w1_k=nsa_cmp_w1_k,
               nsa_cmp_w2_k=nsa_cmp_w2_k, nsa_cmp_pe_v=nsa_cmp_pe_v, nsa_cmp_w1_v=nsa_cmp_w1_v,
               nsa_cmp_w2_v=nsa_cmp_w2_v, nsa_gate_b=nsa_gate_b, gla_a_w2=gla_a_w2, gla_a_b=gla_a_b,
               gla_out_norm=gla_out_norm)
    batch, seq, d = x.shape
    depth = w_in.shape[0]
    m = batch * seq
    tables = _rope_lane_tables(seq)
    ncp = seq // CMP_STRIDE
    ovl_t = _overlap_t(seq, ncp)
    tm = min(512, m)
    tf = 512
    xf = x.reshape(m, d)
    for l in range(depth):
        xf = _ffn(xf, _row(ffn1_norm[l]), ffn1_w_gate[l].astype(BF16), ffn1_w_up[l].astype(BF16),
                  ffn1_w_down[l].astype(BF16), tm, tf)
        p = _mix_in(xf, _row(mix_norm[l]), _pack_w_in(w_in[l]), tm, N_PACKED // 5)
        ya, yb, yc = _mixers(p, batch, seq, l, prm, tables, ovl_t)
        xf = _mix_out(xf, ya, yb, yc, _row(nsa_out_norm[l]), w_out[l].astype(BF16), tm)
        xf = _ffn(xf, _row(ffn2_norm[l]), ffn2_w_gate[l].astype(BF16), ffn2_w_up[l].astype(BF16),
                  ffn2_w_down[l].astype(BF16), tm, tf)
    return xf.reshape(batch, seq, d)
```

```python
import functools

import numpy as np
import jax
import jax.numpy as jnp
from jax import lax
from jax.experimental import pallas as pl
from jax.experimental.pallas import tpu as pltpu

F32 = jnp.float32
BF16 = jnp.bfloat16

NORM_EPS = 1e-6
D_MODEL = 2048
D_FF = 5632

LRU_WIDTH = 512
LRU_BLOCKS = 8
LRU_BLOCK_SIZE = LRU_WIDTH // LRU_BLOCKS
CONV_WIDTH = 4
LRU_C = 8.0

HEAD_DIM = 128
NSA_HEADS = 8
NSA_KV_HEADS = 2
NSA_GROUP = NSA_HEADS // NSA_KV_HEADS
NSA_WIDTH = NSA_HEADS * HEAD_DIM
NSA_KV_WIDTH = NSA_KV_HEADS * HEAD_DIM
ROPE_DIM = HEAD_DIM // 4
ROPE_HALF = ROPE_DIM // 2
ROPE_THETA = 500000.0
CMP_BLOCK = 32
CMP_STRIDE = 16
SEL_BLOCK = 64
N_SELECT = 16
WINDOW = 512
MASK_VALUE = 1e30
QK_LOG2_SCALE = float(HEAD_DIM ** -0.5 * np.log2(np.e))
VT_ROWS = HEAD_DIM + 16

GLA_DV = 128
GLA_HEADS = 4
GLA_DK = 64
GLA_WIDTH = GLA_HEADS * GLA_DV
GLA_QK_WIDTH = GLA_HEADS * GLA_DK
GLA_RANK = 16
GLA_TAU = 16.0
GLA_STEP = 16

LANES = 128

OFF_LRU_X = 0
OFF_LRU_Y = OFF_LRU_X + LRU_WIDTH
OFF_Q = OFF_LRU_Y + LRU_WIDTH
OFF_KC = OFF_Q + NSA_WIDTH
OFF_VC = OFF_KC + NSA_KV_WIDTH
OFF_KS = OFF_VC + NSA_KV_WIDTH
OFF_VS = OFF_KS + NSA_KV_WIDTH
OFF_KW = OFF_VS + NSA_KV_WIDTH
OFF_VW = OFF_KW + NSA_KV_WIDTH
OFF_GQ = OFF_VW + NSA_KV_WIDTH
OFF_GK = OFF_GQ + GLA_HEADS * LANES
OFF_GV = OFF_GK + GLA_HEADS * LANES
OFF_GG = OFF_GV + GLA_WIDTH
OFF_MISC = OFF_GG + GLA_WIDTH
N_PACKED = OFF_MISC + LANES
MISC_GATE = 0
MISC_GA = NSA_HEADS * 3

VMEM_LIMIT = 56 * 1024 * 1024


def _cparams(sem):
    return pltpu.CompilerParams(dimension_semantics=sem, vmem_limit_bytes=VMEM_LIMIT)


def _sigmoid(x):
    return 1.0 / (1.0 + jnp.exp(-x))


def _softplus(x):
    return jnp.maximum(x, 0.0) + jnp.log1p(jnp.exp(-jnp.abs(x)))


def _gelu_tanh(x):
    c = np.float32(np.sqrt(2.0 / np.pi))
    return x * (0.5 * (1.0 + jnp.tanh(c * (x + 0.044715 * (x * x * x)))))


def _rms(x, g):
    return x * lax.rsqrt(jnp.mean(x * x, axis=-1, keepdims=True) + NORM_EPS) * g


def _dot(a, b):
    return jnp.dot(a, b, preferred_element_type=F32)


def _dot_nt(a, b):
    return lax.dot_general(a, b, (((1,), (1,)), ((), ())), preferred_element_type=F32)


def _dot_tn(a, b):
    return lax.dot_general(a, b, (((0,), (0,)), ((), ())), preferred_element_type=F32)


def _ffn_kernel(x_ref, g_ref, wg_ref, wu_ref, wd_ref, o_ref, h_ref):
    j = pl.program_id(1)

    @pl.when(j == 0)
    def _():
        h_ref[...] = _rms(x_ref[...], g_ref[...]).astype(BF16)
        o_ref[...] = jnp.zeros_like(o_ref)

    h = h_ref[...]
    gate = _dot(h, wg_ref[...])
    up = _dot(h, wu_ref[...])
    act = (gate * _sigmoid(gate)) * up
    o_ref[...] += _dot(act.astype(BF16), wd_ref[...])

    @pl.when(j == pl.num_programs(1) - 1)
    def _():
        o_ref[...] = x_ref[...] + 0.5 * o_ref[...]


def _ffn(x, g, wg, wu, wd, tm, tf):
    m, d = x.shape
    f = wg.shape[1]
    return pl.pallas_call(
        _ffn_kernel,
        grid=(m // tm, f // tf),
        in_specs=[
            pl.BlockSpec((tm, d), lambda i, j: (i, 0)),
            pl.BlockSpec((1, d), lambda i, j: (0, 0)),
            pl.BlockSpec((d, tf), lambda i, j: (0, j)),
            pl.BlockSpec((d, tf), lambda i, j: (0, j)),
            pl.BlockSpec((tf, d), lambda i, j: (j, 0)),
        ],
        out_specs=pl.BlockSpec((tm, d), lambda i, j: (i, 0)),
        out_shape=jax.ShapeDtypeStruct((m, d), F32),
        scratch_shapes=[pltpu.VMEM((tm, d), BF16)],
        compiler_params=_cparams(("parallel", "arbitrary")),
        name="ffn",
    )(x, g, wg, wu, wd)


def _mix_in_kernel(x_ref, g_ref, w_ref, o_ref, h_ref):
    @pl.when(pl.program_id(1) == 0)
    def _():
        h_ref[...] = _rms(x_ref[...], g_ref[...]).astype(BF16)

    o_ref[...] = _dot(h_ref[...], w_ref[...])


def _mix_in(x, g, w, tm, tn):
    m, d = x.shape
    n = w.shape[1]
    return pl.pallas_call(
        _mix_in_kernel,
        grid=(m // tm, n // tn),
        in_specs=[
            pl.BlockSpec((tm, d), lambda i, j: (i, 0)),
            pl.BlockSpec((1, d), lambda i, j: (0, 0)),
            pl.BlockSpec((d, tn), lambda i, j: (0, j)),
        ],
        out_specs=pl.BlockSpec((tm, tn), lambda i, j: (i, j)),
        out_shape=jax.ShapeDtypeStruct((m, n), F32),
        scratch_shapes=[pltpu.VMEM((tm, d), BF16)],
        compiler_params=_cparams(("parallel", "arbitrary")),
        name="mix_in",
    )(x, g, w)


def _mix_out_kernel(x_ref, ya_ref, yb_ref, yc_ref, gb_ref, wa_ref, wb_ref, wc_ref, o_ref):
    nb = _rms(yb_ref[...], gb_ref[...]).astype(BF16)
    acc = _dot(ya_ref[...], wa_ref[...])
    acc += _dot(nb, wb_ref[...])
    acc += _dot(yc_ref[...], wc_ref[...])
    o_ref[...] = x_ref[...] + acc


def _mix_out(x, ya, yb, yc, gb, w_out, tm):
    m, d = x.shape
    wa, wb, wc = LRU_WIDTH, NSA_WIDTH, GLA_WIDTH
    return pl.pallas_call(
        _mix_out_kernel,
        grid=(m // tm,),
        in_specs=[
            pl.BlockSpec((tm, d), lambda i: (i, 0)),
            pl.BlockSpec((tm, wa), lambda i: (i, 0)),
            pl.BlockSpec((tm, wb), lambda i: (i, 0)),
            pl.BlockSpec((tm, wc), lambda i: (i, 0)),
            pl.BlockSpec((1, wb), lambda i: (0, 0)),
            pl.BlockSpec((wa, d), lambda i: (0, 0)),
            pl.BlockSpec((wb, d), lambda i: (0, 0)),
            pl.BlockSpec((wc, d), lambda i: (0, 0)),
        ],
        out_specs=pl.BlockSpec((tm, d), lambda i: (i, 0)),
        out_shape=jax.ShapeDtypeStruct((m, d), F32),
        compiler_params=_cparams(("parallel",)),
        name="mix_out",
    )(x, ya, yb, yc, gb, w_out[:wa], w_out[wa:wa + wb], w_out[wa + wb:])


def _lru_kernel(u_ref, y_ref, cw_ref, cb_ref, wg_ref, bg_ref, lam_ref, gn_ref, o_ref,
                ext_ref, a_ref, b_ref, hc_ref):
    tt, w = u_ref.shape

    @pl.when(pl.program_id(1) == 0)
    def _():
        ext_ref[0:8, :] = jnp.zeros((8, w), F32)
        hc_ref[...] = jnp.zeros_like(hc_ref)

    u = u_ref[...]
    ext_ref[8:8 + tt, :] = u
    xc = cb_ref[...] + cw_ref[CONV_WIDTH - 1:CONV_WIDTH, :] * u
    for k in range(CONV_WIDTH - 1):
        sh = CONV_WIDTH - 1 - k
        xc = xc + cw_ref[k:k + 1, :] * ext_ref[8 - sh:8 - sh + tt, :]
    ext_ref[0:8, :] = u[tt - 8:tt, :]

    gates = _dot(xc.astype(BF16), wg_ref[...]) + bg_ref[...]
    r = _sigmoid(gates[:, :w])
    i = _sigmoid(gates[:, w:])
    log_a = (-LRU_C) * r * _softplus(-lam_ref[...])
    a = jnp.exp(log_a)
    a_ref[...] = a
    b_ref[...] = jnp.sqrt(-jnp.tanh(log_a) * (a * a + 1.0)) * i * xc

    row = lax.broadcasted_iota(jnp.int32, (8, w), 0)

    def body(gi, hprev):
        off = pl.multiple_of(gi * 8, 8)
        a8 = a_ref[pl.ds(off, 8), :]
        b8 = b_ref[pl.ds(off, 8), :]
        for s in (1, 2, 4):
            keep = row >= s
            b8 = b8 + a8 * jnp.where(keep, pltpu.roll(b8, s, 0), 0.0)
            a8 = a8 * jnp.where(keep, pltpu.roll(a8, s, 0), 1.0)
        h8 = b8 + a8 * hprev
        b_ref[pl.ds(off, 8), :] = h8
        return jnp.broadcast_to(h8[7:8, :], (8, w))

    hc_ref[...] = lax.fori_loop(0, tt // 8, body, hc_ref[...])

    ya = b_ref[...] * _gelu_tanh(y_ref[...])
    o_ref[...] = _rms(ya, gn_ref[...]).astype(o_ref.dtype)


def _lru(p, batch, seq, cw, cb, wg, bg, lam, gn, tt):
    w = LRU_WIDTH
    nt = seq // tt
    return pl.pallas_call(
        _lru_kernel,
        grid=(batch, nt),
        in_specs=[
            pl.BlockSpec((tt, w), lambda b, t: (b * nt + t, OFF_LRU_X // w)),
            pl.BlockSpec((tt, w), lambda b, t: (b * nt + t, OFF_LRU_Y // w)),
            pl.BlockSpec((CONV_WIDTH, w), lambda b, t: (0, 0)),
            pl.BlockSpec((1, w), lambda b, t: (0, 0)),
            pl.BlockSpec((w, 2 * w), lambda b, t: (0, 0)),
            pl.BlockSpec((1, 2 * w), lambda b, t: (0, 0)),
            pl.BlockSpec((1, w), lambda b, t: (0, 0)),
            pl.BlockSpec((1, w), lambda b, t: (0, 0)),
        ],
        out_specs=pl.BlockSpec((tt, w), lambda b, t: (b * nt + t, 0)),
        out_shape=jax.ShapeDtypeStruct((batch * seq, w), BF16),
        scratch_shapes=[
            pltpu.VMEM((tt + 8, w), F32),
            pltpu.VMEM((tt, w), F32),
            pltpu.VMEM((tt, w), F32),
            pltpu.VMEM((8, w), F32),
        ],
        compiler_params=_cparams(("parallel", "arbitrary")),
        name="lru",
    )(p, p, cw, cb, wg, bg, lam, gn)


def _rope(x, cosf, sina, sinb):
    return x * cosf + pltpu.roll(x, LANES - ROPE_HALF, 1) * sina + pltpu.roll(x, ROPE_HALF, 1) * sinb


def _nsa_prep_kernel(q_ref, ks_ref, vs_ref, kw_ref, vw_ref, cos_ref, sina_ref, sinb_ref,
                     qn_ref, ksn_ref, kwn_ref,
                     qc_ref, qr_ref, kso_ref, vso_ref, kwo_ref, vwo_ref):
    cosf, sina, sinb = cos_ref[...], sina_ref[...], sinb_ref[...]
    for h in range(NSA_HEADS):
        sl = slice(h * HEAD_DIM, (h + 1) * HEAD_DIM)
        qh = _rms(q_ref[:, sl], qn_ref[...])
        qc_ref[:, sl] = (qh * QK_LOG2_SCALE).astype(BF16)
        qr_ref[:, sl] = (_rope(qh, cosf, sina, sinb) * QK_LOG2_SCALE).astype(BF16)
    for h in range(NSA_KV_HEADS):
        sl = slice(h * HEAD_DIM, (h + 1) * HEAD_DIM)
        kso_ref[:, sl] = _rope(_rms(ks_ref[:, sl], ksn_ref[...]), cosf, sina, sinb).astype(BF16)
        kwo_ref[:, sl] = _rope(_rms(kw_ref[:, sl], kwn_ref[...]), cosf, sina, sinb).astype(BF16)
        ones = jnp.ones((VT_ROWS - HEAD_DIM, vs_ref.shape[0]), BF16)
        vso_ref[0, h, 0:HEAD_DIM, :] = vs_ref[:, sl].T.astype(BF16)
        vso_ref[0, h, HEAD_DIM:VT_ROWS, :] = ones
        vwo_ref[0, h, 0:HEAD_DIM, :] = vw_ref[:, sl].T.astype(BF16)
        vwo_ref[0, h, HEAD_DIM:VT_ROWS, :] = ones


def _nsa_prep(p, seq, cosf, sina, sinb, qn, ksn, kwn, tt):
    m = p.shape[0]
    nt = seq // tt
    kvw = NSA_KV_WIDTH

    def col(off, width):
        return pl.BlockSpec((tt, width), lambda i: (i, off // width))

    def tab():
        return pl.BlockSpec((tt, LANES), lambda i: (i % nt, 0))

    def vec():
        return pl.BlockSpec((1, HEAD_DIM), lambda i: (0, 0))

    def vt_spec():
        return pl.BlockSpec((1, NSA_KV_HEADS, VT_ROWS, tt), lambda i: (i // nt, 0, 0, i % nt))

    return pl.pallas_call(
        _nsa_prep_kernel,
        grid=(m // tt,),
        in_specs=[col(OFF_Q, NSA_WIDTH), col(OFF_KS, kvw), col(OFF_VS, kvw), col(OFF_KW, kvw), col(OFF_VW, kvw),
                  tab(), tab(), tab(), vec(), vec(), vec()],
        out_specs=[pl.BlockSpec((tt, NSA_WIDTH), lambda i: (i, 0)),
                   pl.BlockSpec((tt, NSA_WIDTH), lambda i: (i, 0)),
                   pl.BlockSpec((tt, kvw), lambda i: (i, 0)),
                   vt_spec(),
                   pl.BlockSpec((tt, kvw), lambda i: (i, 0)),
                   vt_spec()],
        out_shape=[jax.ShapeDtypeStruct((m, NSA_WIDTH), BF16),
                   jax.ShapeDtypeStruct((m, NSA_WIDTH), BF16),
                   jax.ShapeDtypeStruct((m, kvw), BF16),
                   jax.ShapeDtypeStruct((m // seq, NSA_KV_HEADS, VT_ROWS, seq), BF16),
                   jax.ShapeDtypeStruct((m, kvw), BF16),
                   jax.ShapeDtypeStruct((m // seq, NSA_KV_HEADS, VT_ROWS, seq), BF16)],
        compiler_params=_cparams(("parallel",)),
        name="nsa_prep",
    )(p, p, p, p, p, cosf, sina, sinb, qn, ksn, kwn)


def _nsa_cmp_kernel(k_ref, v_ref, pek_ref, w1k_ref, w2k_ref, pev_ref, w1v_ref, w2v_ref, kn_ref,
                    kc_ref, vc_ref):
    ng = k_ref.shape[0] // CMP_STRIDE
    rowid = lax.broadcasted_iota(jnp.int32, (ng, HEAD_DIM), 0)

    def compress(x_ref, pe_ref, w1_ref, w2_ref):
        first = jnp.zeros((ng, HEAD_DIM), F32)
        second = jnp.zeros((ng, HEAD_DIM), F32)
        for j in range(CMP_STRIDE):
            xj = x_ref[pl.ds(j, ng, stride=CMP_STRIDE), :]
            lo = (xj + pe_ref[j:j + 1, :]).astype(BF16)
            hi = (xj + pe_ref[CMP_STRIDE + j:CMP_STRIDE + j + 1, :]).astype(BF16)
            first += _dot(lo, w1_ref[j * HEAD_DIM:(j + 1) * HEAD_DIM, :])
            second += _dot(hi, w1_ref[(CMP_STRIDE + j) * HEAD_DIM:(CMP_STRIDE + j + 1) * HEAD_DIM, :])
        pre = first + pltpu.roll(second, ng - 1, 0)
        return _dot(_gelu_tanh(pre).astype(BF16), w2_ref[...])

    kc = _rms(compress(k_ref, pek_ref, w1k_ref, w2k_ref), kn_ref[...])
    vc = compress(v_ref, pev_ref, w1v_ref, w2v_ref)
    valid = rowid < ng - 1
    kc_ref[0, 0] = jnp.where(valid, kc, 0.0).astype(BF16)
    vc_ref[0, 0] = jnp.where(valid, vc, 0.0).T.astype(BF16)


def _nsa_cmp(p, batch, seq, pek, w1k, w2k, pev, w1v, w2v, kn):
    ng = seq // CMP_STRIDE
    hd = HEAD_DIM
    kvh = NSA_KV_HEADS

    def full(shape):
        return pl.BlockSpec(shape, lambda b, h: (0,) * len(shape))

    return pl.pallas_call(
        _nsa_cmp_kernel,
        grid=(batch, kvh),
        in_specs=[
            pl.BlockSpec((seq, hd), lambda b, h: (b, OFF_KC // hd + h)),
            pl.BlockSpec((seq, hd), lambda b, h: (b, OFF_VC // hd + h)),
            full((CMP_BLOCK, hd)), full((CMP_BLOCK * hd, hd)), full((hd, hd)),
            full((CMP_BLOCK, hd)), full((CMP_BLOCK * hd, hd)), full((hd, hd)),
            full((1, hd)),
        ],
        out_specs=[pl.BlockSpec((1, 1, ng, hd), lambda b, h: (b, h, 0, 0)),
                   pl.BlockSpec((1, 1, hd, ng), lambda b, h: (b, h, 0, 0))],
        out_shape=[jax.ShapeDtypeStruct((batch, kvh, ng, hd), BF16),
                   jax.ShapeDtypeStruct((batch, kvh, hd, ng), BF16)],
        compiler_params=_cparams(("parallel", "parallel")),
        name="nsa_cmp",
    )(p, p, pek, w1k, w2k, pev, w1v, w2v, kn)


def _nsa_attn_kernel(qc_ref, qr_ref, kc_ref, vct_ref, ks_ref, vst_ref, kw_ref, vwt_ref, gl_ref, gb_ref, ovl_ref,
                     o_ref, selb_ref, gt_ref, acc_ref, z_ref, b_ref, p_ref, *, tq, tk, n_sel):
    seq = ks_ref.shape[0]
    ncp = kc_ref.shape[2]
    ns = seq // SEL_BLOCK
    grp = NSA_GROUP
    hd = HEAD_DIM
    gq = grp * tq
    nblk = tk // SEL_BLOCK
    kv = pl.program_id(1)
    s0 = pl.program_id(2) * tq
    wspan = WINDOW + tq

    qc = jnp.concatenate([qc_ref[:, g * hd:(g + 1) * hd] for g in range(grp)], axis=0)
    qr = jnp.concatenate([qr_ref[:, g * hd:(g + 1) * hd] for g in range(grp)], axis=0)
    t_row = s0 + lax.broadcasted_iota(jnp.int32, (1, tq), 1)
    t_all = s0 + lax.broadcasted_iota(jnp.int32, (1, gq), 1) % tq

    def heads(x):
        return jnp.concatenate([x] * grp, axis=1)

    cend = lax.broadcasted_iota(jnp.int32, (ncp, 1), 0) * CMP_STRIDE + (CMP_BLOCK - 1)
    cmask = cend <= t_all
    zc = jnp.where(cmask, _dot_nt(kc_ref[0, 0], qc), -MASK_VALUE)
    e = jnp.where(cmask, jnp.exp2(zc - jnp.max(zc, axis=0, keepdims=True)), 0.0)
    den = jnp.sum(e, axis=0, keepdims=True)
    pc = e * (1.0 / jnp.where(den > 0.0, den, 1.0))
    o_cmp = _dot(vct_ref[0, 0], pc.astype(BF16))

    pcs = pc[:, 0:tq]
    for g in range(1, grp):
        pcs = pcs + pc[:, g * tq:(g + 1) * tq]
    pcs_hi = pcs.astype(BF16)
    pcs_lo = (pcs - pcs_hi.astype(F32)).astype(BF16)
    imp = _dot(ovl_ref[...], pcs_hi) + _dot(ovl_ref[...], pcs_lo)
    jrow = lax.broadcasted_iota(jnp.int32, (ns, tq), 0)
    cur = (s0 + lax.broadcasted_iota(jnp.int32, (ns, tq), 1)) // SEL_BLOCK
    forced = (jrow == 0) | (jrow == cur) | (jrow == cur - 1)
    imp = jnp.where(forced, MASK_VALUE, jnp.where(jrow <= cur, imp, -MASK_VALUE))
    ngrp = ns // 8
    imp_g = [imp[8 * r:8 * r + 8, :] for r in range(ngrp)]
    sub = lax.broadcasted_iota(jnp.int32, (8, tq), 0)
    rank_g = [jnp.zeros((8, tq), F32) for _ in range(ngrp)]
    for i in range(ns):
        other = jnp.broadcast_to(imp[i:i + 1, :], (8, tq))
        for r in range(ngrp):
            if 8 * r > i:
                ahead = jnp.where(other >= imp_g[r], 1.0, 0.0)
            elif 8 * r + 7 < i:
                ahead = jnp.where(other > imp_g[r], 1.0, 0.0)
            else:
                ahead = jnp.where(sub > i - 8 * r, jnp.where(other >= imp_g[r], 1.0, 0.0),
                                  jnp.where(other > imp_g[r], 1.0, 0.0))
            rank_g[r] = rank_g[r] + ahead
    for r in range(ngrp):
        selb_ref[8 * r:8 * r + 8, :] = jnp.where(rank_g[r] < n_sel, 0.0, -MASK_VALUE)

    acc_ref[...] = jnp.zeros_like(acc_ref)

    n_kt = (s0 + tq + tk - 1) // tk

    def scores(kt):
        return _dot_nt(ks_ref[pl.ds(pl.multiple_of(kt * tk, tk), tk), :], qr)

    n_pairs = (n_kt + 1) // 2
    z_ref[0] = scores(0)
    srow = lax.broadcasted_iota(jnp.int32, (SEL_BLOCK, 1), 0)

    def tile_step(kt, m, slot):
        z_ref[1 - slot] = scores(jnp.minimum(kt + 1, 2 * n_pairs - 1))
        koff = pl.multiple_of(kt * tk, tk)
        zs = z_ref.at[slot]
        mx = jnp.full((8, gq), -MASK_VALUE, F32)
        for j in range(nblk):
            rs = slice(j * SEL_BLOCK, (j + 1) * SEL_BLOCK)
            bias = (jnp.broadcast_to(selb_ref[pl.ds(kt * nblk + j, 1), :], (SEL_BLOCK, tq))
                    + jnp.where(koff + j * SEL_BLOCK + srow <= t_row, 0.0, -MASK_VALUE))
            b_ref[rs, :] = bias
            zb = zs[rs, :] + heads(bias)
            mx = jnp.maximum(mx, jnp.max(zb.reshape(SEL_BLOCK // 8, 8, gq), axis=0))
        m_new = jnp.maximum(m, jnp.max(mx, axis=0, keepdims=True))
        for j in range(nblk):
            rs = slice(j * SEL_BLOCK, (j + 1) * SEL_BLOCK)
            p_ref[rs, :] = jnp.exp2(zs[rs, :] + heads(b_ref[rs, :]) - m_new).astype(BF16)
        acc_ref[...] = acc_ref[...] * jnp.exp2(m - m_new) + _dot(vst_ref[0, 0, :, pl.ds(koff, tk)], p_ref[...])
        return m_new

    def pair_body(pi, m):
        return tile_step(2 * pi + 1, tile_step(2 * pi, m, 0), 1)

    lax.fori_loop(0, n_pairs, pair_body, jnp.full((1, gq), -MASK_VALUE, F32))
    o_sel = acc_ref[0:hd, :] * (1.0 / acc_ref[hd:hd + 1, :])

    start = pl.multiple_of(jnp.maximum(s0 - WINDOW, 0), tq)
    kpos = start + lax.broadcasted_iota(jnp.int32, (wspan, 1), 0)
    wbias = jnp.where((kpos <= t_row) & (kpos > t_row - WINDOW), 0.0, -MASK_VALUE)
    zw = _dot_nt(kw_ref[pl.ds(start, wspan), :], qr) + heads(wbias)
    pw = jnp.exp2(zw - jnp.max(zw, axis=0, keepdims=True)).astype(BF16)
    ow = _dot(vwt_ref[0, 0, :, pl.ds(start, wspan)], pw)
    o_win = ow[0:hd, :] * (1.0 / ow[hd:hd + 1, :])

    gt_ref[...] = _sigmoid(gl_ref[...] + gb_ref[...]).T
    for g in range(grp):
        base = MISC_GATE + (kv * grp + g) * 3
        cs = slice(g * tq, (g + 1) * tq)
        og = (gt_ref[pl.ds(base, 1), :] * o_cmp[:, cs] + gt_ref[pl.ds(base + 1, 1), :] * o_sel[:, cs]
              + gt_ref[pl.ds(base + 2, 1), :] * o_win[:, cs])
        o_ref[:, g * hd:(g + 1) * hd] = og.T


def _nsa_attn(p, qc, qr, kc, vct, ks, vst, kw, vwt, gate_b, ovl, batch, seq, tq, tk):
    hd = HEAD_DIM
    gw = NSA_GROUP * hd
    nq = seq // tq
    ns = seq // SEL_BLOCK
    ncp = kc.shape[2]
    n_sel = min(N_SELECT, ns)
    assert seq % (2 * tk) == 0 and tk % SEL_BLOCK == 0 and seq >= WINDOW + tq and WINDOW % tq == 0

    def qspec():
        return pl.BlockSpec((tq, gw), lambda b, h, i: (b * nq + i, h))

    def kspec():
        return pl.BlockSpec((seq, hd), lambda b, h, i: (b, h))

    def vtspec():
        return pl.BlockSpec((1, 1, VT_ROWS, seq), lambda b, h, i: (b, h, 0, 0))

    return pl.pallas_call(
        functools.partial(_nsa_attn_kernel, tq=tq, tk=tk, n_sel=n_sel),
        grid=(batch, NSA_KV_HEADS, nq),
        in_specs=[qspec(), qspec(),
                  pl.BlockSpec((1, 1, ncp, hd), lambda b, h, i: (b, h, 0, 0)),
                  pl.BlockSpec((1, 1, hd, ncp), lambda b, h, i: (b, h, 0, 0)),
                  kspec(), vtspec(), kspec(), vtspec(),
                  pl.BlockSpec((tq, LANES), lambda b, h, i: (b * nq + i, OFF_MISC // LANES)),
                  pl.BlockSpec((1, LANES), lambda b, h, i: (0, 0)),
                  pl.BlockSpec((ns, ncp), lambda b, h, i: (0, 0))],
        out_specs=pl.BlockSpec((tq, gw), lambda b, h, i: (b * nq + i, h)),
        out_shape=jax.ShapeDtypeStruct((batch * seq, NSA_WIDTH), F32),
        scratch_shapes=[pltpu.VMEM((ns, tq), F32), pltpu.VMEM((LANES, tq), F32),
                        pltpu.VMEM((VT_ROWS, NSA_GROUP * tq), F32),
                        pltpu.VMEM((2, tk, NSA_GROUP * tq), F32), pltpu.VMEM((tk, tq), F32),
                        pltpu.VMEM((tk, NSA_GROUP * tq), BF16)],
        compiler_params=_cparams(("parallel", "parallel", "arbitrary")),
        name="nsa_attn",
    )(qc, qr, kc, vct, ks, vst, kw, vwt, p, gate_b, ovl)


def _gla_kernel(q_ref, k_ref, v_ref, g_ref, misc_ref, a2_ref, ab_ref, gn_ref, o_ref,
                st_ref, qe_ref, kd_ref, vb_ref, ds_ref, oi_ref):
    tt, wd = q_ref.shape
    step = GLA_STEP
    ng = tt // step
    nh = GLA_HEADS
    hl = LANES

    @pl.when(pl.program_id(1) == 0)
    def _():
        st_ref[...] = jnp.zeros_like(st_ref)

    x = _dot(misc_ref[...].astype(BF16), a2_ref[...]) + ab_ref[...]
    la = (-_softplus(-x)) * np.float32(1.0 / GLA_TAU)
    pos = lax.broadcasted_iota(jnp.int32, (tt, wd), 0) % step
    b = la
    s = 1
    while s < step:
        b = b + jnp.where(pos >= s, pltpu.roll(b, s, 0), 0.0)
        s *= 2

    q3 = (q_ref[...] * np.float32(GLA_DK ** -0.5)).reshape(ng, step, wd)
    k3 = k_ref[...].reshape(ng, step, wd)
    v3 = v_ref[...].reshape(ng, step, wd)
    b3 = b.reshape(ng, step, wd)
    pos3 = pos.reshape(ng, step, wd)
    b_last = b3[:, step - 1:step, :]
    qe_ref[...] = (q3 * jnp.exp(b3)).reshape(tt, wd).astype(BF16)
    kd_ref[...] = (k3 * jnp.exp(b_last - b3)).reshape(tt, wd).astype(BF16)
    vb_ref[...] = v_ref[...].astype(BF16)
    ds_ref[...] = jnp.exp(b_last)

    ones = jnp.ones((hl, hl), BF16)
    odiag = jnp.zeros((tt, wd), F32)
    for j in range(step):
        dec = jnp.exp(jnp.where(pos3 >= j, b3 - b3[:, j:j + 1, :], -MASK_VALUE))
        term = (q3 * k3[:, j:j + 1, :] * dec).reshape(tt, wd).astype(BF16)
        vj = jnp.broadcast_to(v3[:, j:j + 1, :], (ng, step, wd)).reshape(tt, wd)
        aij = jnp.concatenate([_dot(term[:, h * hl:(h + 1) * hl], ones) for h in range(nh)], axis=1)
        odiag = odiag + aij * vj

    def body(gi, carry):
        r0 = pl.multiple_of(gi * step, step)
        dsg = ds_ref[gi]
        for h in range(nh):
            sl = slice(h * hl, (h + 1) * hl)
            st = st_ref[h]
            oi_ref[pl.ds(r0, step), sl] = _dot_nt(qe_ref[pl.ds(r0, step), sl], st.astype(BF16))
            upd = _dot_tn(vb_ref[pl.ds(r0, step), sl], kd_ref[pl.ds(r0, step), sl])
            st_ref[h] = st * dsg[:, sl] + upd
        return carry

    lax.fori_loop(0, ng, body, 0)

    o = oi_ref[...] + odiag
    gate = g_ref[...]
    gate = gate * _sigmoid(gate)
    for h in range(nh):
        sl = slice(h * hl, (h + 1) * hl)
        o_ref[:, sl] = (_rms(o[:, sl], gn_ref[...]) * gate[:, sl]).astype(o_ref.dtype)


def _gla(p, batch, seq, a2, ab, gn, tt):
    wd = GLA_HEADS * LANES
    nt = seq // tt
    ng = tt // GLA_STEP

    def col(off, width):
        return pl.BlockSpec((tt, width), lambda b, t: (b * nt + t, off // width))

    return pl.pallas_call(
        _gla_kernel,
        grid=(batch, nt),
        in_specs=[col(OFF_GQ, wd), col(OFF_GK, wd), col(OFF_GV, wd), col(OFF_GG, wd), col(OFF_MISC, LANES),
                  pl.BlockSpec((LANES, wd), lambda b, t: (0, 0)),
                  pl.BlockSpec((1, wd), lambda b, t: (0, 0)),
                  pl.BlockSpec((1, GLA_DV), lambda b, t: (0, 0))],
        out_specs=pl.BlockSpec((tt, wd), lambda b, t: (b * nt + t, 0)),
        out_shape=jax.ShapeDtypeStruct((batch * seq, wd), BF16),
        scratch_shapes=[
            pltpu.VMEM((GLA_HEADS, GLA_DV, LANES), F32),
            pltpu.VMEM((tt, wd), BF16),
            pltpu.VMEM((tt, wd), BF16),
            pltpu.VMEM((tt, wd), BF16),
            pltpu.VMEM((ng, 1, wd), F32),
            pltpu.VMEM((tt, wd), F32),
        ],
        compiler_params=_cparams(("parallel", "arbitrary")),
        name="gla",
    )(p, p, p, p, p, a2, ab, gn)


def _pad_heads(w, heads, width):
    lead = w.shape[:-1]
    w = w.reshape(lead + (heads, width))
    w = jnp.pad(w, [(0, 0)] * len(lead) + [(0, 0), (0, LANES - width)])
    return w.reshape(lead + (heads * LANES,))


def _pack_w_in(w_in):
    d = w_in.shape[0]
    o_gate = OFF_VW + NSA_KV_WIDTH
    o_gq = o_gate + NSA_HEADS * 3
    o_gk = o_gq + GLA_QK_WIDTH
    o_gv = o_gk + GLA_QK_WIDTH
    o_ga = o_gv + 2 * GLA_WIDTH
    misc = jnp.concatenate([w_in[:, o_gate:o_gq], w_in[:, o_ga:o_ga + GLA_RANK],
                            jnp.zeros((d, LANES - NSA_HEADS * 3 - GLA_RANK), w_in.dtype)], axis=1)
    packed = jnp.concatenate([w_in[:, :o_gate],
                              _pad_heads(w_in[:, o_gq:o_gk], GLA_HEADS, GLA_DK),
                              _pad_heads(w_in[:, o_gk:o_gv], GLA_HEADS, GLA_DK),
                              w_in[:, o_gv:o_ga], misc], axis=1)
    assert packed.shape[1] == N_PACKED
    return packed.astype(BF16)


def _block_diag(w):
    nb, bi, bo = w.shape
    eye = jnp.eye(nb, dtype=w.dtype)
    return (eye[:, None, :, None] * w[:, :, None, :]).reshape(nb * bi, nb * bo)


def _rope_lane_tables(seq):
    inv = 1.0 / (ROPE_THETA ** (jnp.arange(0, ROPE_DIM, 2, dtype=F32) / ROPE_DIM))
    ang = jnp.arange(seq, dtype=F32)[:, None] * inv[None, :]
    cos, sin = jnp.cos(ang), jnp.sin(ang)
    rest = LANES - ROPE_DIM
    zeros_h = jnp.zeros((seq, ROPE_HALF), F32)
    cosf = jnp.concatenate([cos, cos, jnp.ones((seq, rest), F32)], axis=1)
    sina = jnp.concatenate([-sin, zeros_h, jnp.zeros((seq, rest), F32)], axis=1)
    sinb = jnp.concatenate([zeros_h, sin, jnp.zeros((seq, rest), F32)], axis=1)
    return cosf, sina, sinb


def _overlap_t(seq, ncp):
    ns = seq // SEL_BLOCK
    nc = (seq - CMP_BLOCK) // CMP_STRIDE + 1
    cs = np.arange(ncp) * CMP_STRIDE
    ss = np.arange(ns) * SEL_BLOCK
    ov = (cs[None, :] < ss[:, None] + SEL_BLOCK) & (cs[None, :] + CMP_BLOCK > ss[:, None]) & (np.arange(ncp) < nc)[None, :]
    return jnp.asarray(ov.astype(np.float32), dtype=BF16)


def _row(v):
    return v.reshape(1, -1)


def _mixers(p, batch, seq, l, prm, tables, ovl_t):
    cosf, sina, sinb = tables
    wg = jnp.concatenate([_block_diag(prm['lru_gate_a_w'][l]), _block_diag(prm['lru_gate_x_w'][l])], axis=1).astype(BF16)
    bg = jnp.concatenate([prm['lru_gate_a_b'][l], prm['lru_gate_x_b'][l]]).reshape(1, -1)
    ya = _lru(p, batch, seq, prm['lru_conv_w'][l], _row(prm['lru_conv_b'][l]), wg, bg,
              _row(prm['lru_lambda'][l]), _row(prm['lru_out_norm'][l]), tt=min(512, seq))
    qc, qr, ks, vs, kw, vw = _nsa_prep(p, seq, cosf, sina, sinb, _row(prm['nsa_q_norm'][l]),
                                       _row(prm['nsa_k_sel_norm'][l]), _row(prm['nsa_k_win_norm'][l]),
                                       tt=min(512, seq))
    kc, vc = _nsa_cmp(p, batch, seq, prm['nsa_cmp_pe_k'][l], prm['nsa_cmp_w1_k'][l].astype(BF16),
                      prm['nsa_cmp_w2_k'][l].astype(BF16), prm['nsa_cmp_pe_v'][l],
                      prm['nsa_cmp_w1_v'][l].astype(BF16), prm['nsa_cmp_w2_v'][l].astype(BF16),
                      _row(prm['nsa_k_cmp_norm'][l]))
    gate_b = jnp.pad(prm['nsa_gate_b'][l], (MISC_GATE, LANES - MISC_GATE - NSA_HEADS * 3)).reshape(1, LANES)
    yb = _nsa_attn(p, qc, qr, kc, vc, ks, vs, kw, vw, gate_b, ovl_t, batch, seq, tq=128, tk=min(512, seq))
    a2 = jnp.zeros((LANES, GLA_HEADS * LANES), F32).at[MISC_GA:MISC_GA + GLA_RANK].set(
        _pad_heads(prm['gla_a_w2'][l], GLA_HEADS, GLA_DK)).astype(BF16)
    ab = _pad_heads(prm['gla_a_b'][l], GLA_HEADS, GLA_DK).reshape(1, -1)
    yc = _gla(p, batch, seq, a2, ab, _row(prm['gla_out_norm'][l]), tt=min(256, seq))
    return ya, yb, yc


def kernel(x, ffn1_norm, ffn1_w_gate, ffn1_w_up, ffn1_w_down, mix_norm, w_in, lru_conv_w, lru_conv_b, lru_gate_a_w, lru_gate_a_b, lru_gate_x_w, lru_gate_x_b, lru_lambda, lru_out_norm, nsa_q_norm, nsa_k_cmp_norm, nsa_k_sel_norm, nsa_k_win_norm, nsa_cmp_pe_k, nsa_cmp_w1_k, nsa_cmp_w2_k, nsa_cmp_pe_v, nsa_cmp_w1_v, nsa_cmp_w2_v, nsa_gate_b, nsa_out_norm, gla_a_w2, gla_a_b, gla_out_norm, w_out, ffn2_norm, ffn2_w_gate, ffn2_w_up, ffn2_w_down):
    prm = dict(lru_conv_w=lru_conv_w, lru_conv_b=lru_conv_b, lru_gate_a_w=lru_gate_a_w, lru_gate_a_b=lru_gate_a_b,
               lru_gate_x_w=lru_gate_x_w, lru_gate_x_b=lru_gate_x_b, lru_lambda=lru_lambda, lru_out_norm=lru_out_norm,
               nsa_q_norm=nsa_q_norm, nsa_k_cmp_norm=nsa_k_cmp_norm, nsa_k_sel_norm=nsa_k_sel_norm,
               nsa_k_win_norm=nsa_k_win_norm, nsa_cmp_pe_k=nsa_cmp_pe_k, nsa_cmp_w1_k=nsa_cmp_w1_k,
               nsa_cmp_w2_k=nsa_cmp_w2_k, nsa_cmp_pe_v=nsa_cmp_pe_v, nsa_cmp_w1_v=nsa_cmp_w1_v,
               nsa_cmp_w2_v=nsa_cmp_w2_v, nsa_gate_b=nsa_gate_b, gla_a_w2=gla_a_w2, gla_a_b=gla_a_b,
               gla_out_norm=gla_out_norm)
    batch, seq, d = x.shape
    depth = w_in.shape[0]
    m = batch * seq
    tables = _rope_lane_tables(seq)
    ncp = seq // CMP_STRIDE
    ovl_t = _overlap_t(seq, ncp)
    tm = min(512, m)
    tf = 512
    xf = x.reshape(m, d)
    for l in range(depth):
        xf = _ffn(xf, _row(ffn1_norm[l]), ffn1_w_gate[l].astype(BF16), ffn1_w_up[l].astype(BF16),
                  ffn1_w_down[l].astype(BF16), tm, tf)
        p = _mix_in(xf, _row(mix_norm[l]), _pack_w_in(w_in[l]), tm, N_PACKED // 5)
        ya, yb, yc = _mixers(p, batch, seq, l, prm, tables, ovl_t)
        xf = _mix_out(xf, ya, yb, yc, _row(nsa_out_norm[l]), w_out[l].astype(BF16), tm)
        xf = _ffn(xf, _row(ffn2_norm[l]), ffn2_w_gate[l].astype(BF16), ffn2_w_up[l].astype(BF16),
                  ffn2_w_down[l].astype(BF16), tm, tf)
    return xf.reshape(batch, seq, d)
```

```python
import functools

import numpy as np
import jax
import jax.numpy as jnp
from jax import lax
from jax.experimental import pallas as pl
from jax.experimental.pallas import tpu as pltpu

F32 = jnp.float32
BF16 = jnp.bfloat16

NORM_EPS = 1e-6
D_MODEL = 2048
D_FF = 5632

LRU_WIDTH = 512
LRU_BLOCKS = 8
LRU_BLOCK_SIZE = LRU_WIDTH // LRU_BLOCKS
CONV_WIDTH = 4
LRU_C = 8.0

HEAD_DIM = 128
NSA_HEADS = 8
NSA_KV_HEADS = 2
NSA_GROUP = NSA_HEADS // NSA_KV_HEADS
NSA_WIDTH = NSA_HEADS * HEAD_DIM
NSA_KV_WIDTH = NSA_KV_HEADS * HEAD_DIM
ROPE_DIM = HEAD_DIM // 4
ROPE_HALF = ROPE_DIM // 2
ROPE_THETA = 500000.0
CMP_BLOCK = 32
CMP_STRIDE = 16
SEL_BLOCK = 64
N_SELECT = 16
WINDOW = 512
MASK_VALUE = 1e30
QK_LOG2_SCALE = float(HEAD_DIM ** -0.5 * np.log2(np.e))
VT_ROWS = HEAD_DIM + 16

GLA_DV = 128
GLA_HEADS = 4
GLA_DK = 64
GLA_WIDTH = GLA_HEADS * GLA_DV
GLA_QK_WIDTH = GLA_HEADS * GLA_DK
GLA_RANK = 16
GLA_TAU = 16.0
GLA_STEP = 16

LANES = 128

OFF_LRU_X = 0
OFF_LRU_Y = OFF_LRU_X + LRU_WIDTH
OFF_Q = OFF_LRU_Y + LRU_WIDTH
OFF_KC = OFF_Q + NSA_WIDTH
OFF_VC = OFF_KC + NSA_KV_WIDTH
OFF_KS = OFF_VC + NSA_KV_WIDTH
OFF_VS = OFF_KS + NSA_KV_WIDTH
OFF_KW = OFF_VS + NSA_KV_WIDTH
OFF_VW = OFF_KW + NSA_KV_WIDTH
OFF_GQ = OFF_VW + NSA_KV_WIDTH
OFF_GK = OFF_GQ + GLA_QK_WIDTH
OFF_GV = OFF_GK + GLA_QK_WIDTH
OFF_GG = OFF_GV + GLA_WIDTH
OFF_MISC = OFF_GG + GLA_WIDTH
MIX_IN_TILES = 3
N_PACKED = -(-(OFF_MISC + LANES) // (2 * LANES * MIX_IN_TILES)) * (2 * LANES * MIX_IN_TILES)
MISC_GATE = 0
MISC_GA = NSA_HEADS * 3

VMEM_LIMIT = 56 * 1024 * 1024


def _cparams(sem):
    return pltpu.CompilerParams(dimension_semantics=sem, vmem_limit_bytes=VMEM_LIMIT)


def _sigmoid(x):
    return 1.0 / (1.0 + jnp.exp(-x))


def _softplus(x):
    return jnp.maximum(x, 0.0) + jnp.log1p(jnp.exp(-jnp.abs(x)))


def _gelu_tanh(x):
    c = np.float32(np.sqrt(2.0 / np.pi))
    return x * (0.5 * (1.0 + jnp.tanh(c * (x + 0.044715 * (x * x * x)))))


def _rms(x, g):
    return x * lax.rsqrt(jnp.mean(x * x, axis=-1, keepdims=True) + NORM_EPS) * g


def _dot(a, b):
    return jnp.dot(a, b, preferred_element_type=F32)


def _dot_nt(a, b):
    return lax.dot_general(a, b, (((1,), (1,)), ((), ())), preferred_element_type=F32)


def _dot_tn(a, b):
    return lax.dot_general(a, b, (((0,), (0,)), ((), ())), preferred_element_type=F32)


def _ffn_kernel(x_ref, g_ref, wg_ref, wu_ref, wd_ref, o_ref, h_ref):
    j = pl.program_id(1)

    @pl.when(j == 0)
    def _():
        h_ref[...] = _rms(x_ref[...], g_ref[...]).astype(BF16)
        o_ref[...] = jnp.zeros_like(o_ref)

    h = h_ref[...]
    gate = _dot(h, wg_ref[...])
    up = _dot(h, wu_ref[...])
    act = (gate * _sigmoid(gate)) * up
    o_ref[...] += _dot(act.astype(BF16), wd_ref[...])

    @pl.when(j == pl.num_programs(1) - 1)
    def _():
        o_ref[...] = x_ref[...] + 0.5 * o_ref[...]


def _ffn(x, g, wg, wu, wd, layer, tm, tf):
    m, d = x.shape
    f = wg.shape[2]
    return pl.pallas_call(
        _ffn_kernel,
        grid=(m // tm, f // tf),
        in_specs=[
            pl.BlockSpec((tm, d), lambda i, j: (i, 0)),
            pl.BlockSpec((1, d), lambda i, j: (0, 0)),
            pl.BlockSpec((None, d, tf), lambda i, j: (layer, 0, j)),
            pl.BlockSpec((None, d, tf), lambda i, j: (layer, 0, j)),
            pl.BlockSpec((None, tf, d), lambda i, j: (layer, j, 0)),
        ],
        out_specs=pl.BlockSpec((tm, d), lambda i, j: (i, 0)),
        out_shape=jax.ShapeDtypeStruct((m, d), F32),
        scratch_shapes=[pltpu.VMEM((tm, d), BF16)],
        compiler_params=_cparams(("parallel", "arbitrary")),
        name="ffn",
    )(x, g, wg, wu, wd)


def _mix_in_kernel(x_ref, g_ref, w_ref, o_ref, h_ref):
    @pl.when(pl.program_id(1) == 0)
    def _():
        h_ref[...] = _rms(x_ref[...], g_ref[...]).astype(BF16)

    o_ref[...] = _dot(h_ref[...], w_ref[...])


def _mix_in(x, g, w, tm, tn):
    m, d = x.shape
    n = w.shape[1]
    return pl.pallas_call(
        _mix_in_kernel,
        grid=(m // tm, n // tn),
        in_specs=[
            pl.BlockSpec((tm, d), lambda i, j: (i, 0)),
            pl.BlockSpec((1, d), lambda i, j: (0, 0)),
            pl.BlockSpec((d, tn), lambda i, j: (0, j)),
        ],
        out_specs=pl.BlockSpec((tm, tn), lambda i, j: (i, j)),
        out_shape=jax.ShapeDtypeStruct((m, n), F32),
        scratch_shapes=[pltpu.VMEM((tm, d), BF16)],
        compiler_params=_cparams(("parallel", "arbitrary")),
        name="mix_in",
    )(x, g, w)


def _mix_out_kernel(x_ref, ya_ref, yb_ref, yc_ref, gb_ref, wa_ref, wb_ref, wc_ref, o_ref):
    nb = _rms(yb_ref[...], gb_ref[...]).astype(BF16)
    acc = _dot(ya_ref[...], wa_ref[...])
    acc += _dot(nb, wb_ref[...])
    acc += _dot(yc_ref[...], wc_ref[...])
    o_ref[...] = x_ref[...] + acc


def _mix_out(x, ya, yb, yc, gb, w_out, tm):
    m, d = x.shape
    wa, wb, wc = LRU_WIDTH, NSA_WIDTH, GLA_WIDTH
    return pl.pallas_call(
        _mix_out_kernel,
        grid=(m // tm,),
        in_specs=[
            pl.BlockSpec((tm, d), lambda i: (i, 0)),
            pl.BlockSpec((tm, wa), lambda i: (i, 0)),
            pl.BlockSpec((tm, wb), lambda i: (i, 0)),
            pl.BlockSpec((tm, wc), lambda i: (i, 0)),
            pl.BlockSpec((1, wb), lambda i: (0, 0)),
            pl.BlockSpec((wa, d), lambda i: (0, 0)),
            pl.BlockSpec((wb, d), lambda i: (0, 0)),
            pl.BlockSpec((wc, d), lambda i: (0, 0)),
        ],
        out_specs=pl.BlockSpec((tm, d), lambda i: (i, 0)),
        out_shape=jax.ShapeDtypeStruct((m, d), F32),
        compiler_params=_cparams(("parallel",)),
        name="mix_out",
    )(x, ya, yb, yc, gb, w_out[:wa], w_out[wa:wa + wb], w_out[wa + wb:])


def _lru_kernel(u_ref, y_ref, cw_ref, cb_ref, wg_ref, bg_ref, lam_ref, gn_ref, o_ref,
                ext_ref, a_ref, b_ref, hc_ref):
    tt, w = u_ref.shape

    @pl.when(pl.program_id(1) == 0)
    def _():
        ext_ref[0:8, :] = jnp.zeros((8, w), F32)
        hc_ref[...] = jnp.zeros_like(hc_ref)

    u = u_ref[...]
    ext_ref[8:8 + tt, :] = u
    xc = cb_ref[...] + cw_ref[CONV_WIDTH - 1:CONV_WIDTH, :] * u
    for k in range(CONV_WIDTH - 1):
        sh = CONV_WIDTH - 1 - k
        xc = xc + cw_ref[k:k + 1, :] * ext_ref[8 - sh:8 - sh + tt, :]
    ext_ref[0:8, :] = u[tt - 8:tt, :]

    gates = _dot(xc.astype(BF16), wg_ref[...]) + bg_ref[...]
    r = _sigmoid(gates[:, :w])
    i = _sigmoid(gates[:, w:])
    log_a = (-LRU_C) * r * _softplus(-lam_ref[...])
    a = jnp.exp(log_a)
    a_ref[...] = a
    b_ref[...] = jnp.sqrt(-jnp.tanh(log_a) * (a * a + 1.0)) * i * xc

    row = lax.broadcasted_iota(jnp.int32, (8, w), 0)

    def body(gi, hprev):
        off = pl.multiple_of(gi * 8, 8)
        a8 = a_ref[pl.ds(off, 8), :]
        b8 = b_ref[pl.ds(off, 8), :]
        for s in (1, 2, 4):
            keep = row >= s
            b8 = b8 + a8 * jnp.where(keep, pltpu.roll(b8, s, 0), 0.0)
            a8 = a8 * jnp.where(keep, pltpu.roll(a8, s, 0), 1.0)
        h8 = b8 + a8 * hprev
        b_ref[pl.ds(off, 8), :] = h8
        return jnp.broadcast_to(h8[7:8, :], (8, w))

    hc_ref[...] = lax.fori_loop(0, tt // 8, body, hc_ref[...])

    ya = b_ref[...] * _gelu_tanh(y_ref[...])
    o_ref[...] = _rms(ya, gn_ref[...]).astype(o_ref.dtype)


def _lru(p, batch, seq, cw, cb, wg, bg, lam, gn, tt):
    w = LRU_WIDTH
    nt = seq // tt
    return pl.pallas_call(
        _lru_kernel,
        grid=(batch, nt),
        in_specs=[
            pl.BlockSpec((tt, w), lambda b, t: (b * nt + t, OFF_LRU_X // w)),
            pl.BlockSpec((tt, w), lambda b, t: (b * nt + t, OFF_LRU_Y // w)),
            pl.BlockSpec((CONV_WIDTH, w), lambda b, t: (0, 0)),
            pl.BlockSpec((1, w), lambda b, t: (0, 0)),
            pl.BlockSpec((w, 2 * w), lambda b, t: (0, 0)),
            pl.BlockSpec((1, 2 * w), lambda b, t: (0, 0)),
            pl.BlockSpec((1, w), lambda b, t: (0, 0)),
            pl.BlockSpec((1, w), lambda b, t: (0, 0)),
        ],
        out_specs=pl.BlockSpec((tt, w), lambda b, t: (b * nt + t, 0)),
        out_shape=jax.ShapeDtypeStruct((batch * seq, w), BF16),
        scratch_shapes=[
            pltpu.VMEM((tt + 8, w), F32),
            pltpu.VMEM((tt, w), F32),
            pltpu.VMEM((tt, w), F32),
            pltpu.VMEM((8, w), F32),
        ],
        compiler_params=_cparams(("parallel", "arbitrary")),
        name="lru",
    )(p, p, cw, cb, wg, bg, lam, gn)


def _rope(x, cosf, sina, sinb):
    return x * cosf + pltpu.roll(x, LANES - ROPE_HALF, 1) * sina + pltpu.roll(x, ROPE_HALF, 1) * sinb


def _nsa_prep_kernel(q_ref, ks_ref, vs_ref, kw_ref, vw_ref, cos_ref, sina_ref, sinb_ref,
                     qn_ref, ksn_ref, kwn_ref,
                     qc_ref, qr_ref, kso_ref, vso_ref, kwo_ref, vwo_ref):
    cosf, sina, sinb = cos_ref[...], sina_ref[...], sinb_ref[...]
    for h in range(NSA_HEADS):
        sl = slice(h * HEAD_DIM, (h + 1) * HEAD_DIM)
        qh = _rms(q_ref[:, sl], qn_ref[...])
        qc_ref[:, sl] = (qh * QK_LOG2_SCALE).astype(BF16)
        qr_ref[:, sl] = (_rope(qh, cosf, sina, sinb) * QK_LOG2_SCALE).astype(BF16)
    for h in range(NSA_KV_HEADS):
        sl = slice(h * HEAD_DIM, (h + 1) * HEAD_DIM)
        kso_ref[:, sl] = _rope(_rms(ks_ref[:, sl], ksn_ref[...]), cosf, sina, sinb).astype(BF16)
        kwo_ref[:, sl] = _rope(_rms(kw_ref[:, sl], kwn_ref[...]), cosf, sina, sinb).astype(BF16)
        ones = jnp.ones((VT_ROWS - HEAD_DIM, vs_ref.shape[0]), BF16)
        vso_ref[0, h, 0:HEAD_DIM, :] = vs_ref[:, sl].T.astype(BF16)
        vso_ref[0, h, HEAD_DIM:VT_ROWS, :] = ones
        vwo_ref[0, h, 0:HEAD_DIM, :] = vw_ref[:, sl].T.astype(BF16)
        vwo_ref[0, h, HEAD_DIM:VT_ROWS, :] = ones


def _nsa_prep(p, seq, cosf, sina, sinb, qn, ksn, kwn, tt):
    m = p.shape[0]
    nt = seq // tt
    kvw = NSA_KV_WIDTH

    def col(off, width):
        return pl.BlockSpec((tt, width), lambda i: (i, off // width))

    def tab():
        return pl.BlockSpec((tt, LANES), lambda i: (i % nt, 0))

    def vec():
        return pl.BlockSpec((1, HEAD_DIM), lambda i: (0, 0))

    def vt_spec():
        return pl.BlockSpec((1, NSA_KV_HEADS, VT_ROWS, tt), lambda i: (i // nt, 0, 0, i % nt))

    return pl.pallas_call(
        _nsa_prep_kernel,
        grid=(m // tt,),
        in_specs=[col(OFF_Q, NSA_WIDTH), col(OFF_KS, kvw), col(OFF_VS, kvw), col(OFF_KW, kvw), col(OFF_VW, kvw),
                  tab(), tab(), tab(), vec(), vec(), vec()],
        out_specs=[pl.BlockSpec((tt, NSA_WIDTH), lambda i: (i, 0)),
                   pl.BlockSpec((tt, NSA_WIDTH), lambda i: (i, 0)),
                   pl.BlockSpec((tt, kvw), lambda i: (i, 0)),
                   vt_spec(),
                   pl.BlockSpec((tt, kvw), lambda i: (i, 0)),
                   vt_spec()],
        out_shape=[jax.ShapeDtypeStruct((m, NSA_WIDTH), BF16),
                   jax.ShapeDtypeStruct((m, NSA_WIDTH), BF16),
                   jax.ShapeDtypeStruct((m, kvw), BF16),
                   jax.ShapeDtypeStruct((m // seq, NSA_KV_HEADS, VT_ROWS, seq), BF16),
                   jax.ShapeDtypeStruct((m, kvw), BF16),
                   jax.ShapeDtypeStruct((m // seq, NSA_KV_HEADS, VT_ROWS, seq), BF16)],
        compiler_params=_cparams(("parallel",)),
        name="nsa_prep",
    )(p, p, p, p, p, cosf, sina, sinb, qn, ksn, kwn)


def _nsa_cmp_kernel(k_ref, v_ref, pek_ref, w1k_ref, w2k_ref, pev_ref, w1v_ref, w2v_ref, kn_ref,
                    kc_ref, vc_ref):
    ng = k_ref.shape[0] // CMP_STRIDE
    rowid = lax.broadcasted_iota(jnp.int32, (ng, HEAD_DIM), 0)

    def compress(x_ref, pe_ref, w1_ref, w2_ref):
        first = jnp.zeros((ng, HEAD_DIM), F32)
        second = jnp.zeros((ng, HEAD_DIM), F32)
        for j in range(CMP_STRIDE):
            xj = x_ref[pl.ds(j, ng, stride=CMP_STRIDE), :]
            lo = (xj + pe_ref[j:j + 1, :]).astype(BF16)
            hi = (xj + pe_ref[CMP_STRIDE + j:CMP_STRIDE + j + 1, :]).astype(BF16)
            first += _dot(lo, w1_ref[j * HEAD_DIM:(j + 1) * HEAD_DIM, :])
            second += _dot(hi, w1_ref[(CMP_STRIDE + j) * HEAD_DIM:(CMP_STRIDE + j + 1) * HEAD_DIM, :])
        pre = first + pltpu.roll(second, ng - 1, 0)
        return _dot(_gelu_tanh(pre).astype(BF16), w2_ref[...])

    kc = _rms(compress(k_ref, pek_ref, w1k_ref, w2k_ref), kn_ref[...])
    vc = compress(v_ref, pev_ref, w1v_ref, w2v_ref)
    valid = rowid < ng - 1
    kc_ref[0, 0] = jnp.where(valid, kc, 0.0).astype(BF16)
    vc_ref[0, 0] = jnp.where(valid, vc, 0.0).T.astype(BF16)


def _nsa_cmp(p, batch, seq, pek, w1k, w2k, pev, w1v, w2v, kn):
    ng = seq // CMP_STRIDE
    hd = HEAD_DIM
    kvh = NSA_KV_HEADS

    def full(shape):
        return pl.BlockSpec(shape, lambda b, h: (0,) * len(shape))

    return pl.pallas_call(
        _nsa_cmp_kernel,
        grid=(batch, kvh),
        in_specs=[
            pl.BlockSpec((seq, hd), lambda b, h: (b, OFF_KC // hd + h)),
            pl.BlockSpec((seq, hd), lambda b, h: (b, OFF_VC // hd + h)),
            full((CMP_BLOCK, hd)), full((CMP_BLOCK * hd, hd)), full((hd, hd)),
            full((CMP_BLOCK, hd)), full((CMP_BLOCK * hd, hd)), full((hd, hd)),
            full((1, hd)),
        ],
        out_specs=[pl.BlockSpec((1, 1, ng, hd), lambda b, h: (b, h, 0, 0)),
                   pl.BlockSpec((1, 1, hd, ng), lambda b, h: (b, h, 0, 0))],
        out_shape=[jax.ShapeDtypeStruct((batch, kvh, ng, hd), BF16),
                   jax.ShapeDtypeStruct((batch, kvh, hd, ng), BF16)],
        compiler_params=_cparams(("parallel", "parallel")),
        name="nsa_cmp",
    )(p, p, pek, w1k, w2k, pev, w1v, w2v, kn)


def _nsa_attn_kernel(qc_ref, qr_ref, kc_ref, vct_ref, ks_ref, vst_ref, kw_ref, vwt_ref, gl_ref, gb_ref, ovl_ref,
                     o_ref, selb_ref, gt_ref, acc_ref, z_ref, p_ref, zc_ref, pcb_ref, zw_ref, pw_ref, ob_ref, *, tq, tk, n_sel):
    seq = ks_ref.shape[0]
    ncp = kc_ref.shape[2]
    ns = seq // SEL_BLOCK
    grp = NSA_GROUP
    hd = HEAD_DIM
    gq = grp * tq
    nblk = tk // SEL_BLOCK
    kv = pl.program_id(1)
    s0 = pl.program_id(2) * tq
    wspan = WINDOW + tq

    qc = jnp.concatenate([qc_ref[:, g * hd:(g + 1) * hd] for g in range(grp)], axis=0)
    qr = jnp.concatenate([qr_ref[:, g * hd:(g + 1) * hd] for g in range(grp)], axis=0)
    t_row = s0 + lax.broadcasted_iota(jnp.int32, (1, tq), 1)
    t_all = s0 + lax.broadcasted_iota(jnp.int32, (1, gq), 1) % tq

    def heads(x):
        return jnp.concatenate([x] * grp, axis=1)

    srow = lax.broadcasted_iota(jnp.int32, (SEL_BLOCK, 1), 0)
    chunk = SEL_BLOCK

    def biased_max(z, nrows, bias_fn):
        mx = jnp.full((8, gq), -MASK_VALUE, F32)
        for c in range(nrows // chunk):
            rs = slice(c * chunk, (c + 1) * chunk)
            zb = z[rs, :] + heads(bias_fn(c))
            z[rs, :] = zb
            mx = jnp.maximum(mx, jnp.max(zb.reshape(chunk // 8, 8, gq), axis=0))
        return jnp.max(mx, axis=0, keepdims=True)

    start = pl.multiple_of(jnp.maximum(s0 - WINDOW, 0), tq)
    zc_ref[...] = _dot_nt(kc_ref[0, 0], qc)
    zw_ref[...] = _dot_nt(kw_ref[pl.ds(start, wspan), :], qr)

    def cmp_bias(c):
        cend = (c * chunk + srow) * CMP_STRIDE + (CMP_BLOCK - 1)
        return jnp.where(cend <= t_row, 0.0, -MASK_VALUE)

    def win_bias(c):
        kpos = start + c * chunk + srow
        return jnp.where((kpos <= t_row) & (kpos > t_row - WINDOW), 0.0, -MASK_VALUE)

    mc = biased_max(zc_ref, ncp, cmp_bias)
    mw = biased_max(zw_ref, wspan, win_bias)
    den8 = jnp.zeros((8, gq), F32)
    for c in range(ncp // chunk):
        rs = slice(c * chunk, (c + 1) * chunk)
        e = jnp.exp2(zc_ref[rs, :] - mc)
        zc_ref[rs, :] = e
        pcb_ref[rs, :] = e.astype(BF16)
        den8 = den8 + jnp.sum(e.reshape(chunk // 8, 8, gq), axis=0)
    for c in range(wspan // chunk):
        rs = slice(c * chunk, (c + 1) * chunk)
        pw_ref[rs, :] = jnp.exp2(zw_ref[rs, :] - mw).astype(BF16)
    inv_c = jnp.where(t_all >= CMP_BLOCK - 1, 1.0 / jnp.sum(den8, axis=0, keepdims=True), 0.0)
    ob_ref[0] = _dot(vct_ref[0, 0], pcb_ref[...]) * inv_c
    ow = _dot(vwt_ref[0, 0, :, pl.ds(start, wspan)], pw_ref[...])
    ob_ref[1] = ow[0:hd, :] * (1.0 / ow[hd:hd + 1, :])

    pcs = []
    for c in range(ncp // chunk):
        pn = zc_ref[c * chunk:(c + 1) * chunk, :] * inv_c
        pcs.append(sum(pn[:, g * tq:(g + 1) * tq] for g in range(1, grp)) + pn[:, 0:tq])
    pcs = jnp.concatenate(pcs, axis=0)
    pcs_hi = pcs.astype(BF16)
    pcs_lo = (pcs - pcs_hi.astype(F32)).astype(BF16)
    imp = _dot(ovl_ref[...], pcs_hi) + _dot(ovl_ref[...], pcs_lo)
    jrow = lax.broadcasted_iota(jnp.int32, (ns, tq), 0)
    cur = (s0 + lax.broadcasted_iota(jnp.int32, (ns, tq), 1)) // SEL_BLOCK
    forced = (jrow == 0) | (jrow == cur) | (jrow == cur - 1)
    imp = jnp.where(forced, MASK_VALUE, jnp.where(jrow <= cur, imp, -MASK_VALUE))
    ngrp = ns // 8
    imp_g = [imp[8 * r:8 * r + 8, :] for r in range(ngrp)]
    sub = lax.broadcasted_iota(jnp.int32, (8, tq), 0)
    rank_g = [jnp.zeros((8, tq), F32) for _ in range(ngrp)]
    for i in range(ns):
        other = jnp.broadcast_to(imp[i:i + 1, :], (8, tq))
        for r in range(ngrp):
            if 8 * r > i:
                ahead = jnp.where(other >= imp_g[r], 1.0, 0.0)
            elif 8 * r + 7 < i:
                ahead = jnp.where(other > imp_g[r], 1.0, 0.0)
            else:
                ahead = jnp.where(sub > i - 8 * r, jnp.where(other >= imp_g[r], 1.0, 0.0),
                                  jnp.where(other > imp_g[r], 1.0, 0.0))
            rank_g[r] = rank_g[r] + ahead
    for r in range(ngrp):
        selb_ref[8 * r:8 * r + 8, :] = jnp.where(rank_g[r] < n_sel, 0.0, -MASK_VALUE)

    acc_ref[...] = jnp.zeros_like(acc_ref)

    n_kt = (s0 + tq + tk - 1) // tk

    def scores(kt):
        return _dot_nt(ks_ref[pl.ds(pl.multiple_of(kt * tk, tk), tk), :], qr)

    n_pairs = (n_kt + 1) // 2
    z_ref[0] = scores(0)

    def tile_step(kt, m, slot):
        z_ref[1 - slot] = scores(jnp.minimum(kt + 1, 2 * n_pairs - 1))
        koff = pl.multiple_of(kt * tk, tk)
        zs = z_ref.at[slot]

        def sel_bias(j):
            return (jnp.broadcast_to(selb_ref[pl.ds(kt * nblk + j, 1), :], (chunk, tq))
                    + jnp.where(koff + j * chunk + srow <= t_row, 0.0, -MASK_VALUE))

        m_new = jnp.maximum(m, biased_max(zs, tk, sel_bias))
        for j in range(nblk):
            rs = slice(j * chunk, (j + 1) * chunk)
            p_ref[rs, :] = jnp.exp2(zs[rs, :] - m_new).astype(BF16)
        acc_ref[...] = acc_ref[...] * jnp.exp2(m - m_new) + _dot(vst_ref[0, 0, :, pl.ds(koff, tk)], p_ref[...])
        return m_new

    def pair_body(pi, m):
        return tile_step(2 * pi + 1, tile_step(2 * pi, m, 0), 1)

    lax.fori_loop(0, n_pairs, pair_body, jnp.full((1, gq), -MASK_VALUE, F32))
    o_sel = acc_ref[0:hd, :] * (1.0 / acc_ref[hd:hd + 1, :])

    gt_ref[...] = _sigmoid(gl_ref[...] + gb_ref[...]).T
    for g in range(grp):
        base = MISC_GATE + (kv * grp + g) * 3
        cs = slice(g * tq, (g + 1) * tq)
        og = (gt_ref[pl.ds(base, 1), :] * ob_ref[0, :, cs] + gt_ref[pl.ds(base + 1, 1), :] * o_sel[:, cs]
              + gt_ref[pl.ds(base + 2, 1), :] * ob_ref[1, :, cs])
        o_ref[:, g * hd:(g + 1) * hd] = og.T


def _nsa_attn(p, qc, qr, kc, vct, ks, vst, kw, vwt, gate_b, ovl, batch, seq, tq, tk):
    hd = HEAD_DIM
    gw = NSA_GROUP * hd
    nq = seq // tq
    ns = seq // SEL_BLOCK
    ncp = kc.shape[2]
    n_sel = min(N_SELECT, ns)
    assert seq % (2 * tk) == 0 and tk % SEL_BLOCK == 0 and seq >= WINDOW + tq and WINDOW % tq == 0

    def qspec():
        return pl.BlockSpec((tq, gw), lambda b, h, i: (b * nq + i, h))

    def kspec():
        return pl.BlockSpec((seq, hd), lambda b, h, i: (b, h))

    def vtspec():
        return pl.BlockSpec((1, 1, VT_ROWS, seq), lambda b, h, i: (b, h, 0, 0))

    return pl.pallas_call(
        functools.partial(_nsa_attn_kernel, tq=tq, tk=tk, n_sel=n_sel),
        grid=(batch, NSA_KV_HEADS, nq),
        in_specs=[qspec(), qspec(),
                  pl.BlockSpec((1, 1, ncp, hd), lambda b, h, i: (b, h, 0, 0)),
                  pl.BlockSpec((1, 1, hd, ncp), lambda b, h, i: (b, h, 0, 0)),
                  kspec(), vtspec(), kspec(), vtspec(),
                  pl.BlockSpec((tq, LANES), lambda b, h, i: (b * nq + i, OFF_MISC // LANES)),
                  pl.BlockSpec((1, LANES), lambda b, h, i: (0, 0)),
                  pl.BlockSpec((ns, ncp), lambda b, h, i: (0, 0))],
        out_specs=pl.BlockSpec((tq, gw), lambda b, h, i: (b * nq + i, h)),
        out_shape=jax.ShapeDtypeStruct((batch * seq, NSA_WIDTH), F32),
        scratch_shapes=[pltpu.VMEM((ns, tq), F32), pltpu.VMEM((LANES, tq), F32),
                        pltpu.VMEM((VT_ROWS, NSA_GROUP * tq), F32),
                        pltpu.VMEM((2, tk, NSA_GROUP * tq), F32), pltpu.VMEM((tk, NSA_GROUP * tq), BF16),
                        pltpu.VMEM((ncp, NSA_GROUP * tq), F32), pltpu.VMEM((ncp, NSA_GROUP * tq), BF16),
                        pltpu.VMEM((WINDOW + tq, NSA_GROUP * tq), F32),
                        pltpu.VMEM((WINDOW + tq, NSA_GROUP * tq), BF16),
                        pltpu.VMEM((2, hd, NSA_GROUP * tq), F32)],
        compiler_params=_cparams(("parallel", "parallel", "arbitrary")),
        name="nsa_attn",
    )(qc, qr, kc, vct, ks, vst, kw, vwt, p, gate_b, ovl)


def _gla_kernel(q_ref, k_ref, v_ref, g_ref, misc_ref, a2_ref, ab_ref, gn_ref, ex_ref, o_ref,
                st_ref, qe_ref, kd_ref, vb_ref, ds_ref, oi_ref, od_ref):
    nb, tt, wq = q_ref.shape
    wv = v_ref.shape[2]
    step = GLA_STEP
    ng = tt // step
    nh = GLA_HEADS
    hl = LANES
    per = hl // GLA_DK

    @pl.when(pl.program_id(1) == 0)
    def _():
        st_ref[...] = jnp.zeros_like(st_ref)

    pos = lax.broadcasted_iota(jnp.int32, (tt, wq), 0) % step
    pos3 = pos.reshape(ng, step, wq)
    lane = lax.broadcasted_iota(jnp.int32, (tt, hl), 1)

    for s_i in range(nb):
        x = _dot(misc_ref[s_i].astype(BF16), a2_ref[...]) + ab_ref[...]
        b = (-_softplus(-x)) * np.float32(np.log2(np.e) / GLA_TAU)
        s = 1
        while s < step:
            b = b + jnp.where(pos >= s, pltpu.roll(b, s, 0), 0.0)
            s *= 2

        q3 = (q_ref[s_i] * np.float32(GLA_DK ** -0.5)).reshape(ng, step, wq)
        k3 = k_ref[s_i].reshape(ng, step, wq)
        v3 = v_ref[s_i].reshape(ng, step, wv)
        b3 = b.reshape(ng, step, wq)
        b_last = b3[:, step - 1:step, :]
        qe_ref[s_i] = (q3 * jnp.exp2(b3)).reshape(tt, wq).astype(BF16)
        kd = (k3 * jnp.exp2(b_last - b3)).reshape(tt, wq)
        for h in range(nh):
            grp_sl = slice((h // per) * hl, (h // per + 1) * hl)
            mine = (lane >= (h % per) * GLA_DK) & (lane < (h % per + 1) * GLA_DK)
            kd_ref[s_i * nh + h] = jnp.where(mine, kd[:, grp_sl], 0.0).astype(BF16)
        vb_ref[s_i] = v_ref[s_i].astype(BF16)
        ds_ref[s_i] = jnp.exp2(b_last)

        odiag = jnp.zeros((tt, wv), F32)
        for j in range(step):
            dec = jnp.exp2(jnp.where(pos3 >= j, b3 - b3[:, j:j + 1, :], -MASK_VALUE))
            term = (q3 * k3[:, j:j + 1, :] * dec).reshape(tt, wq).astype(BF16)
            vj = jnp.broadcast_to(v3[:, j:j + 1, :], (ng, step, wv)).reshape(tt, wv)
            odiag = odiag + _dot(term, ex_ref[...]) * vj
        od_ref[s_i] = odiag

    def body(gi, carry):
        r0 = pl.multiple_of(gi * step, step)
        for s_i in range(nb):
            dsg = ds_ref[s_i, gi]
            for h in range(nh):
                sl = slice(h * hl, (h + 1) * hl)
                grp_sl = slice((h // per) * hl, (h // per + 1) * hl)
                st = st_ref[s_i * nh + h]
                oi_ref[s_i, pl.ds(r0, step), sl] = _dot_nt(qe_ref[s_i, pl.ds(r0, step), grp_sl], st.astype(BF16))
                upd = _dot_tn(vb_ref[s_i, pl.ds(r0, step), sl], kd_ref[s_i * nh + h, pl.ds(r0, step), :])
                st_ref[s_i * nh + h] = st * dsg[:, grp_sl] + upd
        return carry

    lax.fori_loop(0, ng, body, 0)

    for s_i in range(nb):
        o = oi_ref[s_i] + od_ref[s_i]
        gate = g_ref[s_i]
        gate = gate * _sigmoid(gate)
        for h in range(nh):
            sl = slice(h * hl, (h + 1) * hl)
            o_ref[s_i, :, sl] = (_rms(o[:, sl], gn_ref[...]) * gate[:, sl]).astype(o_ref.dtype)


def _gla(p, batch, seq, a2, ab, gn, tt, nb):
    wq = GLA_QK_WIDTH
    wv = GLA_WIDTH
    nt = seq // tt
    ng = tt // GLA_STEP
    expand = jnp.asarray(np.kron(np.eye(GLA_HEADS), np.ones((GLA_DK, GLA_DV))), dtype=BF16)
    p3 = p.reshape(batch, seq, p.shape[1])

    def col(off, width):
        return pl.BlockSpec((nb, tt, width), lambda b, t: (b, t, off // width))

    out = pl.pallas_call(
        _gla_kernel,
        grid=(batch // nb, nt),
        in_specs=[col(OFF_GQ, wq), col(OFF_GK, wq), col(OFF_GV, wv), col(OFF_GG, wv), col(OFF_MISC, LANES),
                  pl.BlockSpec((LANES, wq), lambda b, t: (0, 0)),
                  pl.BlockSpec((1, wq), lambda b, t: (0, 0)),
                  pl.BlockSpec((1, GLA_DV), lambda b, t: (0, 0)),
                  pl.BlockSpec((wq, wv), lambda b, t: (0, 0))],
        out_specs=pl.BlockSpec((nb, tt, wv), lambda b, t: (b, t, 0)),
        out_shape=jax.ShapeDtypeStruct((batch, seq, wv), BF16),
        scratch_shapes=[
            pltpu.VMEM((nb * GLA_HEADS, GLA_DV, LANES), F32),
            pltpu.VMEM((nb, tt, wq), BF16),
            pltpu.VMEM((nb * GLA_HEADS, tt, LANES), BF16),
            pltpu.VMEM((nb, tt, wv), BF16),
            pltpu.VMEM((nb, ng, 1, wq), F32),
            pltpu.VMEM((nb, tt, wv), F32),
            pltpu.VMEM((nb, tt, wv), F32),
        ],
        compiler_params=_cparams(("parallel", "arbitrary")),
        name="gla",
    )(p3, p3, p3, p3, p3, a2, ab, gn, expand)
    return out.reshape(batch * seq, wv)


def _pack_w_in(w_in):
    d = w_in.shape[0]
    o_gate = OFF_VW + NSA_KV_WIDTH
    o_gq = o_gate + NSA_HEADS * 3
    o_ga = o_gq + 2 * GLA_QK_WIDTH + 2 * GLA_WIDTH
    tail = jnp.zeros((d, N_PACKED - OFF_MISC - NSA_HEADS * 3 - GLA_RANK), w_in.dtype)
    packed = jnp.concatenate([w_in[:, :o_gate], w_in[:, o_gq:o_ga],
                              w_in[:, o_gate:o_gq], w_in[:, o_ga:o_ga + GLA_RANK], tail], axis=1)
    assert packed.shape[1] == N_PACKED
    return packed.astype(BF16)


def _block_diag(w):
    nb, bi, bo = w.shape
    eye = jnp.eye(nb, dtype=w.dtype)
    return (eye[:, None, :, None] * w[:, :, None, :]).reshape(nb * bi, nb * bo)


def _rope_lane_tables(seq):
    inv = 1.0 / (ROPE_THETA ** (jnp.arange(0, ROPE_DIM, 2, dtype=F32) / ROPE_DIM))
    ang = jnp.arange(seq, dtype=F32)[:, None] * inv[None, :]
    cos, sin = jnp.cos(ang), jnp.sin(ang)
    rest = LANES - ROPE_DIM
    zeros_h = jnp.zeros((seq, ROPE_HALF), F32)
    cosf = jnp.concatenate([cos, cos, jnp.ones((seq, rest), F32)], axis=1)
    sina = jnp.concatenate([-sin, zeros_h, jnp.zeros((seq, rest), F32)], axis=1)
    sinb = jnp.concatenate([zeros_h, sin, jnp.zeros((seq, rest), F32)], axis=1)
    return cosf, sina, sinb


def _overlap_t(seq, ncp):
    ns = seq // SEL_BLOCK
    nc = (seq - CMP_BLOCK) // CMP_STRIDE + 1
    cs = np.arange(ncp) * CMP_STRIDE
    ss = np.arange(ns) * SEL_BLOCK
    ov = (cs[None, :] < ss[:, None] + SEL_BLOCK) & (cs[None, :] + CMP_BLOCK > ss[:, None]) & (np.arange(ncp) < nc)[None, :]
    return jnp.asarray(ov.astype(np.float32), dtype=BF16)


def _row(v):
    return v.reshape(1, -1)


def _mixers(p, batch, seq, l, prm, tables, ovl_t):
    cosf, sina, sinb = tables
    wg = jnp.concatenate([_block_diag(prm['lru_gate_a_w'][l]), _block_diag(prm['lru_gate_x_w'][l])], axis=1).astype(BF16)
    bg = jnp.concatenate([prm['lru_gate_a_b'][l], prm['lru_gate_x_b'][l]]).reshape(1, -1)
    ya = _lru(p, batch, seq, prm['lru_conv_w'][l], _row(prm['lru_conv_b'][l]), wg, bg,
              _row(prm['lru_lambda'][l]), _row(prm['lru_out_norm'][l]), tt=min(512, seq))
    qc, qr, ks, vs, kw, vw = _nsa_prep(p, seq, cosf, sina, sinb, _row(prm['nsa_q_norm'][l]),
                                       _row(prm['nsa_k_sel_norm'][l]), _row(prm['nsa_k_win_norm'][l]),
                                       tt=min(512, seq))
    kc, vc = _nsa_cmp(p, batch, seq, prm['nsa_cmp_pe_k'][l], prm['nsa_cmp_w1_k'][l].astype(BF16),
                      prm['nsa_cmp_w2_k'][l].astype(BF16), prm['nsa_cmp_pe_v'][l],
                      prm['nsa_cmp_w1_v'][l].astype(BF16), prm['nsa_cmp_w2_v'][l].astype(BF16),
                      _row(prm['nsa_k_cmp_norm'][l]))
    gate_b = jnp.pad(prm['nsa_gate_b'][l], (MISC_GATE, LANES - MISC_GATE - NSA_HEADS * 3)).reshape(1, LANES)
    yb = _nsa_attn(p, qc, qr, kc, vc, ks, vs, kw, vw, gate_b, ovl_t, batch, seq, tq=128, tk=min(512, seq))
    a2 = jnp.zeros((LANES, GLA_QK_WIDTH), F32).at[MISC_GA:MISC_GA + GLA_RANK].set(prm['gla_a_w2'][l]).astype(BF16)
    ab = _row(prm['gla_a_b'][l])
    yc = _gla(p, batch, seq, a2, ab, _row(prm['gla_out_norm'][l]), tt=min(256, seq),
              nb=4 if batch % 4 == 0 else 1)
    return ya, yb, yc


def kernel(x, ffn1_norm, ffn1_w_gate, ffn1_w_up, ffn1_w_down, mix_norm, w_in, lru_conv_w, lru_conv_b, lru_gate_a_w, lru_gate_a_b, lru_gate_x_w, lru_gate_x_b, lru_lambda, lru_out_norm, nsa_q_norm, nsa_k_cmp_norm, nsa_k_sel_norm, nsa_k_win_norm, nsa_cmp_pe_k, nsa_cmp_w1_k, nsa_cmp_w2_k, nsa_cmp_pe_v, nsa_cmp_w1_v, nsa_cmp_w2_v, nsa_gate_b, nsa_out_norm, gla_a_w2, gla_a_b, gla_out_norm, w_out, ffn2_norm, ffn2_w_gate, ffn2_w_up, ffn2_w_down):
    prm = dict(lru_conv_w=lru_conv_w, lru_conv_b=lru_conv_b, lru_gate_a_w=lru_gate_a_w, lru_gate_a_b=lru_gate_a_b,
               lru_gate_x_w=lru_gate_x_w, lru_gate_x_b=lru_gate_x_b, lru_lambda=lru_lambda, lru_out_norm=lru_out_norm,
               nsa_q_norm=nsa_q_norm, nsa_k_cmp_norm=nsa_k_cmp_norm, nsa_k_sel_norm=nsa_k_sel_norm,
               nsa_k_win_norm=nsa_k_win_norm, nsa_cmp_pe_k=nsa_cmp_pe_k, nsa_cmp_w1_k=nsa_cmp_w1_k,
               nsa_cmp_w2_k=nsa_cmp_w2_k, nsa_cmp_pe_v=nsa_cmp_pe_v, nsa_cmp_w1_v=nsa_cmp_w1_v,
               nsa_cmp_w2_v=nsa_cmp_w2_v, nsa_gate_b=nsa_gate_b, gla_a_w2=gla_a_w2, gla_a_b=gla_a_b,
               gla_out_norm=gla_out_norm)
    batch, seq, d = x.shape
    depth = w_in.shape[0]
    m = batch * seq
    tables = _rope_lane_tables(seq)
    ncp = seq // CMP_STRIDE
    ovl_t = _overlap_t(seq, ncp)
    tm = min(512, m)
    tm_ffn = min(1024, m)
    tf = 256
    xf = x.reshape(m, d)
    ffn1 = [w.astype(BF16) for w in (ffn1_w_gate, ffn1_w_up, ffn1_w_down)]
    ffn2 = [w.astype(BF16) for w in (ffn2_w_gate, ffn2_w_up, ffn2_w_down)]
    for l in range(depth):
        xf = _ffn(xf, _row(ffn1_norm[l]), *ffn1, l, tm_ffn, tf)
        p = _mix_in(xf, _row(mix_norm[l]), _pack_w_in(w_in[l]), tm_ffn, N_PACKED // MIX_IN_TILES)
        ya, yb, yc = _mixers(p, batch, seq, l, prm, tables, ovl_t)
        xf = _mix_out(xf, ya, yb, yc, _row(nsa_out_norm[l]), w_out[l].astype(BF16), tm)
        xf = _ffn(xf, _row(ffn2_norm[l]), *ffn2, l, tm_ffn, tf)
    return xf.reshape(batch, seq, d)
```

```python
import functools

import numpy as np
import jax
import jax.numpy as jnp
from jax import lax
from jax.experimental import pallas as pl
from jax.experimental.pallas import tpu as pltpu

F32 = jnp.float32
BF16 = jnp.bfloat16

NORM_EPS = 1e-6
D_MODEL = 2048
D_FF = 5632

LRU_WIDTH = 512
LRU_BLOCKS = 8
LRU_BLOCK_SIZE = LRU_WIDTH // LRU_BLOCKS
CONV_WIDTH = 4
LRU_C = 8.0

HEAD_DIM = 128
NSA_HEADS = 8
NSA_KV_HEADS = 2
NSA_GROUP = NSA_HEADS // NSA_KV_HEADS
NSA_WIDTH = NSA_HEADS * HEAD_DIM
NSA_KV_WIDTH = NSA_KV_HEADS * HEAD_DIM
ROPE_DIM = HEAD_DIM // 4
ROPE_HALF = ROPE_DIM // 2
ROPE_THETA = 500000.0
CMP_BLOCK = 32
CMP_STRIDE = 16
SEL_BLOCK = 64
N_SELECT = 16
WINDOW = 512
MASK_VALUE = 1e30
QK_LOG2_SCALE = float(HEAD_DIM ** -0.5 * np.log2(np.e))
VT_ROWS = HEAD_DIM + 16

GLA_DV = 128
GLA_HEADS = 4
GLA_DK = 64
GLA_WIDTH = GLA_HEADS * GLA_DV
GLA_QK_WIDTH = GLA_HEADS * GLA_DK
GLA_RANK = 16
GLA_TAU = 16.0
GLA_STEP = 16

LANES = 128

OFF_LRU_X = 0
OFF_LRU_Y = OFF_LRU_X + LRU_WIDTH
OFF_Q = OFF_LRU_Y + LRU_WIDTH
OFF_KC = OFF_Q + NSA_WIDTH
OFF_VC = OFF_KC + NSA_KV_WIDTH
OFF_KS = OFF_VC + NSA_KV_WIDTH
OFF_VS = OFF_KS + NSA_KV_WIDTH
OFF_KW = OFF_VS + NSA_KV_WIDTH
OFF_VW = OFF_KW + NSA_KV_WIDTH
OFF_GQ = OFF_VW + NSA_KV_WIDTH
OFF_GK = OFF_GQ + GLA_QK_WIDTH
OFF_GV = OFF_GK + GLA_QK_WIDTH
OFF_GG = OFF_GV + GLA_WIDTH
OFF_MISC = OFF_GG + GLA_WIDTH
MIX_IN_TILES = 3
N_PACKED = -(-(OFF_MISC + LANES) // (2 * LANES * MIX_IN_TILES)) * (2 * LANES * MIX_IN_TILES)
MISC_GATE = 0
MISC_GA = NSA_HEADS * 3

VMEM_LIMIT = 56 * 1024 * 1024
FFN_SUB = 256


def _cparams(sem):
    return pltpu.CompilerParams(dimension_semantics=sem, vmem_limit_bytes=VMEM_LIMIT)


def _sigmoid(x):
    return 1.0 / (1.0 + jnp.exp(-x))


def _softplus(x):
    return jnp.maximum(x, 0.0) + jnp.log1p(jnp.exp(-jnp.abs(x)))


def _gelu_tanh(x):
    c = np.float32(np.sqrt(2.0 / np.pi))
    return x * (0.5 * (1.0 + jnp.tanh(c * (x + 0.044715 * (x * x * x)))))


def _rms(x, g):
    return x * lax.rsqrt(jnp.mean(x * x, axis=-1, keepdims=True) + NORM_EPS) * g


def _dot(a, b):
    return jnp.dot(a, b, preferred_element_type=F32)


def _dot_nt(a, b):
    return lax.dot_general(a, b, (((1,), (1,)), ((), ())), preferred_element_type=F32)


def _dot_tn(a, b):
    return lax.dot_general(a, b, (((0,), (0,)), ((), ())), preferred_element_type=F32)


def _ffn_kernel(x_ref, g_ref, wg_ref, wu_ref, wd_ref, o_ref, h_ref):
    j = pl.program_id(1)

    @pl.when(j == 0)
    def _():
        h_ref[...] = _rms(x_ref[...], g_ref[...]).astype(BF16)
        o_ref[...] = jnp.zeros_like(o_ref)

    h = h_ref[...]
    tf = wg_ref.shape[1]
    acc = None
    for c in range(tf // FFN_SUB):
        cs = slice(c * FFN_SUB, (c + 1) * FFN_SUB)
        gate = _dot(h, wg_ref[:, cs])
        up = _dot(h, wu_ref[:, cs])
        act = (gate * _sigmoid(gate)) * up
        part = _dot(act.astype(BF16), wd_ref[cs, :])
        acc = part if acc is None else acc + part
    o_ref[...] += acc

    @pl.when(j == pl.num_programs(1) - 1)
    def _():
        o_ref[...] = x_ref[...] + 0.5 * o_ref[...]


def _ffn(x, g, wg, wu, wd, layer, tm, tf):
    m, d = x.shape
    f = wg.shape[2]
    return pl.pallas_call(
        _ffn_kernel,
        grid=(m // tm, f // tf),
        in_specs=[
            pl.BlockSpec((tm, d), lambda i, j: (i, 0)),
            pl.BlockSpec((1, d), lambda i, j: (0, 0)),
            pl.BlockSpec((None, d, tf), lambda i, j: (layer, 0, j)),
            pl.BlockSpec((None, d, tf), lambda i, j: (layer, 0, j)),
            pl.BlockSpec((None, tf, d), lambda i, j: (layer, j, 0)),
        ],
        out_specs=pl.BlockSpec((tm, d), lambda i, j: (i, 0)),
        out_shape=jax.ShapeDtypeStruct((m, d), F32),
        scratch_shapes=[pltpu.VMEM((tm, d), BF16)],
        compiler_params=_cparams(("parallel", "arbitrary")),
        name="ffn",
    )(x, g, wg, wu, wd)


def _mix_in_kernel(x_ref, g_ref, w_ref, o_ref, h_ref):
    @pl.when(pl.program_id(1) == 0)
    def _():
        h_ref[...] = _rms(x_ref[...], g_ref[...]).astype(BF16)

    o_ref[...] = _dot(h_ref[...], w_ref[...])


def _mix_in(x, g, w, tm, tn):
    m, d = x.shape
    n = w.shape[1]
    return pl.pallas_call(
        _mix_in_kernel,
        grid=(m // tm, n // tn),
        in_specs=[
            pl.BlockSpec((tm, d), lambda i, j: (i, 0)),
            pl.BlockSpec((1, d), lambda i, j: (0, 0)),
            pl.BlockSpec((d, tn), lambda i, j: (0, j)),
        ],
        out_specs=pl.BlockSpec((tm, tn), lambda i, j: (i, j)),
        out_shape=jax.ShapeDtypeStruct((m, n), F32),
        scratch_shapes=[pltpu.VMEM((tm, d), BF16)],
        compiler_params=_cparams(("parallel", "arbitrary")),
        name="mix_in",
    )(x, g, w)


def _mix_out_kernel(x_ref, ya_ref, yb_ref, yc_ref, gb_ref, wa_ref, wb_ref, wc_ref, o_ref):
    nb = _rms(yb_ref[...], gb_ref[...]).astype(BF16)
    acc = _dot(ya_ref[...], wa_ref[...])
    acc += _dot(nb, wb_ref[...])
    acc += _dot(yc_ref[...], wc_ref[...])
    o_ref[...] = x_ref[...] + acc


def _mix_out(x, ya, yb, yc, gb, w_out, tm):
    m, d = x.shape
    wa, wb, wc = LRU_WIDTH, NSA_WIDTH, GLA_WIDTH
    return pl.pallas_call(
        _mix_out_kernel,
        grid=(m // tm,),
        in_specs=[
            pl.BlockSpec((tm, d), lambda i: (i, 0)),
            pl.BlockSpec((tm, wa), lambda i: (i, 0)),
            pl.BlockSpec((tm, wb), lambda i: (i, 0)),
            pl.BlockSpec((tm, wc), lambda i: (i, 0)),
            pl.BlockSpec((1, wb), lambda i: (0, 0)),
            pl.BlockSpec((wa, d), lambda i: (0, 0)),
            pl.BlockSpec((wb, d), lambda i: (0, 0)),
            pl.BlockSpec((wc, d), lambda i: (0, 0)),
        ],
        out_specs=pl.BlockSpec((tm, d), lambda i: (i, 0)),
        out_shape=jax.ShapeDtypeStruct((m, d), F32),
        compiler_params=_cparams(("parallel",)),
        name="mix_out",
    )(x, ya, yb, yc, gb, w_out[:wa], w_out[wa:wa + wb], w_out[wa + wb:])


def _lru_kernel(u_ref, y_ref, cw_ref, cb_ref, wg_ref, bg_ref, lam_ref, gn_ref, o_ref,
                ext_ref, a_ref, b_ref, hc_ref):
    tt, w = u_ref.shape

    @pl.when(pl.program_id(1) == 0)
    def _():
        ext_ref[0:8, :] = jnp.zeros((8, w), F32)
        hc_ref[...] = jnp.zeros_like(hc_ref)

    u = u_ref[...]
    ext_ref[8:8 + tt, :] = u
    xc = cb_ref[...] + cw_ref[CONV_WIDTH - 1:CONV_WIDTH, :] * u
    for k in range(CONV_WIDTH - 1):
        sh = CONV_WIDTH - 1 - k
        xc = xc + cw_ref[k:k + 1, :] * ext_ref[8 - sh:8 - sh + tt, :]
    ext_ref[0:8, :] = u[tt - 8:tt, :]

    gates = _dot(xc.astype(BF16), wg_ref[...]) + bg_ref[...]
    r = _sigmoid(gates[:, :w])
    i = _sigmoid(gates[:, w:])
    log_a = (-LRU_C) * r * _softplus(-lam_ref[...])
    a = jnp.exp(log_a)
    a_ref[...] = a
    b_ref[...] = jnp.sqrt(-jnp.tanh(log_a) * (a * a + 1.0)) * i * xc

    row = lax.broadcasted_iota(jnp.int32, (8, w), 0)

    def body(gi, hprev):
        off = pl.multiple_of(gi * 8, 8)
        a8 = a_ref[pl.ds(off, 8), :]
        b8 = b_ref[pl.ds(off, 8), :]
        for s in (1, 2, 4):
            keep = row >= s
            b8 = b8 + a8 * jnp.where(keep, pltpu.roll(b8, s, 0), 0.0)
            a8 = a8 * jnp.where(keep, pltpu.roll(a8, s, 0), 1.0)
        h8 = b8 + a8 * hprev
        b_ref[pl.ds(off, 8), :] = h8
        return jnp.broadcast_to(h8[7:8, :], (8, w))

    hc_ref[...] = lax.fori_loop(0, tt // 8, body, hc_ref[...])

    ya = b_ref[...] * _gelu_tanh(y_ref[...])
    o_ref[...] = _rms(ya, gn_ref[...]).astype(o_ref.dtype)


def _lru(p, batch, seq, cw, cb, wg, bg, lam, gn, tt):
    w = LRU_WIDTH
    nt = seq // tt
    return pl.pallas_call(
        _lru_kernel,
        grid=(batch, nt),
        in_specs=[
            pl.BlockSpec((tt, w), lambda b, t: (b * nt + t, OFF_LRU_X // w)),
            pl.BlockSpec((tt, w), lambda b, t: (b * nt + t, OFF_LRU_Y // w)),
            pl.BlockSpec((CONV_WIDTH, w), lambda b, t: (0, 0)),
            pl.BlockSpec((1, w), lambda b, t: (0, 0)),
            pl.BlockSpec((w, 2 * w), lambda b, t: (0, 0)),
            pl.BlockSpec((1, 2 * w), lambda b, t: (0, 0)),
            pl.BlockSpec((1, w), lambda b, t: (0, 0)),
            pl.BlockSpec((1, w), lambda b, t: (0, 0)),
        ],
        out_specs=pl.BlockSpec((tt, w), lambda b, t: (b * nt + t, 0)),
        out_shape=jax.ShapeDtypeStruct((batch * seq, w), BF16),
        scratch_shapes=[
            pltpu.VMEM((tt + 8, w), F32),
            pltpu.VMEM((tt, w), F32),
            pltpu.VMEM((tt, w), F32),
            pltpu.VMEM((8, w), F32),
        ],
        compiler_params=_cparams(("parallel", "arbitrary")),
        name="lru",
    )(p, p, cw, cb, wg, bg, lam, gn)


def _rope(x, cosf, sina, sinb):
    return x * cosf + pltpu.roll(x, LANES - ROPE_HALF, 1) * sina + pltpu.roll(x, ROPE_HALF, 1) * sinb


def _nsa_prep_kernel(q_ref, ks_ref, vs_ref, kw_ref, vw_ref, cos_ref, sina_ref, sinb_ref,
                     qn_ref, ksn_ref, kwn_ref,
                     qc_ref, qr_ref, kso_ref, vso_ref, kwo_ref, vwo_ref):
    cosf, sina, sinb = cos_ref[...], sina_ref[...], sinb_ref[...]
    for h in range(NSA_HEADS):
        sl = slice(h * HEAD_DIM, (h + 1) * HEAD_DIM)
        qh = _rms(q_ref[:, sl], qn_ref[...])
        qc_ref[:, sl] = (qh * QK_LOG2_SCALE).astype(BF16)
        qr_ref[:, sl] = (_rope(qh, cosf, sina, sinb) * QK_LOG2_SCALE).astype(BF16)
    for h in range(NSA_KV_HEADS):
        sl = slice(h * HEAD_DIM, (h + 1) * HEAD_DIM)
        kso_ref[:, sl] = _rope(_rms(ks_ref[:, sl], ksn_ref[...]), cosf, sina, sinb).astype(BF16)
        kwo_ref[:, sl] = _rope(_rms(kw_ref[:, sl], kwn_ref[...]), cosf, sina, sinb).astype(BF16)
        ones = jnp.ones((VT_ROWS - HEAD_DIM, vs_ref.shape[0]), BF16)
        vso_ref[0, h, 0:HEAD_DIM, :] = vs_ref[:, sl].T.astype(BF16)
        vso_ref[0, h, HEAD_DIM:VT_ROWS, :] = ones
        vwo_ref[0, h, 0:HEAD_DIM, :] = vw_ref[:, sl].T.astype(BF16)
        vwo_ref[0, h, HEAD_DIM:VT_ROWS, :] = ones


def _nsa_prep(p, seq, cosf, sina, sinb, qn, ksn, kwn, tt):
    m = p.shape[0]
    nt = seq // tt
    kvw = NSA_KV_WIDTH

    def col(off, width):
        return pl.BlockSpec((tt, width), lambda i: (i, off // width))

    def tab():
        return pl.BlockSpec((tt, LANES), lambda i: (i % nt, 0))

    def vec():
        return pl.BlockSpec((1, HEAD_DIM), lambda i: (0, 0))

    def vt_spec():
        return pl.BlockSpec((1, NSA_KV_HEADS, VT_ROWS, tt), lambda i: (i // nt, 0, 0, i % nt))

    return pl.pallas_call(
        _nsa_prep_kernel,
        grid=(m // tt,),
        in_specs=[col(OFF_Q, NSA_WIDTH), col(OFF_KS, kvw), col(OFF_VS, kvw), col(OFF_KW, kvw), col(OFF_VW, kvw),
                  tab(), tab(), tab(), vec(), vec(), vec()],
        out_specs=[pl.BlockSpec((tt, NSA_WIDTH), lambda i: (i, 0)),
                   pl.BlockSpec((tt, NSA_WIDTH), lambda i: (i, 0)),
                   pl.BlockSpec((tt, kvw), lambda i: (i, 0)),
                   vt_spec(),
                   pl.BlockSpec((tt, kvw), lambda i: (i, 0)),
                   vt_spec()],
        out_shape=[jax.ShapeDtypeStruct((m, NSA_WIDTH), BF16),
                   jax.ShapeDtypeStruct((m, NSA_WIDTH), BF16),
                   jax.ShapeDtypeStruct((m, kvw), BF16),
                   jax.ShapeDtypeStruct((m // seq, NSA_KV_HEADS, VT_ROWS, seq), BF16),
                   jax.ShapeDtypeStruct((m, kvw), BF16),
                   jax.ShapeDtypeStruct((m // seq, NSA_KV_HEADS, VT_ROWS, seq), BF16)],
        compiler_params=_cparams(("parallel",)),
        name="nsa_prep",
    )(p, p, p, p, p, cosf, sina, sinb, qn, ksn, kwn)


def _nsa_cmp_kernel(k_ref, v_ref, pek_ref, w1k_ref, w2k_ref, pev_ref, w1v_ref, w2v_ref, kn_ref,
                    kc_ref, vc_ref):
    ng = k_ref.shape[0] // CMP_STRIDE
    rowid = lax.broadcasted_iota(jnp.int32, (ng, HEAD_DIM), 0)

    def compress(x_ref, pe_ref, w1_ref, w2_ref):
        first = jnp.zeros((ng, HEAD_DIM), F32)
        second = jnp.zeros((ng, HEAD_DIM), F32)
        for j in range(CMP_STRIDE):
            xj = x_ref[pl.ds(j, ng, stride=CMP_STRIDE), :]
            lo = (xj + pe_ref[j:j + 1, :]).astype(BF16)
            hi = (xj + pe_ref[CMP_STRIDE + j:CMP_STRIDE + j + 1, :]).astype(BF16)
            first += _dot(lo, w1_ref[j * HEAD_DIM:(j + 1) * HEAD_DIM, :])
            second += _dot(hi, w1_ref[(CMP_STRIDE + j) * HEAD_DIM:(CMP_STRIDE + j + 1) * HEAD_DIM, :])
        pre = first + pltpu.roll(second, ng - 1, 0)
        return _dot(_gelu_tanh(pre).astype(BF16), w2_ref[...])

    kc = _rms(compress(k_ref, pek_ref, w1k_ref, w2k_ref), kn_ref[...])
    vc = compress(v_ref, pev_ref, w1v_ref, w2v_ref)
    valid = rowid < ng - 1
    kc_ref[0, 0] = jnp.where(valid, kc, 0.0).astype(BF16)
    vc_ref[0, 0] = jnp.where(valid, vc, 0.0).T.astype(BF16)


def _nsa_cmp(p, batch, seq, pek, w1k, w2k, pev, w1v, w2v, kn):
    ng = seq // CMP_STRIDE
    hd = HEAD_DIM
    kvh = NSA_KV_HEADS

    def full(shape):
        return pl.BlockSpec(shape, lambda b, h: (0,) * len(shape))

    return pl.pallas_call(
        _nsa_cmp_kernel,
        grid=(batch, kvh),
        in_specs=[
            pl.BlockSpec((seq, hd), lambda b, h: (b, OFF_KC // hd + h)),
            pl.BlockSpec((seq, hd), lambda b, h: (b, OFF_VC // hd + h)),
            full((CMP_BLOCK, hd)), full((CMP_BLOCK * hd, hd)), full((hd, hd)),
            full((CMP_BLOCK, hd)), full((CMP_BLOCK * hd, hd)), full((hd, hd)),
            full((1, hd)),
        ],
        out_specs=[pl.BlockSpec((1, 1, ng, hd), lambda b, h: (b, h, 0, 0)),
                   pl.BlockSpec((1, 1, hd, ng), lambda b, h: (b, h, 0, 0))],
        out_shape=[jax.ShapeDtypeStruct((batch, kvh, ng, hd), BF16),
                   jax.ShapeDtypeStruct((batch, kvh, hd, ng), BF16)],
        compiler_params=_cparams(("parallel", "parallel")),
        name="nsa_cmp",
    )(p, p, pek, w1k, w2k, pev, w1v, w2v, kn)


def _nsa_attn_kernel(qc_ref, qr_ref, kc_ref, vct_ref, ks_ref, vst_ref, kw_ref, vwt_ref, gl_ref, gb_ref, ovl_ref,
                     o_ref, selb_ref, gt_ref, acc_ref, z_ref, p_ref, zc_ref, pcb_ref, zw_ref, pw_ref, ob_ref, *, tq, tk, n_sel):
    seq = ks_ref.shape[0]
    ncp = kc_ref.shape[2]
    ns = seq // SEL_BLOCK
    grp = NSA_GROUP
    hd = HEAD_DIM
    gq = grp * tq
    nblk = tk // SEL_BLOCK
    kv = pl.program_id(1)
    s0 = pl.program_id(2) * tq
    wspan = WINDOW + tq

    qc = jnp.concatenate([qc_ref[:, g * hd:(g + 1) * hd] for g in range(grp)], axis=0)
    qr = jnp.concatenate([qr_ref[:, g * hd:(g + 1) * hd] for g in range(grp)], axis=0)
    t_row = s0 + lax.broadcasted_iota(jnp.int32, (1, tq), 1)
    t_all = s0 + lax.broadcasted_iota(jnp.int32, (1, gq), 1) % tq

    def heads(x):
        return jnp.concatenate([x] * grp, axis=1)

    srow = lax.broadcasted_iota(jnp.int32, (SEL_BLOCK, 1), 0)
    chunk = SEL_BLOCK

    def biased_max(z, nrows, bias_fn):
        mx = jnp.full((8, gq), -MASK_VALUE, F32)
        for c in range(nrows // chunk):
            rs = slice(c * chunk, (c + 1) * chunk)
            zb = z[rs, :]
            if bias_fn is not None:
                zb = zb + heads(bias_fn(c))
                z[rs, :] = zb
            mx = jnp.maximum(mx, jnp.max(zb.reshape(chunk // 8, 8, gq), axis=0))
        return jnp.max(mx, axis=0, keepdims=True)

    start = pl.multiple_of(jnp.maximum(s0 - WINDOW, 0), tq)
    zc_ref[...] = _dot_nt(kc_ref[0, 0], qc)
    zw_ref[...] = _dot_nt(kw_ref[pl.ds(start, wspan), :], qr)

    def cmp_bias(c):
        cend = (c * chunk + srow) * CMP_STRIDE + (CMP_BLOCK - 1)
        return jnp.where(cend <= t_row, 0.0, -MASK_VALUE)

    def win_bias(c):
        kpos = start + c * chunk + srow
        return jnp.where((kpos <= t_row) & (kpos > t_row - WINDOW), 0.0, -MASK_VALUE)

    mc = biased_max(zc_ref, ncp, cmp_bias)
    mw = biased_max(zw_ref, wspan, win_bias)
    den8 = jnp.zeros((8, gq), F32)
    for c in range(ncp // chunk):
        rs = slice(c * chunk, (c + 1) * chunk)
        e = jnp.exp2(zc_ref[rs, :] - mc)
        zc_ref[rs, :] = e
        pcb_ref[rs, :] = e.astype(BF16)
        den8 = den8 + jnp.sum(e.reshape(chunk // 8, 8, gq), axis=0)
    for c in range(wspan // chunk):
        rs = slice(c * chunk, (c + 1) * chunk)
        pw_ref[rs, :] = jnp.exp2(zw_ref[rs, :] - mw).astype(BF16)
    inv_c = jnp.where(t_all >= CMP_BLOCK - 1, 1.0 / jnp.sum(den8, axis=0, keepdims=True), 0.0)
    ob_ref[0] = _dot(vct_ref[0, 0], pcb_ref[...]) * inv_c
    ow = _dot(vwt_ref[0, 0, :, pl.ds(start, wspan)], pw_ref[...])
    ob_ref[1] = ow[0:hd, :] * (1.0 / ow[hd:hd + 1, :])

    pcs = []
    for c in range(ncp // chunk):
        pn = zc_ref[c * chunk:(c + 1) * chunk, :] * inv_c
        pcs.append(sum(pn[:, g * tq:(g + 1) * tq] for g in range(1, grp)) + pn[:, 0:tq])
    pcs = jnp.concatenate(pcs, axis=0)
    pcs_hi = pcs.astype(BF16)
    pcs_lo = (pcs - pcs_hi.astype(F32)).astype(BF16)
    imp = _dot(ovl_ref[...], pcs_hi) + _dot(ovl_ref[...], pcs_lo)
    jrow = lax.broadcasted_iota(jnp.int32, (ns, tq), 0)
    cur = (s0 + lax.broadcasted_iota(jnp.int32, (ns, tq), 1)) // SEL_BLOCK
    forced = (jrow == 0) | (jrow == cur) | (jrow == cur - 1)
    imp = jnp.where(forced, MASK_VALUE, jnp.where(jrow <= cur, imp, -MASK_VALUE))
    ngrp = ns // 8
    imp_g = [imp[8 * r:8 * r + 8, :] for r in range(ngrp)]
    sub = lax.broadcasted_iota(jnp.int32, (8, tq), 0)
    rank_g = [jnp.zeros((8, tq), F32) for _ in range(ngrp)]
    for i in range(ns):
        other = jnp.broadcast_to(imp[i:i + 1, :], (8, tq))
        for r in range(ngrp):
            if 8 * r > i:
                ahead = jnp.where(other >= imp_g[r], 1.0, 0.0)
            elif 8 * r + 7 < i:
                ahead = jnp.where(other > imp_g[r], 1.0, 0.0)
            else:
                ahead = jnp.where(sub > i - 8 * r, jnp.where(other >= imp_g[r], 1.0, 0.0),
                                  jnp.where(other > imp_g[r], 1.0, 0.0))
            rank_g[r] = rank_g[r] + ahead
    for r in range(ngrp):
        selb_ref[8 * r:8 * r + 8, :] = jnp.where(rank_g[r] < n_sel, 0.0, -MASK_VALUE)

    acc_ref[...] = jnp.zeros_like(acc_ref)

    n_kt = (s0 + tq + tk - 1) // tk

    def scores(kt):
        return _dot_nt(ks_ref[pl.ds(pl.multiple_of(kt * tk, tk), tk), :], qr)

    n_pairs = (n_kt + 1) // 2
    z_ref[0] = scores(0)

    def tile_step(kt, m, slot):
        z_ref[1 - slot] = scores(jnp.minimum(kt + 1, 2 * n_pairs - 1))
        koff = pl.multiple_of(kt * tk, tk)
        zs = z_ref.at[slot]

        def sel_bias(j):
            return (jnp.broadcast_to(selb_ref[pl.ds(kt * nblk + j, 1), :], (chunk, tq))
                    + jnp.where(koff + j * chunk + srow <= t_row, 0.0, -MASK_VALUE))

        m_new = jnp.maximum(m, biased_max(zs, tk, sel_bias))
        for j in range(nblk):
            rs = slice(j * chunk, (j + 1) * chunk)
            p_ref[rs, :] = jnp.exp2(zs[rs, :] - m_new).astype(BF16)
        acc_ref[...] = acc_ref[...] * jnp.exp2(m - m_new) + _dot(vst_ref[0, 0, :, pl.ds(koff, tk)], p_ref[...])
        return m_new

    def pair_body(pi, m):
        return tile_step(2 * pi + 1, tile_step(2 * pi, m, 0), 1)

    lax.fori_loop(0, n_pairs, pair_body, jnp.full((1, gq), -MASK_VALUE, F32))
    o_sel = acc_ref[0:hd, :] * (1.0 / acc_ref[hd:hd + 1, :])

    gt_ref[...] = _sigmoid(gl_ref[...] + gb_ref[...]).T
    for g in range(grp):
        base = MISC_GATE + (kv * grp + g) * 3
        cs = slice(g * tq, (g + 1) * tq)
        og = (gt_ref[pl.ds(base, 1), :] * ob_ref[0, :, cs] + gt_ref[pl.ds(base + 1, 1), :] * o_sel[:, cs]
              + gt_ref[pl.ds(base + 2, 1), :] * ob_ref[1, :, cs])
        o_ref[:, g * hd:(g + 1) * hd] = og.T


def _nsa_attn(p, qc, qr, kc, vct, ks, vst, kw, vwt, gate_b, ovl, batch, seq, tq, tk):
    hd = HEAD_DIM
    gw = NSA_GROUP * hd
    nq = seq // tq
    ns = seq // SEL_BLOCK
    ncp = kc.shape[2]
    n_sel = min(N_SELECT, ns)
    assert seq % (2 * tk) == 0 and tk % SEL_BLOCK == 0 and seq >= WINDOW + tq and WINDOW % tq == 0

    def qspec():
        return pl.BlockSpec((tq, gw), lambda b, h, i: (b * nq + i, h))

    def kspec(width):
        return pl.BlockSpec((seq, width), lambda b, h, i: (b, h))

    def vtspec():
        return pl.BlockSpec((1, 1, VT_ROWS, seq), lambda b, h, i: (b, h, 0, 0))

    return pl.pallas_call(
        functools.partial(_nsa_attn_kernel, tq=tq, tk=tk, n_sel=n_sel),
        grid=(batch, NSA_KV_HEADS, nq),
        in_specs=[qspec(), qspec(),
                  pl.BlockSpec((1, 1, ncp, hd), lambda b, h, i: (b, h, 0, 0)),
                  pl.BlockSpec((1, 1, hd, ncp), lambda b, h, i: (b, h, 0, 0)),
                  kspec(hd), vtspec(), kspec(hd), vtspec(),
                  pl.BlockSpec((tq, LANES), lambda b, h, i: (b * nq + i, OFF_MISC // LANES)),
                  pl.BlockSpec((1, LANES), lambda b, h, i: (0, 0)),
                  pl.BlockSpec((ns, ncp), lambda b, h, i: (0, 0))],
        out_specs=pl.BlockSpec((tq, gw), lambda b, h, i: (b * nq + i, h)),
        out_shape=jax.ShapeDtypeStruct((batch * seq, NSA_WIDTH), F32),
        scratch_shapes=[pltpu.VMEM((ns, tq), F32), pltpu.VMEM((LANES, tq), F32),
                        pltpu.VMEM((VT_ROWS, NSA_GROUP * tq), F32),
                        pltpu.VMEM((2, tk, NSA_GROUP * tq), F32), pltpu.VMEM((tk, NSA_GROUP * tq), BF16),
                        pltpu.VMEM((ncp, NSA_GROUP * tq), F32), pltpu.VMEM((ncp, NSA_GROUP * tq), BF16),
                        pltpu.VMEM((WINDOW + tq, NSA_GROUP * tq), F32),
                        pltpu.VMEM((WINDOW + tq, NSA_GROUP * tq), BF16),
                        pltpu.VMEM((2, hd, NSA_GROUP * tq), F32)],
        compiler_params=_cparams(("parallel", "parallel", "arbitrary")),
        name="nsa_attn",
    )(qc, qr, kc, vct, ks, vst, kw, vwt, p, gate_b, ovl)


def _gla_kernel(q_ref, k_ref, v_ref, g_ref, misc_ref, a2_ref, ab_ref, gn_ref, ex_ref, o_ref,
                st_ref, qe_ref, kd_ref, vb_ref, ds_ref, oi_ref, od_ref):
    nb, tt, wq = q_ref.shape
    wv = v_ref.shape[2]
    step = GLA_STEP
    ng = tt // step
    nh = GLA_HEADS
    hl = LANES
    per = hl // GLA_DK

    @pl.when(pl.program_id(1) == 0)
    def _():
        st_ref[...] = jnp.zeros_like(st_ref)

    pos = lax.broadcasted_iota(jnp.int32, (tt, wq), 0) % step
    pos3 = pos.reshape(ng, step, wq)
    lane = lax.broadcasted_iota(jnp.int32, (tt, hl), 1)

    for s_i in range(nb):
        x = _dot(misc_ref[s_i].astype(BF16), a2_ref[...]) + ab_ref[...]
        b = (-_softplus(-x)) * np.float32(np.log2(np.e) / GLA_TAU)
        s = 1
        while s < step:
            b = b + jnp.where(pos >= s, pltpu.roll(b, s, 0), 0.0)
            s *= 2

        q3 = (q_ref[s_i] * np.float32(GLA_DK ** -0.5)).reshape(ng, step, wq)
        k3 = k_ref[s_i].reshape(ng, step, wq)
        v3 = v_ref[s_i].reshape(ng, step, wv)
        b3 = b.reshape(ng, step, wq)
        b_last = b3[:, step - 1:step, :]
        qe_ref[s_i] = (q3 * jnp.exp2(b3)).reshape(tt, wq).astype(BF16)
        kd = (k3 * jnp.exp2(b_last - b3)).reshape(tt, wq)
        for h in range(nh):
            grp_sl = slice((h // per) * hl, (h // per + 1) * hl)
            mine = (lane >= (h % per) * GLA_DK) & (lane < (h % per + 1) * GLA_DK)
            kd_ref[s_i * nh + h] = jnp.where(mine, kd[:, grp_sl], 0.0).astype(BF16)
        vb_ref[s_i] = v_ref[s_i].astype(BF16)
        ds_ref[s_i] = jnp.exp2(b_last)

        odiag = jnp.zeros((tt, wv), F32)
        for j in range(step):
            dec = jnp.exp2(jnp.where(pos3 >= j, b3 - b3[:, j:j + 1, :], -MASK_VALUE))
            term = (q3 * k3[:, j:j + 1, :] * dec).reshape(tt, wq).astype(BF16)
            vj = jnp.broadcast_to(v3[:, j:j + 1, :], (ng, step, wv)).reshape(tt, wv)
            odiag = odiag + _dot(term, ex_ref[...]) * vj
        od_ref[s_i] = odiag

    def body(gi, carry):
        r0 = pl.multiple_of(gi * step, step)
        for s_i in range(nb):
            dsg = ds_ref[s_i, gi]
            for h in range(nh):
                sl = slice(h * hl, (h + 1) * hl)
                grp_sl = slice((h // per) * hl, (h // per + 1) * hl)
                st = st_ref[s_i * nh + h]
                oi_ref[s_i, pl.ds(r0, step), sl] = _dot_nt(qe_ref[s_i, pl.ds(r0, step), grp_sl], st.astype(BF16))
                upd = _dot_tn(vb_ref[s_i, pl.ds(r0, step), sl], kd_ref[s_i * nh + h, pl.ds(r0, step), :])
                st_ref[s_i * nh + h] = st * dsg[:, grp_sl] + upd
        return carry

    lax.fori_loop(0, ng, body, 0)

    for s_i in range(nb):
        o = oi_ref[s_i] + od_ref[s_i]
        gate = g_ref[s_i]
        gate = gate * _sigmoid(gate)
        for h in range(nh):
            sl = slice(h * hl, (h + 1) * hl)
            o_ref[s_i, :, sl] = (_rms(o[:, sl], gn_ref[...]) * gate[:, sl]).astype(o_ref.dtype)


def _gla(p, batch, seq, a2, ab, gn, tt, nb):
    wq = GLA_QK_WIDTH
    wv = GLA_WIDTH
    nt = seq // tt
    ng = tt // GLA_STEP
    expand = jnp.asarray(np.kron(np.eye(GLA_HEADS), np.ones((GLA_DK, GLA_DV))), dtype=BF16)
    p3 = p.reshape(batch, seq, p.shape[1])

    def col(off, width):
        return pl.BlockSpec((nb, tt, width), lambda b, t: (b, t, off // width))

    out = pl.pallas_call(
        _gla_kernel,
        grid=(batch // nb, nt),
        in_specs=[col(OFF_GQ, wq), col(OFF_GK, wq), col(OFF_GV, wv), col(OFF_GG, wv), col(OFF_MISC, LANES),
                  pl.BlockSpec((LANES, wq), lambda b, t: (0, 0)),
                  pl.BlockSpec((1, wq), lambda b, t: (0, 0)),
                  pl.BlockSpec((1, GLA_DV), lambda b, t: (0, 0)),
                  pl.BlockSpec((wq, wv), lambda b, t: (0, 0))],
        out_specs=pl.BlockSpec((nb, tt, wv), lambda b, t: (b, t, 0)),
        out_shape=jax.ShapeDtypeStruct((batch, seq, wv), BF16),
        scratch_shapes=[
            pltpu.VMEM((nb * GLA_HEADS, GLA_DV, LANES), F32),
            pltpu.VMEM((nb, tt, wq), BF16),
            pltpu.VMEM((nb * GLA_HEADS, tt, LANES), BF16),
            pltpu.VMEM((nb, tt, wv), BF16),
            pltpu.VMEM((nb, ng, 1, wq), F32),
            pltpu.VMEM((nb, tt, wv), F32),
            pltpu.VMEM((nb, tt, wv), F32),
        ],
        compiler_params=_cparams(("parallel", "arbitrary")),
        name="gla",
    )(p3, p3, p3, p3, p3, a2, ab, gn, expand)
    return out.reshape(batch * seq, wv)


def _pack_w_in(w_in):
    d = w_in.shape[0]
    o_gate = OFF_VW + NSA_KV_WIDTH
    o_gq = o_gate + NSA_HEADS * 3
    o_ga = o_gq + 2 * GLA_QK_WIDTH + 2 * GLA_WIDTH
    tail = jnp.zeros((d, N_PACKED - OFF_MISC - NSA_HEADS * 3 - GLA_RANK), w_in.dtype)
    packed = jnp.concatenate([w_in[:, :o_gate], w_in[:, o_gq:o_ga],
                              w_in[:, o_gate:o_gq], w_in[:, o_ga:o_ga + GLA_RANK], tail], axis=1)
    assert packed.shape[1] == N_PACKED
    return packed.astype(BF16)


def _block_diag(w):
    nb, bi, bo = w.shape
    eye = jnp.eye(nb, dtype=w.dtype)
    return (eye[:, None, :, None] * w[:, :, None, :]).reshape(nb * bi, nb * bo)


def _rope_lane_tables(seq):
    inv = 1.0 / (ROPE_THETA ** (jnp.arange(0, ROPE_DIM, 2, dtype=F32) / ROPE_DIM))
    ang = jnp.arange(seq, dtype=F32)[:, None] * inv[None, :]
    cos, sin = jnp.cos(ang), jnp.sin(ang)
    rest = LANES - ROPE_DIM
    zeros_h = jnp.zeros((seq, ROPE_HALF), F32)
    cosf = jnp.concatenate([cos, cos, jnp.ones((seq, rest), F32)], axis=1)
    sina = jnp.concatenate([-sin, zeros_h, jnp.zeros((seq, rest), F32)], axis=1)
    sinb = jnp.concatenate([zeros_h, sin, jnp.zeros((seq, rest), F32)], axis=1)
    return cosf, sina, sinb


def _overlap_t(seq, ncp):
    ns = seq // SEL_BLOCK
    nc = (seq - CMP_BLOCK) // CMP_STRIDE + 1
    cs = np.arange(ncp) * CMP_STRIDE
    ss = np.arange(ns) * SEL_BLOCK
    ov = (cs[None, :] < ss[:, None] + SEL_BLOCK) & (cs[None, :] + CMP_BLOCK > ss[:, None]) & (np.arange(ncp) < nc)[None, :]
    return jnp.asarray(ov.astype(np.float32), dtype=BF16)


def _row(v):
    return v.reshape(1, -1)


def _mixers(p, batch, seq, l, prm, tables, ovl_t):
    cosf, sina, sinb = tables
    wg = jnp.concatenate([_block_diag(prm['lru_gate_a_w'][l]), _block_diag(prm['lru_gate_x_w'][l])], axis=1).astype(BF16)
    bg = jnp.concatenate([prm['lru_gate_a_b'][l], prm['lru_gate_x_b'][l]]).reshape(1, -1)
    ya = _lru(p, batch, seq, prm['lru_conv_w'][l], _row(prm['lru_conv_b'][l]), wg, bg,
              _row(prm['lru_lambda'][l]), _row(prm['lru_out_norm'][l]), tt=min(512, seq))
    qc, qr, ks, vs, kw, vw = _nsa_prep(p, seq, cosf, sina, sinb, _row(prm['nsa_q_norm'][l]),
                                       _row(prm['nsa_k_sel_norm'][l]), _row(prm['nsa_k_win_norm'][l]),
                                       tt=min(512, seq))
    kc, vc = _nsa_cmp(p, batch, seq, prm['nsa_cmp_pe_k'][l], prm['nsa_cmp_w1_k'][l].astype(BF16),
                      prm['nsa_cmp_w2_k'][l].astype(BF16), prm['nsa_cmp_pe_v'][l],
                      prm['nsa_cmp_w1_v'][l].astype(BF16), prm['nsa_cmp_w2_v'][l].astype(BF16),
                      _row(prm['nsa_k_cmp_norm'][l]))
    gate_b = jnp.pad(prm['nsa_gate_b'][l], (MISC_GATE, LANES - MISC_GATE - NSA_HEADS * 3)).reshape(1, LANES)
    yb = _nsa_attn(p, qc, qr, kc, vc, ks, vs, kw, vw, gate_b, ovl_t, batch, seq, tq=128, tk=min(512, seq))
    a2 = jnp.zeros((LANES, GLA_QK_WIDTH), F32).at[MISC_GA:MISC_GA + GLA_RANK].set(prm['gla_a_w2'][l]).astype(BF16)
    ab = _row(prm['gla_a_b'][l])
    yc = _gla(p, batch, seq, a2, ab, _row(prm['gla_out_norm'][l]), tt=min(256, seq),
              nb=4 if batch % 4 == 0 else 1)
    return ya, yb, yc


def kernel(x, ffn1_norm, ffn1_w_gate, ffn1_w_up, ffn1_w_down, mix_norm, w_in, lru_conv_w, lru_conv_b, lru_gate_a_w, lru_gate_a_b, lru_gate_x_w, lru_gate_x_b, lru_lambda, lru_out_norm, nsa_q_norm, nsa_k_cmp_norm, nsa_k_sel_norm, nsa_k_win_norm, nsa_cmp_pe_k, nsa_cmp_w1_k, nsa_cmp_w2_k, nsa_cmp_pe_v, nsa_cmp_w1_v, nsa_cmp_w2_v, nsa_gate_b, nsa_out_norm, gla_a_w2, gla_a_b, gla_out_norm, w_out, ffn2_norm, ffn2_w_gate, ffn2_w_up, ffn2_w_down):
    prm = dict(lru_conv_w=lru_conv_w, lru_conv_b=lru_conv_b, lru_gate_a_w=lru_gate_a_w, lru_gate_a_b=lru_gate_a_b,
               lru_gate_x_w=lru_gate_x_w, lru_gate_x_b=lru_gate_x_b, lru_lambda=lru_lambda, lru_out_norm=lru_out_norm,
               nsa_q_norm=nsa_q_norm, nsa_k_cmp_norm=nsa_k_cmp_norm, nsa_k_sel_norm=nsa_k_sel_norm,
               nsa_k_win_norm=nsa_k_win_norm, nsa_cmp_pe_k=nsa_cmp_pe_k, nsa_cmp_w1_k=nsa_cmp_w1_k,
               nsa_cmp_w2_k=nsa_cmp_w2_k, nsa_cmp_pe_v=nsa_cmp_pe_v, nsa_cmp_w1_v=nsa_cmp_w1_v,
               nsa_cmp_w2_v=nsa_cmp_w2_v, nsa_gate_b=nsa_gate_b, gla_a_w2=gla_a_w2, gla_a_b=gla_a_b,
               gla_out_norm=gla_out_norm)
    batch, seq, d = x.shape
    depth = w_in.shape[0]
    m = batch * seq
    tables = _rope_lane_tables(seq)
    ncp = seq // CMP_STRIDE
    ovl_t = _overlap_t(seq, ncp)
    tm = min(512, m)
    tm_ffn = min(1024, m)
    tf = 512
    xf = x.reshape(m, d)
    ffn1 = [w.astype(BF16) for w in (ffn1_w_gate, ffn1_w_up, ffn1_w_down)]
    ffn2 = [w.astype(BF16) for w in (ffn2_w_gate, ffn2_w_up, ffn2_w_down)]
    for l in range(depth):
        xf = _ffn(xf, _row(ffn1_norm[l]), *ffn1, l, tm_ffn, tf)
        p = _mix_in(xf, _row(mix_norm[l]), _pack_w_in(w_in[l]), tm_ffn, N_PACKED // MIX_IN_TILES)
        ya, yb, yc = _mixers(p, batch, seq, l, prm, tables, ovl_t)
        xf = _mix_out(xf, ya, yb, yc, _row(nsa_out_norm[l]), w_out[l].astype(BF16), tm)
        xf = _ffn(xf, _row(ffn2_norm[l]), *ffn2, l, tm_ffn, tf)
    return xf.reshape(batch, seq, d)
```

```python
import functools

import numpy as np
import jax
import jax.numpy as jnp
from jax import lax
from jax.experimental import pallas as pl
from jax.experimental.pallas import tpu as pltpu

F32 = jnp.float32
BF16 = jnp.bfloat16

NORM_EPS = 1e-6
D_MODEL = 2048
D_FF = 5632

LRU_WIDTH = 512
LRU_BLOCKS = 8
LRU_BLOCK_SIZE = LRU_WIDTH // LRU_BLOCKS
CONV_WIDTH = 4
LRU_C = 8.0

HEAD_DIM = 128
NSA_HEADS = 8
NSA_KV_HEADS = 2
NSA_GROUP = NSA_HEADS // NSA_KV_HEADS
NSA_WIDTH = NSA_HEADS * HEAD_DIM
NSA_KV_WIDTH = NSA_KV_HEADS * HEAD_DIM
ROPE_DIM = HEAD_DIM // 4
ROPE_HALF = ROPE_DIM // 2
ROPE_THETA = 500000.0
CMP_BLOCK = 32
CMP_STRIDE = 16
SEL_BLOCK = 64
N_SELECT = 16
WINDOW = 512
MASK_VALUE = 1e30
QK_LOG2_SCALE = float(HEAD_DIM ** -0.5 * np.log2(np.e))
VT_ROWS = HEAD_DIM + 16

GLA_DV = 128
GLA_HEADS = 4
GLA_DK = 64
GLA_WIDTH = GLA_HEADS * GLA_DV
GLA_QK_WIDTH = GLA_HEADS * GLA_DK
GLA_RANK = 16
GLA_TAU = 16.0
GLA_STEP = 16

LANES = 128

OFF_LRU_X = 0
OFF_LRU_Y = OFF_LRU_X + LRU_WIDTH
OFF_Q = OFF_LRU_Y + LRU_WIDTH
OFF_KC = OFF_Q + NSA_WIDTH
OFF_VC = OFF_KC + NSA_KV_WIDTH
OFF_KS = OFF_VC + NSA_KV_WIDTH
OFF_VS = OFF_KS + NSA_KV_WIDTH
OFF_KW = OFF_VS + NSA_KV_WIDTH
OFF_VW = OFF_KW + NSA_KV_WIDTH
OFF_GQ = OFF_VW + NSA_KV_WIDTH
OFF_GK = OFF_GQ + GLA_QK_WIDTH
OFF_GV = OFF_GK + GLA_QK_WIDTH
OFF_GG = OFF_GV + GLA_WIDTH
OFF_MISC = OFF_GG + GLA_WIDTH
MIX_IN_TILES = 3
N_PACKED = -(-(OFF_MISC + LANES) // (2 * LANES * MIX_IN_TILES)) * (2 * LANES * MIX_IN_TILES)
MISC_GATE = 0
MISC_GA = NSA_HEADS * 3

VMEM_LIMIT = 56 * 1024 * 1024
FFN_SUB = 256


def _cparams(sem):
    return pltpu.CompilerParams(dimension_semantics=sem, vmem_limit_bytes=VMEM_LIMIT)


def _sigmoid(x):
    return 0.5 * jnp.tanh(0.5 * x) + 0.5


def _softplus(x):
    return jnp.maximum(x, 0.0) + jnp.log1p(jnp.exp(-jnp.abs(x)))


def _gelu_tanh(x):
    c = np.float32(np.sqrt(2.0 / np.pi))
    return x * (0.5 * (1.0 + jnp.tanh(c * (x + 0.044715 * (x * x * x)))))


def _rms(x, g):
    return x * lax.rsqrt(jnp.mean(x * x, axis=-1, keepdims=True) + NORM_EPS) * g


def _dot(a, b):
    return jnp.dot(a, b, preferred_element_type=F32)


def _dot_nt(a, b):
    return lax.dot_general(a, b, (((1,), (1,)), ((), ())), preferred_element_type=F32)


def _dot_tn(a, b):
    return lax.dot_general(a, b, (((0,), (0,)), ((), ())), preferred_element_type=F32)


def _ffn_kernel(x_ref, g_ref, wg_ref, wu_ref, wd_ref, o_ref, h_ref):
    j = pl.program_id(1)

    @pl.when(j == 0)
    def _():
        x = x_ref[...]
        h_ref[...] = _rms(x, g_ref[...]).astype(BF16)
        o_ref[...] = x

    h = h_ref[...]
    tf = wg_ref.shape[1]
    acc = None
    for c in range(tf // FFN_SUB):
        cs = slice(c * FFN_SUB, (c + 1) * FFN_SUB)
        gate = _dot(h, wg_ref[:, cs])
        up = _dot(h, wu_ref[:, cs])
        act = (gate * _sigmoid(gate)) * (0.5 * up)
        part = _dot(act.astype(BF16), wd_ref[cs, :])
        acc = part if acc is None else acc + part
    o_ref[...] += acc


def _ffn(x, g, wg, wu, wd, layer, tm, tf):
    m, d = x.shape
    f = wg.shape[2]
    return pl.pallas_call(
        _ffn_kernel,
        grid=(m // tm, f // tf),
        in_specs=[
            pl.BlockSpec((tm, d), lambda i, j: (i, 0)),
            pl.BlockSpec((1, d), lambda i, j: (0, 0)),
            pl.BlockSpec((None, d, tf), lambda i, j: (layer, 0, j)),
            pl.BlockSpec((None, d, tf), lambda i, j: (layer, 0, j)),
            pl.BlockSpec((None, tf, d), lambda i, j: (layer, j, 0)),
        ],
        out_specs=pl.BlockSpec((tm, d), lambda i, j: (i, 0)),
        out_shape=jax.ShapeDtypeStruct((m, d), F32),
        scratch_shapes=[pltpu.VMEM((tm, d), BF16)],
        compiler_params=_cparams(("parallel", "arbitrary")),
        name="ffn",
    )(x, g, wg, wu, wd)


def _mix_in_kernel(x_ref, g_ref, w_ref, o_ref, h_ref):
    @pl.when(pl.program_id(1) == 0)
    def _():
        h_ref[...] = _rms(x_ref[...], g_ref[...]).astype(BF16)

    o_ref[...] = _dot(h_ref[...], w_ref[...])


def _mix_in(x, g, w, tm, tn):
    m, d = x.shape
    n = w.shape[1]
    return pl.pallas_call(
        _mix_in_kernel,
        grid=(m // tm, n // tn),
        in_specs=[
            pl.BlockSpec((tm, d), lambda i, j: (i, 0)),
            pl.BlockSpec((1, d), lambda i, j: (0, 0)),
            pl.BlockSpec((d, tn), lambda i, j: (0, j)),
        ],
        out_specs=pl.BlockSpec((tm, tn), lambda i, j: (i, j)),
        out_shape=jax.ShapeDtypeStruct((m, n), F32),
        scratch_shapes=[pltpu.VMEM((tm, d), BF16)],
        compiler_params=_cparams(("parallel", "arbitrary")),
        name="mix_in",
    )(x, g, w)


def _mix_out_kernel(x_ref, ya_ref, yb_ref, yc_ref, gb_ref, wa_ref, wb_ref, wc_ref, o_ref):
    nb = _rms(yb_ref[...], gb_ref[...]).astype(BF16)
    acc = _dot(ya_ref[...], wa_ref[...])
    acc += _dot(nb, wb_ref[...])
    acc += _dot(yc_ref[...], wc_ref[...])
    o_ref[...] = x_ref[...] + acc


def _mix_out(x, ya, yb, yc, gb, w_out, tm):
    m, d = x.shape
    wa, wb, wc = LRU_WIDTH, NSA_WIDTH, GLA_WIDTH
    return pl.pallas_call(
        _mix_out_kernel,
        grid=(m // tm,),
        in_specs=[
            pl.BlockSpec((tm, d), lambda i: (i, 0)),
            pl.BlockSpec((tm, wa), lambda i: (i, 0)),
            pl.BlockSpec((tm, wb), lambda i: (i, 0)),
            pl.BlockSpec((tm, wc), lambda i: (i, 0)),
            pl.BlockSpec((1, wb), lambda i: (0, 0)),
            pl.BlockSpec((wa, d), lambda i: (0, 0)),
            pl.BlockSpec((wb, d), lambda i: (0, 0)),
            pl.BlockSpec((wc, d), lambda i: (0, 0)),
        ],
        out_specs=pl.BlockSpec((tm, d), lambda i: (i, 0)),
        out_shape=jax.ShapeDtypeStruct((m, d), F32),
        compiler_params=_cparams(("parallel",)),
        name="mix_out",
    )(x, ya, yb, yc, gb, w_out[:wa], w_out[wa:wa + wb], w_out[wa + wb:])


def _lru_kernel(u_ref, y_ref, cw_ref, cb_ref, wg_ref, bg_ref, lam_ref, gn_ref, o_ref,
                ext_ref, a_ref, b_ref, hc_ref):
    tt, w = u_ref.shape

    @pl.when(pl.program_id(1) == 0)
    def _():
        ext_ref[0:8, :] = jnp.zeros((8, w), F32)
        hc_ref[...] = jnp.zeros_like(hc_ref)

    u = u_ref[...]
    ext_ref[8:8 + tt, :] = u
    xc = cb_ref[...] + cw_ref[CONV_WIDTH - 1:CONV_WIDTH, :] * u
    for k in range(CONV_WIDTH - 1):
        sh = CONV_WIDTH - 1 - k
        xc = xc + cw_ref[k:k + 1, :] * ext_ref[8 - sh:8 - sh + tt, :]
    ext_ref[0:8, :] = u[tt - 8:tt, :]

    gates = _dot(xc.astype(BF16), wg_ref[...]) + bg_ref[...]
    r = _sigmoid(gates[:, :w])
    i = _sigmoid(gates[:, w:])
    log_a = (-LRU_C) * r * _softplus(-lam_ref[...])
    a = jnp.exp(log_a)
    a_ref[...] = a
    b_ref[...] = jnp.sqrt(-jnp.tanh(log_a) * (a * a + 1.0)) * i * xc

    row = lax.broadcasted_iota(jnp.int32, (8, w), 0)

    def body(gi, hprev):
        off = pl.multiple_of(gi * 8, 8)
        a8 = a_ref[pl.ds(off, 8), :]
        b8 = b_ref[pl.ds(off, 8), :]
        for s in (1, 2, 4):
            keep = row >= s
            b8 = b8 + a8 * jnp.where(keep, pltpu.roll(b8, s, 0), 0.0)
            a8 = a8 * jnp.where(keep, pltpu.roll(a8, s, 0), 1.0)
        h8 = b8 + a8 * hprev
        b_ref[pl.ds(off, 8), :] = h8
        return jnp.broadcast_to(h8[7:8, :], (8, w))

    hc_ref[...] = lax.fori_loop(0, tt // 8, body, hc_ref[...])

    ya = b_ref[...] * _gelu_tanh(y_ref[...])
    o_ref[...] = _rms(ya, gn_ref[...]).astype(o_ref.dtype)


def _lru(p, batch, seq, cw, cb, wg, bg, lam, gn, tt):
    w = LRU_WIDTH
    nt = seq // tt
    return pl.pallas_call(
        _lru_kernel,
        grid=(batch, nt),
        in_specs=[
            pl.BlockSpec((tt, w), lambda b, t: (b * nt + t, OFF_LRU_X // w)),
            pl.BlockSpec((tt, w), lambda b, t: (b * nt + t, OFF_LRU_Y // w)),
            pl.BlockSpec((CONV_WIDTH, w), lambda b, t: (0, 0)),
            pl.BlockSpec((1, w), lambda b, t: (0, 0)),
            pl.BlockSpec((w, 2 * w), lambda b, t: (0, 0)),
            pl.BlockSpec((1, 2 * w), lambda b, t: (0, 0)),
            pl.BlockSpec((1, w), lambda b, t: (0, 0)),
            pl.BlockSpec((1, w), lambda b, t: (0, 0)),
        ],
        out_specs=pl.BlockSpec((tt, w), lambda b, t: (b * nt + t, 0)),
        out_shape=jax.ShapeDtypeStruct((batch * seq, w), BF16),
        scratch_shapes=[
            pltpu.VMEM((tt + 8, w), F32),
            pltpu.VMEM((tt, w), F32),
            pltpu.VMEM((tt, w), F32),
            pltpu.VMEM((8, w), F32),
        ],
        compiler_params=_cparams(("parallel", "arbitrary")),
        name="lru",
    )(p, p, cw, cb, wg, bg, lam, gn)


def _rope(x, cosf, sina, sinb):
    return x * cosf + pltpu.roll(x, LANES - ROPE_HALF, 1) * sina + pltpu.roll(x, ROPE_HALF, 1) * sinb


def _nsa_prep_kernel(q_ref, ks_ref, vs_ref, kw_ref, vw_ref, cos_ref, sina_ref, sinb_ref,
                     qn_ref, ksn_ref, kwn_ref,
                     qc_ref, qr_ref, kso_ref, vso_ref, kwo_ref, vwo_ref):
    cosf, sina, sinb = cos_ref[...], sina_ref[...], sinb_ref[...]
    for h in range(NSA_HEADS):
        sl = slice(h * HEAD_DIM, (h + 1) * HEAD_DIM)
        qh = _rms(q_ref[:, sl], qn_ref[...])
        qc_ref[:, sl] = (qh * QK_LOG2_SCALE).astype(BF16)
        qr_ref[:, sl] = (_rope(qh, cosf, sina, sinb) * QK_LOG2_SCALE).astype(BF16)
    for h in range(NSA_KV_HEADS):
        sl = slice(h * HEAD_DIM, (h + 1) * HEAD_DIM)
        kso_ref[:, sl] = _rope(_rms(ks_ref[:, sl], ksn_ref[...]), cosf, sina, sinb).astype(BF16)
        kwo_ref[:, sl] = _rope(_rms(kw_ref[:, sl], kwn_ref[...]), cosf, sina, sinb).astype(BF16)
        ones = jnp.ones((VT_ROWS - HEAD_DIM, vs_ref.shape[0]), BF16)
        vso_ref[0, h, 0:HEAD_DIM, :] = vs_ref[:, sl].T.astype(BF16)
        vso_ref[0, h, HEAD_DIM:VT_ROWS, :] = ones
        vwo_ref[0, h, 0:HEAD_DIM, :] = vw_ref[:, sl].T.astype(BF16)
        vwo_ref[0, h, HEAD_DIM:VT_ROWS, :] = ones


def _nsa_prep(p, seq, cosf, sina, sinb, qn, ksn, kwn, tt):
    m = p.shape[0]
    nt = seq // tt
    kvw = NSA_KV_WIDTH

    def col(off, width):
        return pl.BlockSpec((tt, width), lambda i: (i, off // width))

    def tab():
        return pl.BlockSpec((tt, LANES), lambda i: (i % nt, 0))

    def vec():
        return pl.BlockSpec((1, HEAD_DIM), lambda i: (0, 0))

    def vt_spec():
        return pl.BlockSpec((1, NSA_KV_HEADS, VT_ROWS, tt), lambda i: (i // nt, 0, 0, i % nt))

    return pl.pallas_call(
        _nsa_prep_kernel,
        grid=(m // tt,),
        in_specs=[col(OFF_Q, NSA_WIDTH), col(OFF_KS, kvw), col(OFF_VS, kvw), col(OFF_KW, kvw), col(OFF_VW, kvw),
                  tab(), tab(), tab(), vec(), vec(), vec()],
        out_specs=[pl.BlockSpec((tt, NSA_WIDTH), lambda i: (i, 0)),
                   pl.BlockSpec((tt, NSA_WIDTH), lambda i: (i, 0)),
                   pl.BlockSpec((tt, kvw), lambda i: (i, 0)),
                   vt_spec(),
                   pl.BlockSpec((tt, kvw), lambda i: (i, 0)),
                   vt_spec()],
        out_shape=[jax.ShapeDtypeStruct((m, NSA_WIDTH), BF16),
                   jax.ShapeDtypeStruct((m, NSA_WIDTH), BF16),
                   jax.ShapeDtypeStruct((m, kvw), BF16),
                   jax.ShapeDtypeStruct((m // seq, NSA_KV_HEADS, VT_ROWS, seq), BF16),
                   jax.ShapeDtypeStruct((m, kvw), BF16),
                   jax.ShapeDtypeStruct((m // seq, NSA_KV_HEADS, VT_ROWS, seq), BF16)],
        compiler_params=_cparams(("parallel",)),
        name="nsa_prep",
    )(p, p, p, p, p, cosf, sina, sinb, qn, ksn, kwn)


def _nsa_cmp_kernel(k_ref, v_ref, pek_ref, w1k_ref, w2k_ref, pev_ref, w1v_ref, w2v_ref, kn_ref,
                    kc_ref, vc_ref):
    ng = k_ref.shape[0] // CMP_STRIDE
    rowid = lax.broadcasted_iota(jnp.int32, (ng, HEAD_DIM), 0)

    def compress(x_ref, pe_ref, w1_ref, w2_ref):
        first = jnp.zeros((ng, HEAD_DIM), F32)
        second = jnp.zeros((ng, HEAD_DIM), F32)
        for j in range(CMP_STRIDE):
            xj = x_ref[pl.ds(j, ng, stride=CMP_STRIDE), :]
            lo = (xj + pe_ref[j:j + 1, :]).astype(BF16)
            hi = (xj + pe_ref[CMP_STRIDE + j:CMP_STRIDE + j + 1, :]).astype(BF16)
            first += _dot(lo, w1_ref[j * HEAD_DIM:(j + 1) * HEAD_DIM, :])
            second += _dot(hi, w1_ref[(CMP_STRIDE + j) * HEAD_DIM:(CMP_STRIDE + j + 1) * HEAD_DIM, :])
        pre = first + pltpu.roll(second, ng - 1, 0)
        return _dot(_gelu_tanh(pre).astype(BF16), w2_ref[...])

    kc = _rms(compress(k_ref, pek_ref, w1k_ref, w2k_ref), kn_ref[...])
    vc = compress(v_ref, pev_ref, w1v_ref, w2v_ref)
    valid = rowid < ng - 1
    kc_ref[0, 0] = jnp.where(valid, kc, 0.0).astype(BF16)
    vc_ref[0, 0] = jnp.where(valid, vc, 0.0).T.astype(BF16)


def _nsa_cmp(p, batch, seq, pek, w1k, w2k, pev, w1v, w2v, kn):
    ng = seq // CMP_STRIDE
    hd = HEAD_DIM
    kvh = NSA_KV_HEADS

    def full(shape):
        return pl.BlockSpec(shape, lambda b, h: (0,) * len(shape))

    return pl.pallas_call(
        _nsa_cmp_kernel,
        grid=(batch, kvh),
        in_specs=[
            pl.BlockSpec((seq, hd), lambda b, h: (b, OFF_KC // hd + h)),
            pl.BlockSpec((seq, hd), lambda b, h: (b, OFF_VC // hd + h)),
            full((CMP_BLOCK, hd)), full((CMP_BLOCK * hd, hd)), full((hd, hd)),
            full((CMP_BLOCK, hd)), full((CMP_BLOCK * hd, hd)), full((hd, hd)),
            full((1, hd)),
        ],
        out_specs=[pl.BlockSpec((1, 1, ng, hd), lambda b, h: (b, h, 0, 0)),
                   pl.BlockSpec((1, 1, hd, ng), lambda b, h: (b, h, 0, 0))],
        out_shape=[jax.ShapeDtypeStruct((batch, kvh, ng, hd), BF16),
                   jax.ShapeDtypeStruct((batch, kvh, hd, ng), BF16)],
        compiler_params=_cparams(("parallel", "parallel")),
        name="nsa_cmp",
    )(p, p, pek, w1k, w2k, pev, w1v, w2v, kn)


def _nsa_attn_kernel(qc_ref, qr_ref, kc_ref, vct_ref, ks_ref, vst_ref, kw_ref, vwt_ref, gl_ref, gb_ref, ovl_ref,
                     o_ref, selb_ref, gt_ref, acc_ref, z_ref, p_ref, zc_ref, pcb_ref, zw_ref, pw_ref, ob_ref, *, tq, tk, group, n_sel):
    seq = ks_ref.shape[0]
    ncp = kc_ref.shape[2]
    ns = seq // SEL_BLOCK
    grp = NSA_GROUP
    hd = HEAD_DIM
    gq = grp * tq
    nblk = tk // SEL_BLOCK
    kv = pl.program_id(1)
    s0 = pl.program_id(2) * tq
    wspan = WINDOW + tq

    qc = jnp.concatenate([qc_ref[:, g * hd:(g + 1) * hd] for g in range(grp)], axis=0)
    qr = jnp.concatenate([qr_ref[:, g * hd:(g + 1) * hd] for g in range(grp)], axis=0)
    t_row = s0 + lax.broadcasted_iota(jnp.int32, (1, tq), 1)
    t_all = s0 + lax.broadcasted_iota(jnp.int32, (1, gq), 1) % tq

    def heads(x):
        return jnp.concatenate([x] * grp, axis=1)

    srow = lax.broadcasted_iota(jnp.int32, (SEL_BLOCK, 1), 0)
    chunk = SEL_BLOCK

    def biased_max(z, nrows, bias_fn):
        mx = jnp.full((8, gq), -MASK_VALUE, F32)
        for c in range(nrows // chunk):
            rs = slice(c * chunk, (c + 1) * chunk)
            zb = z[rs, :]
            if bias_fn is not None:
                zb = zb + heads(bias_fn(c))
                z[rs, :] = zb
            mx = jnp.maximum(mx, jnp.max(zb.reshape(chunk // 8, 8, gq), axis=0))
        return jnp.max(mx, axis=0, keepdims=True)

    start = pl.multiple_of(jnp.maximum(s0 - WINDOW, 0), tq)
    zc_ref[...] = _dot_nt(kc_ref[0, 0], qc)
    zw_ref[...] = _dot_nt(kw_ref[pl.ds(start, wspan), :], qr)

    def cmp_bias(c):
        cend = (c * chunk + srow) * CMP_STRIDE + (CMP_BLOCK - 1)
        return jnp.where(cend <= t_row, 0.0, -MASK_VALUE)

    def win_bias(c):
        kpos = start + c * chunk + srow
        return jnp.where((kpos <= t_row) & (kpos > t_row - WINDOW), 0.0, -MASK_VALUE)

    mc = biased_max(zc_ref, ncp, cmp_bias)
    mw = biased_max(zw_ref, wspan, win_bias)
    den8 = jnp.zeros((8, gq), F32)
    for c in range(ncp // chunk):
        rs = slice(c * chunk, (c + 1) * chunk)
        e = jnp.exp2(zc_ref[rs, :] - mc)
        zc_ref[rs, :] = e
        pcb_ref[rs, :] = e.astype(BF16)
        den8 = den8 + jnp.sum(e.reshape(chunk // 8, 8, gq), axis=0)
    for c in range(wspan // chunk):
        rs = slice(c * chunk, (c + 1) * chunk)
        pw_ref[rs, :] = jnp.exp2(zw_ref[rs, :] - mw).astype(BF16)
    inv_c = jnp.where(t_all >= CMP_BLOCK - 1, 1.0 / jnp.sum(den8, axis=0, keepdims=True), 0.0)
    ob_ref[0] = _dot(vct_ref[0, 0], pcb_ref[...]) * inv_c
    ow = _dot(vwt_ref[0, 0, :, pl.ds(start, wspan)], pw_ref[...])
    ob_ref[1] = ow[0:hd, :] * (1.0 / ow[hd:hd + 1, :])

    pcs = []
    for c in range(ncp // chunk):
        pn = zc_ref[c * chunk:(c + 1) * chunk, :] * inv_c
        pcs.append(sum(pn[:, g * tq:(g + 1) * tq] for g in range(1, grp)) + pn[:, 0:tq])
    pcs = jnp.concatenate(pcs, axis=0)
    pcs_hi = pcs.astype(BF16)
    pcs_lo = (pcs - pcs_hi.astype(F32)).astype(BF16)
    imp = _dot(ovl_ref[...], pcs_hi) + _dot(ovl_ref[...], pcs_lo)
    jrow = lax.broadcasted_iota(jnp.int32, (ns, tq), 0)
    cur = (s0 + lax.broadcasted_iota(jnp.int32, (ns, tq), 1)) // SEL_BLOCK
    forced = (jrow == 0) | (jrow == cur) | (jrow == cur - 1)
    imp = jnp.where(forced, MASK_VALUE, jnp.where(jrow <= cur, imp, -MASK_VALUE))
    ngrp = ns // 8
    imp_g = [imp[8 * r:8 * r + 8, :] for r in range(ngrp)]
    sub = lax.broadcasted_iota(jnp.int32, (8, tq), 0)
    rank_g = [jnp.zeros((8, tq), F32) for _ in range(ngrp)]
    for i in range(ns):
        other = jnp.broadcast_to(imp[i:i + 1, :], (8, tq))
        for r in range(ngrp):
            if 8 * r > i:
                ahead = jnp.where(other >= imp_g[r], 1.0, 0.0)
            elif 8 * r + 7 < i:
                ahead = jnp.where(other > imp_g[r], 1.0, 0.0)
            else:
                ahead = jnp.where(sub > i - 8 * r, jnp.where(other >= imp_g[r], 1.0, 0.0),
                                  jnp.where(other > imp_g[r], 1.0, 0.0))
            rank_g[r] = rank_g[r] + ahead
    for r in range(ngrp):
        selb_ref[8 * r:8 * r + 8, :] = jnp.where(rank_g[r] < n_sel, 0.0, -MASK_VALUE)

    acc_ref[...] = jnp.zeros_like(acc_ref)

    n_kt = (s0 + tq + tk - 1) // tk

    def scores(kt):
        return _dot_nt(ks_ref[pl.ds(pl.multiple_of(kt * tk, tk), tk), :], qr)

    n_groups = (n_kt + group - 1) // group
    z_ref[0] = scores(0)

    def tile_step(kt, m, slot):
        z_ref[1 - slot] = scores(jnp.minimum(kt + 1, group * n_groups - 1))
        koff = pl.multiple_of(kt * tk, tk)
        zs = z_ref.at[slot]

        def sel_bias(j):
            return (jnp.broadcast_to(selb_ref[pl.ds(kt * nblk + j, 1), :], (chunk, tq))
                    + jnp.where(koff + j * chunk + srow <= t_row, 0.0, -MASK_VALUE))

        m_new = jnp.maximum(m, biased_max(zs, tk, sel_bias))
        for j in range(nblk):
            rs = slice(j * chunk, (j + 1) * chunk)
            p_ref[rs, :] = jnp.exp2(zs[rs, :] - m_new).astype(BF16)
        acc_ref[...] = acc_ref[...] * jnp.exp2(m - m_new) + _dot(vst_ref[0, 0, :, pl.ds(koff, tk)], p_ref[...])
        return m_new

    def group_body(gi, m):
        for u in range(group):
            m = tile_step(group * gi + u, m, u % 2)
        return m

    lax.fori_loop(0, n_groups, group_body, jnp.full((1, gq), -MASK_VALUE, F32))
    o_sel = acc_ref[0:hd, :] * (1.0 / acc_ref[hd:hd + 1, :])

    gt_ref[...] = _sigmoid(gl_ref[...] + gb_ref[...]).T
    for g in range(grp):
        base = MISC_GATE + (kv * grp + g) * 3
        cs = slice(g * tq, (g + 1) * tq)
        og = (gt_ref[pl.ds(base, 1), :] * ob_ref[0, :, cs] + gt_ref[pl.ds(base + 1, 1), :] * o_sel[:, cs]
              + gt_ref[pl.ds(base + 2, 1), :] * ob_ref[1, :, cs])
        o_ref[:, g * hd:(g + 1) * hd] = og.T


def _nsa_attn(p, qc, qr, kc, vct, ks, vst, kw, vwt, gate_b, ovl, batch, seq, tq, tk, group):
    hd = HEAD_DIM
    gw = NSA_GROUP * hd
    nq = seq // tq
    ns = seq // SEL_BLOCK
    ncp = kc.shape[2]
    n_sel = min(N_SELECT, ns)
    assert seq % (group * tk) == 0 and group % 2 == 0 and tk % SEL_BLOCK == 0
    assert seq >= WINDOW + tq and WINDOW % tq == 0

    def qspec():
        return pl.BlockSpec((tq, gw), lambda b, h, i: (b * nq + i, h))

    def kspec(width):
        return pl.BlockSpec((seq, width), lambda b, h, i: (b, h))

    def vtspec():
        return pl.BlockSpec((1, 1, VT_ROWS, seq), lambda b, h, i: (b, h, 0, 0))

    return pl.pallas_call(
        functools.partial(_nsa_attn_kernel, tq=tq, tk=tk, group=group, n_sel=n_sel),
        grid=(batch, NSA_KV_HEADS, nq),
        in_specs=[qspec(), qspec(),
                  pl.BlockSpec((1, 1, ncp, hd), lambda b, h, i: (b, h, 0, 0)),
                  pl.BlockSpec((1, 1, hd, ncp), lambda b, h, i: (b, h, 0, 0)),
                  kspec(hd), vtspec(), kspec(hd), vtspec(),
                  pl.BlockSpec((tq, LANES), lambda b, h, i: (b * nq + i, OFF_MISC // LANES)),
                  pl.BlockSpec((1, LANES), lambda b, h, i: (0, 0)),
                  pl.BlockSpec((ns, ncp), lambda b, h, i: (0, 0))],
        out_specs=pl.BlockSpec((tq, gw), lambda b, h, i: (b * nq + i, h)),
        out_shape=jax.ShapeDtypeStruct((batch * seq, NSA_WIDTH), F32),
        scratch_shapes=[pltpu.VMEM((ns, tq), F32), pltpu.VMEM((LANES, tq), F32),
                        pltpu.VMEM((VT_ROWS, NSA_GROUP * tq), F32),
                        pltpu.VMEM((2, tk, NSA_GROUP * tq), F32), pltpu.VMEM((tk, NSA_GROUP * tq), BF16),
                        pltpu.VMEM((ncp, NSA_GROUP * tq), F32), pltpu.VMEM((ncp, NSA_GROUP * tq), BF16),
                        pltpu.VMEM((WINDOW + tq, NSA_GROUP * tq), F32),
                        pltpu.VMEM((WINDOW + tq, NSA_GROUP * tq), BF16),
                        pltpu.VMEM((2, hd, NSA_GROUP * tq), F32)],
        compiler_params=_cparams(("parallel", "parallel", "arbitrary")),
        name="nsa_attn",
    )(qc, qr, kc, vct, ks, vst, kw, vwt, p, gate_b, ovl)


def _gla_kernel(q_ref, k_ref, v_ref, g_ref, misc_ref, a2_ref, ab_ref, gn_ref, ex_ref, o_ref,
                st_ref, qe_ref, kd_ref, vb_ref, ds_ref, oi_ref, od_ref):
    nb, tt, wq = q_ref.shape
    wv = v_ref.shape[2]
    step = GLA_STEP
    ng = tt // step
    nh = GLA_HEADS
    hl = LANES
    per = hl // GLA_DK

    @pl.when(pl.program_id(1) == 0)
    def _():
        st_ref[...] = jnp.zeros_like(st_ref)

    pos = lax.broadcasted_iota(jnp.int32, (tt, wq), 0) % step
    pos3 = pos.reshape(ng, step, wq)
    lane = lax.broadcasted_iota(jnp.int32, (tt, hl), 1)

    for s_i in range(nb):
        x = _dot(misc_ref[s_i].astype(BF16), a2_ref[...]) + ab_ref[...]
        b = (-_softplus(-x)) * np.float32(np.log2(np.e) / GLA_TAU)
        s = 1
        while s < step:
            b = b + jnp.where(pos >= s, pltpu.roll(b, s, 0), 0.0)
            s *= 2

        q3 = (q_ref[s_i] * np.float32(GLA_DK ** -0.5)).reshape(ng, step, wq)
        k3 = k_ref[s_i].reshape(ng, step, wq)
        v3 = v_ref[s_i].reshape(ng, step, wv)
        b3 = b.reshape(ng, step, wq)
        b_last = b3[:, step - 1:step, :]
        qe_ref[s_i] = (q3 * jnp.exp2(b3)).reshape(tt, wq).astype(BF16)
        kd = (k3 * jnp.exp2(b_last - b3)).reshape(tt, wq)
        for h in range(nh):
            grp_sl = slice((h // per) * hl, (h // per + 1) * hl)
            mine = (lane >= (h % per) * GLA_DK) & (lane < (h % per + 1) * GLA_DK)
            kd_ref[s_i * nh + h] = jnp.where(mine, kd[:, grp_sl], 0.0).astype(BF16)
        vb_ref[s_i] = v_ref[s_i].astype(BF16)
        ds_ref[s_i] = jnp.exp2(b_last)

        odiag = jnp.zeros((tt, wv), F32)
        for j in range(step):
            dec = jnp.exp2(jnp.where(pos3 >= j, b3 - b3[:, j:j + 1, :], -MASK_VALUE))
            term = (q3 * k3[:, j:j + 1, :] * dec).reshape(tt, wq).astype(BF16)
            vj = jnp.broadcast_to(v3[:, j:j + 1, :], (ng, step, wv)).reshape(tt, wv)
            odiag = odiag + _dot(term, ex_ref[...]) * vj
        od_ref[s_i] = odiag

    def body(gi, carry):
        r0 = pl.multiple_of(gi * step, step)
        for s_i in range(nb):
            dsg = ds_ref[s_i, gi]
            for h in range(nh):
                sl = slice(h * hl, (h + 1) * hl)
                grp_sl = slice((h // per) * hl, (h // per + 1) * hl)
                st = st_ref[s_i * nh + h]
                oi_ref[s_i, pl.ds(r0, step), sl] = _dot_nt(qe_ref[s_i, pl.ds(r0, step), grp_sl], st.astype(BF16))
                upd = _dot_tn(vb_ref[s_i, pl.ds(r0, step), sl], kd_ref[s_i * nh + h, pl.ds(r0, step), :])
                st_ref[s_i * nh + h] = st * dsg[:, grp_sl] + upd
        return carry

    lax.fori_loop(0, ng, body, 0)

    for s_i in range(nb):
        o = oi_ref[s_i] + od_ref[s_i]
        gate = g_ref[s_i]
        gate = gate * _sigmoid(gate)
        for h in range(nh):
            sl = slice(h * hl, (h + 1) * hl)
            o_ref[s_i, :, sl] = (_rms(o[:, sl], gn_ref[...]) * gate[:, sl]).astype(o_ref.dtype)


def _gla(p, batch, seq, a2, ab, gn, tt, nb):
    wq = GLA_QK_WIDTH
    wv = GLA_WIDTH
    nt = seq // tt
    ng = tt // GLA_STEP
    expand = jnp.asarray(np.kron(np.eye(GLA_HEADS), np.ones((GLA_DK, GLA_DV))), dtype=BF16)
    p3 = p.reshape(batch, seq, p.shape[1])

    def col(off, width):
        return pl.BlockSpec((nb, tt, width), lambda b, t: (b, t, off // width))

    out = pl.pallas_call(
        _gla_kernel,
        grid=(batch // nb, nt),
        in_specs=[col(OFF_GQ, wq), col(OFF_GK, wq), col(OFF_GV, wv), col(OFF_GG, wv), col(OFF_MISC, LANES),
                  pl.BlockSpec((LANES, wq), lambda b, t: (0, 0)),
                  pl.BlockSpec((1, wq), lambda b, t: (0, 0)),
                  pl.BlockSpec((1, GLA_DV), lambda b, t: (0, 0)),
                  pl.BlockSpec((wq, wv), lambda b, t: (0, 0))],
        out_specs=pl.BlockSpec((nb, tt, wv), lambda b, t: (b, t, 0)),
        out_shape=jax.ShapeDtypeStruct((batch, seq, wv), BF16),
        scratch_shapes=[
            pltpu.VMEM((nb * GLA_HEADS, GLA_DV, LANES), F32),
            pltpu.VMEM((nb, tt, wq), BF16),
            pltpu.VMEM((nb * GLA_HEADS, tt, LANES), BF16),
            pltpu.VMEM((nb, tt, wv), BF16),
            pltpu.VMEM((nb, ng, 1, wq), F32),
            pltpu.VMEM((nb, tt, wv), F32),
            pltpu.VMEM((nb, tt, wv), F32),
        ],
        compiler_params=_cparams(("parallel", "arbitrary")),
        name="gla",
    )(p3, p3, p3, p3, p3, a2, ab, gn, expand)
    return out.reshape(batch * seq, wv)


def _pack_w_in(w_in):
    d = w_in.shape[0]
    o_gate = OFF_VW + NSA_KV_WIDTH
    o_gq = o_gate + NSA_HEADS * 3
    o_ga = o_gq + 2 * GLA_QK_WIDTH + 2 * GLA_WIDTH
    tail = jnp.zeros((d, N_PACKED - OFF_MISC - NSA_HEADS * 3 - GLA_RANK), w_in.dtype)
    packed = jnp.concatenate([w_in[:, :o_gate], w_in[:, o_gq:o_ga],
                              w_in[:, o_gate:o_gq], w_in[:, o_ga:o_ga + GLA_RANK], tail], axis=1)
    assert packed.shape[1] == N_PACKED
    return packed.astype(BF16)


def _block_diag(w):
    nb, bi, bo = w.shape
    eye = jnp.eye(nb, dtype=w.dtype)
    return (eye[:, None, :, None] * w[:, :, None, :]).reshape(nb * bi, nb * bo)


def _rope_lane_tables(seq):
    inv = 1.0 / (ROPE_THETA ** (jnp.arange(0, ROPE_DIM, 2, dtype=F32) / ROPE_DIM))
    ang = jnp.arange(seq, dtype=F32)[:, None] * inv[None, :]
    cos, sin = jnp.cos(ang), jnp.sin(ang)
    rest = LANES - ROPE_DIM
    zeros_h = jnp.zeros((seq, ROPE_HALF), F32)
    cosf = jnp.concatenate([cos, cos, jnp.ones((seq, rest), F32)], axis=1)
    sina = jnp.concatenate([-sin, zeros_h, jnp.zeros((seq, rest), F32)], axis=1)
    sinb = jnp.concatenate([zeros_h, sin, jnp.zeros((seq, rest), F32)], axis=1)
    return cosf, sina, sinb


def _overlap_t(seq, ncp):
    ns = seq // SEL_BLOCK
    nc = (seq - CMP_BLOCK) // CMP_STRIDE + 1
    cs = np.arange(ncp) * CMP_STRIDE
    ss = np.arange(ns) * SEL_BLOCK
    ov = (cs[None, :] < ss[:, None] + SEL_BLOCK) & (cs[None, :] + CMP_BLOCK > ss[:, None]) & (np.arange(ncp) < nc)[None, :]
    return jnp.asarray(ov.astype(np.float32), dtype=BF16)


def _row(v):
    return v.reshape(1, -1)


def _mixers(p, batch, seq, l, prm, tables, ovl_t):
    cosf, sina, sinb = tables
    wg = jnp.concatenate([_block_diag(prm['lru_gate_a_w'][l]), _block_diag(prm['lru_gate_x_w'][l])], axis=1).astype(BF16)
    bg = jnp.concatenate([prm['lru_gate_a_b'][l], prm['lru_gate_x_b'][l]]).reshape(1, -1)
    ya = _lru(p, batch, seq, prm['lru_conv_w'][l], _row(prm['lru_conv_b'][l]), wg, bg,
              _row(prm['lru_lambda'][l]), _row(prm['lru_out_norm'][l]), tt=min(512, seq))
    qc, qr, ks, vs, kw, vw = _nsa_prep(p, seq, cosf, sina, sinb, _row(prm['nsa_q_norm'][l]),
                                       _row(prm['nsa_k_sel_norm'][l]), _row(prm['nsa_k_win_norm'][l]),
                                       tt=min(512, seq))
    kc, vc = _nsa_cmp(p, batch, seq, prm['nsa_cmp_pe_k'][l], prm['nsa_cmp_w1_k'][l].astype(BF16),
                      prm['nsa_cmp_w2_k'][l].astype(BF16), prm['nsa_cmp_pe_v'][l],
                      prm['nsa_cmp_w1_v'][l].astype(BF16), prm['nsa_cmp_w2_v'][l].astype(BF16),
                      _row(prm['nsa_k_cmp_norm'][l]))
    gate_b = jnp.pad(prm['nsa_gate_b'][l], (MISC_GATE, LANES - MISC_GATE - NSA_HEADS * 3)).reshape(1, LANES)
    yb = _nsa_attn(p, qc, qr, kc, vc, ks, vs, kw, vw, gate_b, ovl_t, batch, seq, tq=256, tk=min(512, seq // 2), group=2)
    a2 = jnp.zeros((LANES, GLA_QK_WIDTH), F32).at[MISC_GA:MISC_GA + GLA_RANK].set(prm['gla_a_w2'][l]).astype(BF16)
    ab = _row(prm['gla_a_b'][l])
    yc = _gla(p, batch, seq, a2, ab, _row(prm['gla_out_norm'][l]), tt=min(256, seq),
              nb=4 if batch % 4 == 0 else 1)
    return ya, yb, yc


def kernel(x, ffn1_norm, ffn1_w_gate, ffn1_w_up, ffn1_w_down, mix_norm, w_in, lru_conv_w, lru_conv_b, lru_gate_a_w, lru_gate_a_b, lru_gate_x_w, lru_gate_x_b, lru_lambda, lru_out_norm, nsa_q_norm, nsa_k_cmp_norm, nsa_k_sel_norm, nsa_k_win_norm, nsa_cmp_pe_k, nsa_cmp_w1_k, nsa_cmp_w2_k, nsa_cmp_pe_v, nsa_cmp_w1_v, nsa_cmp_w2_v, nsa_gate_b, nsa_out_norm, gla_a_w2, gla_a_b, gla_out_norm, w_out, ffn2_norm, ffn2_w_gate, ffn2_w_up, ffn2_w_down):
    prm = dict(lru_conv_w=lru_conv_w, lru_conv_b=lru_conv_b, lru_gate_a_w=lru_gate_a_w, lru_gate_a_b=lru_gate_a_b,
               lru_gate_x_w=lru_gate_x_w, lru_gate_x_b=lru_gate_x_b, lru_lambda=lru_lambda, lru_out_norm=lru_out_norm,
               nsa_q_norm=nsa_q_norm, nsa_k_cmp_norm=nsa_k_cmp_norm, nsa_k_sel_norm=nsa_k_sel_norm,
               nsa_k_win_norm=nsa_k_win_norm, nsa_cmp_pe_k=nsa_cmp_pe_k, nsa_cmp_w1_k=nsa_cmp_w1_k,
               nsa_cmp_w2_k=nsa_cmp_w2_k, nsa_cmp_pe_v=nsa_cmp_pe_v, nsa_cmp_w1_v=nsa_cmp_w1_v,
               nsa_cmp_w2_v=nsa_cmp_w2_v, nsa_gate_b=nsa_gate_b, gla_a_w2=gla_a_w2, gla_a_b=gla_a_b,
               gla_out_norm=gla_out_norm)
    batch, seq, d = x.shape
    depth = w_in.shape[0]
    m = batch * seq
    tables = _rope_lane_tables(seq)
    ncp = seq // CMP_STRIDE
    ovl_t = _overlap_t(seq, ncp)
    tm = min(512, m)
    tm_ffn = min(1024, m)
    tf = 512
    xf = x.reshape(m, d)
    ffn1 = [w.astype(BF16) for w in (ffn1_w_gate, ffn1_w_up, ffn1_w_down)]
    ffn2 = [w.astype(BF16) for w in (ffn2_w_gate, ffn2_w_up, ffn2_w_down)]
    for l in range(depth):
        xf = _ffn(xf, _row(ffn1_norm[l]), *ffn1, l, tm_ffn, tf)
        p = _mix_in(xf, _row(mix_norm[l]), _pack_w_in(w_in[l]), tm_ffn, N_PACKED // MIX_IN_TILES)
        ya, yb, yc = _mixers(p, batch, seq, l, prm, tables, ovl_t)
        xf = _mix_out(xf, ya, yb, yc, _row(nsa_out_norm[l]), w_out[l].astype(BF16), tm)
        xf = _ffn(xf, _row(ffn2_norm[l]), *ffn2, l, tm_ffn, tf)
    return xf.reshape(batch, seq, d)
```

```python
import functools

import numpy as np
import jax
import jax.numpy as jnp
from jax import lax
from jax.experimental import pallas as pl
from jax.experimental.pallas import tpu as pltpu

F32 = jnp.float32
BF16 = jnp.bfloat16

NORM_EPS = 1e-6
D_MODEL = 2048
D_FF = 5632

LRU_WIDTH = 512
LRU_BLOCKS = 8
LRU_BLOCK_SIZE = LRU_WIDTH // LRU_BLOCKS
CONV_WIDTH = 4
LRU_C = 8.0

HEAD_DIM = 128
NSA_HEADS = 8
NSA_KV_HEADS = 2
NSA_GROUP = NSA_HEADS // NSA_KV_HEADS
NSA_WIDTH = NSA_HEADS * HEAD_DIM
NSA_KV_WIDTH = NSA_KV_HEADS * HEAD_DIM
ROPE_DIM = HEAD_DIM // 4
ROPE_HALF = ROPE_DIM // 2
ROPE_THETA = 500000.0
CMP_BLOCK = 32
CMP_STRIDE = 16
SEL_BLOCK = 64
N_SELECT = 16
WINDOW = 512
MASK_VALUE = 1e30
QK_LOG2_SCALE = float(HEAD_DIM ** -0.5 * np.log2(np.e))
VT_ROWS = HEAD_DIM + 16

GLA_DV = 128
GLA_HEADS = 4
GLA_DK = 64
GLA_WIDTH = GLA_HEADS * GLA_DV
GLA_QK_WIDTH = GLA_HEADS * GLA_DK
GLA_RANK = 16
GLA_TAU = 16.0
GLA_STEP = 16

LANES = 128

OFF_LRU_X = 0
OFF_LRU_Y = OFF_LRU_X + LRU_WIDTH
OFF_Q = OFF_LRU_Y + LRU_WIDTH
OFF_KC = OFF_Q + NSA_WIDTH
OFF_VC = OFF_KC + NSA_KV_WIDTH
OFF_KS = OFF_VC + NSA_KV_WIDTH
OFF_VS = OFF_KS + NSA_KV_WIDTH
OFF_KW = OFF_VS + NSA_KV_WIDTH
OFF_VW = OFF_KW + NSA_KV_WIDTH
OFF_GQ = OFF_VW + NSA_KV_WIDTH
OFF_GK = OFF_GQ + GLA_QK_WIDTH
OFF_GV = OFF_GK + GLA_QK_WIDTH
OFF_GG = OFF_GV + GLA_WIDTH
OFF_MISC = OFF_GG + GLA_WIDTH
MIX_IN_TILES = 3
N_PACKED = -(-(OFF_MISC + LANES) // (2 * LANES * MIX_IN_TILES)) * (2 * LANES * MIX_IN_TILES)
MISC_GATE = 0
MISC_GA = NSA_HEADS * 3

VMEM_LIMIT = 56 * 1024 * 1024
FFN_SUB = 256


def _cparams(sem):
    return pltpu.CompilerParams(dimension_semantics=sem, vmem_limit_bytes=VMEM_LIMIT)


def _sigmoid(x):
    return 0.5 * jnp.tanh(0.5 * x) + 0.5


def _softplus(x):
    return jnp.maximum(x, 0.0) + jnp.log1p(jnp.exp(-jnp.abs(x)))


def _gelu_tanh(x):
    c = np.float32(np.sqrt(2.0 / np.pi))
    return x * (0.5 * (1.0 + jnp.tanh(c * (x + 0.044715 * (x * x * x)))))


def _rms(x, g):
    return x * lax.rsqrt(jnp.mean(x * x, axis=-1, keepdims=True) + NORM_EPS) * g


def _dot(a, b):
    return jnp.dot(a, b, preferred_element_type=F32)


def _dot_nt(a, b):
    return lax.dot_general(a, b, (((1,), (1,)), ((), ())), preferred_element_type=F32)


def _dot_tn(a, b):
    return lax.dot_general(a, b, (((0,), (0,)), ((), ())), preferred_element_type=F32)


def _ffn_kernel(x_ref, g_ref, wg_ref, wu_ref, wd_ref, o_ref, h_ref):
    j = pl.program_id(1)

    @pl.when(j == 0)
    def _():
        x = x_ref[...]
        h_ref[...] = _rms(x, g_ref[...]).astype(BF16)
        o_ref[...] = x

    h = h_ref[...]
    tf = wg_ref.shape[1]
    acc = None
    for c in range(tf // FFN_SUB):
        cs = slice(c * FFN_SUB, (c + 1) * FFN_SUB)
        gate = _dot(h, wg_ref[:, cs])
        up = _dot(h, wu_ref[:, cs])
        act = (gate * _sigmoid(gate)) * (0.5 * up)
        part = _dot(act.astype(BF16), wd_ref[cs, :])
        acc = part if acc is None else acc + part
    o_ref[...] += acc


def _ffn(x, g, wg, wu, wd, layer, tm, tf):
    m, d = x.shape
    f = wg.shape[2]
    return pl.pallas_call(
        _ffn_kernel,
        grid=(m // tm, f // tf),
        in_specs=[
            pl.BlockSpec((tm, d), lambda i, j: (i, 0)),
            pl.BlockSpec((1, d), lambda i, j: (0, 0)),
            pl.BlockSpec((None, d, tf), lambda i, j: (layer, 0, j)),
            pl.BlockSpec((None, d, tf), lambda i, j: (layer, 0, j)),
            pl.BlockSpec((None, tf, d), lambda i, j: (layer, j, 0)),
        ],
        out_specs=pl.BlockSpec((tm, d), lambda i, j: (i, 0)),
        out_shape=jax.ShapeDtypeStruct((m, d), F32),
        scratch_shapes=[pltpu.VMEM((tm, d), BF16)],
        compiler_params=_cparams(("parallel", "arbitrary")),
        name="ffn",
    )(x, g, wg, wu, wd)


def _mix_in_kernel(x_ref, g_ref, w_ref, o_ref, h_ref):
    @pl.when(pl.program_id(1) == 0)
    def _():
        h_ref[...] = _rms(x_ref[...], g_ref[...]).astype(BF16)

    o_ref[...] = _dot(h_ref[...], w_ref[...])


def _mix_in(x, g, w, tm, tn):
    m, d = x.shape
    n = w.shape[1]
    return pl.pallas_call(
        _mix_in_kernel,
        grid=(m // tm, n // tn),
        in_specs=[
            pl.BlockSpec((tm, d), lambda i, j: (i, 0)),
            pl.BlockSpec((1, d), lambda i, j: (0, 0)),
            pl.BlockSpec((d, tn), lambda i, j: (0, j)),
        ],
        out_specs=pl.BlockSpec((tm, tn), lambda i, j: (i, j)),
        out_shape=jax.ShapeDtypeStruct((m, n), F32),
        scratch_shapes=[pltpu.VMEM((tm, d), BF16)],
        compiler_params=_cparams(("parallel", "arbitrary")),
        name="mix_in",
    )(x, g, w)


def _mix_out_kernel(x_ref, ya_ref, yb_ref, yc_ref, gb_ref, wa_ref, wb_ref, wc_ref, o_ref):
    nb = _rms(yb_ref[...], gb_ref[...]).astype(BF16)
    acc = _dot(ya_ref[...], wa_ref[...])
    acc += _dot(nb, wb_ref[...])
    acc += _dot(yc_ref[...], wc_ref[...])
    o_ref[...] = x_ref[...] + acc


def _mix_out(x, ya, yb, yc, gb, w_out, tm):
    m, d = x.shape
    wa, wb, wc = LRU_WIDTH, NSA_WIDTH, GLA_WIDTH
    return pl.pallas_call(
        _mix_out_kernel,
        grid=(m // tm,),
        in_specs=[
            pl.BlockSpec((tm, d), lambda i: (i, 0)),
            pl.BlockSpec((tm, wa), lambda i: (i, 0)),
            pl.BlockSpec((tm, wb), lambda i: (i, 0)),
            pl.BlockSpec((tm, wc), lambda i: (i, 0)),
            pl.BlockSpec((1, wb), lambda i: (0, 0)),
            pl.BlockSpec((wa, d), lambda i: (0, 0)),
            pl.BlockSpec((wb, d), lambda i: (0, 0)),
            pl.BlockSpec((wc, d), lambda i: (0, 0)),
        ],
        out_specs=pl.BlockSpec((tm, d), lambda i: (i, 0)),
        out_shape=jax.ShapeDtypeStruct((m, d), F32),
        compiler_params=_cparams(("parallel",)),
        name="mix_out",
    )(x, ya, yb, yc, gb, w_out[:wa], w_out[wa:wa + wb], w_out[wa + wb:])


def _lru_kernel(u_ref, y_ref, cw_ref, cb_ref, wg_ref, bg_ref, lam_ref, gn_ref, o_ref,
                ext_ref, a_ref, b_ref, hc_ref):
    tt, w = u_ref.shape

    @pl.when(pl.program_id(1) == 0)
    def _():
        ext_ref[0:8, :] = jnp.zeros((8, w), F32)
        hc_ref[...] = jnp.zeros_like(hc_ref)

    u = u_ref[...]
    ext_ref[8:8 + tt, :] = u
    xc = cb_ref[...] + cw_ref[CONV_WIDTH - 1:CONV_WIDTH, :] * u
    for k in range(CONV_WIDTH - 1):
        sh = CONV_WIDTH - 1 - k
        xc = xc + cw_ref[k:k + 1, :] * ext_ref[8 - sh:8 - sh + tt, :]
    ext_ref[0:8, :] = u[tt - 8:tt, :]

    gates = _dot(xc.astype(BF16), wg_ref[...]) + bg_ref[...]
    r = _sigmoid(gates[:, :w])
    i = _sigmoid(gates[:, w:])
    log_a = (-LRU_C) * r * _softplus(-lam_ref[...])
    a = jnp.exp(log_a)
    a_ref[...] = a
    b_ref[...] = jnp.sqrt(-jnp.tanh(log_a) * (a * a + 1.0)) * i * xc

    row = lax.broadcasted_iota(jnp.int32, (8, w), 0)

    def body(gi, hprev):
        off = pl.multiple_of(gi * 8, 8)
        a8 = a_ref[pl.ds(off, 8), :]
        b8 = b_ref[pl.ds(off, 8), :]
        for s in (1, 2, 4):
            keep = row >= s
            b8 = b8 + a8 * jnp.where(keep, pltpu.roll(b8, s, 0), 0.0)
            a8 = a8 * jnp.where(keep, pltpu.roll(a8, s, 0), 1.0)
        h8 = b8 + a8 * hprev
        b_ref[pl.ds(off, 8), :] = h8
        return jnp.broadcast_to(h8[7:8, :], (8, w))

    hc_ref[...] = lax.fori_loop(0, tt // 8, body, hc_ref[...])

    ya = b_ref[...] * _gelu_tanh(y_ref[...])
    o_ref[...] = _rms(ya, gn_ref[...]).astype(o_ref.dtype)


def _lru(p, batch, seq, cw, cb, wg, bg, lam, gn, tt):
    w = LRU_WIDTH
    nt = seq // tt
    return pl.pallas_call(
        _lru_kernel,
        grid=(batch, nt),
        in_specs=[
            pl.BlockSpec((tt, w), lambda b, t: (b * nt + t, OFF_LRU_X // w)),
            pl.BlockSpec((tt, w), lambda b, t: (b * nt + t, OFF_LRU_Y // w)),
            pl.BlockSpec((CONV_WIDTH, w), lambda b, t: (0, 0)),
            pl.BlockSpec((1, w), lambda b, t: (0, 0)),
            pl.BlockSpec((w, 2 * w), lambda b, t: (0, 0)),
            pl.BlockSpec((1, 2 * w), lambda b, t: (0, 0)),
            pl.BlockSpec((1, w), lambda b, t: (0, 0)),
            pl.BlockSpec((1, w), lambda b, t: (0, 0)),
        ],
        out_specs=pl.BlockSpec((tt, w), lambda b, t: (b * nt + t, 0)),
        out_shape=jax.ShapeDtypeStruct((batch * seq, w), BF16),
        scratch_shapes=[
            pltpu.VMEM((tt + 8, w), F32),
            pltpu.VMEM((tt, w), F32),
            pltpu.VMEM((tt, w), F32),
            pltpu.VMEM((8, w), F32),
        ],
        compiler_params=_cparams(("parallel", "arbitrary")),
        name="lru",
    )(p, p, cw, cb, wg, bg, lam, gn)


def _rope(x, cosf, sina, sinb):
    return x * cosf + pltpu.roll(x, LANES - ROPE_HALF, 1) * sina + pltpu.roll(x, ROPE_HALF, 1) * sinb


def _nsa_prep_kernel(q_ref, ks_ref, vs_ref, kw_ref, vw_ref, cos_ref, sina_ref, sinb_ref,
                     qn_ref, ksn_ref, kwn_ref,
                     qc_ref, qr_ref, kso_ref, vso_ref, kwo_ref, vwo_ref):
    cosf, sina, sinb = cos_ref[...], sina_ref[...], sinb_ref[...]
    for h in range(NSA_HEADS):
        sl = slice(h * HEAD_DIM, (h + 1) * HEAD_DIM)
        qh = _rms(q_ref[:, sl], qn_ref[...])
        qc_ref[:, sl] = (qh * QK_LOG2_SCALE).astype(BF16)
        qr_ref[:, sl] = (_rope(qh, cosf, sina, sinb) * QK_LOG2_SCALE).astype(BF16)
    for h in range(NSA_KV_HEADS):
        sl = slice(h * HEAD_DIM, (h + 1) * HEAD_DIM)
        kso_ref[:, sl] = _rope(_rms(ks_ref[:, sl], ksn_ref[...]), cosf, sina, sinb).astype(BF16)
        kwo_ref[:, sl] = _rope(_rms(kw_ref[:, sl], kwn_ref[...]), cosf, sina, sinb).astype(BF16)
        ones = jnp.ones((VT_ROWS - HEAD_DIM, vs_ref.shape[0]), BF16)
        vso_ref[0, h, 0:HEAD_DIM, :] = vs_ref[:, sl].T.astype(BF16)
        vso_ref[0, h, HEAD_DIM:VT_ROWS, :] = ones
        vwo_ref[0, h, 0:HEAD_DIM, :] = vw_ref[:, sl].T.astype(BF16)
        vwo_ref[0, h, HEAD_DIM:VT_ROWS, :] = ones


def _nsa_prep(p, seq, cosf, sina, sinb, qn, ksn, kwn, tt):
    m = p.shape[0]
    nt = seq // tt
    kvw = NSA_KV_WIDTH

    def col(off, width):
        return pl.BlockSpec((tt, width), lambda i: (i, off // width))

    def tab():
        return pl.BlockSpec((tt, LANES), lambda i: (i % nt, 0))

    def vec():
        return pl.BlockSpec((1, HEAD_DIM), lambda i: (0, 0))

    def vt_spec():
        return pl.BlockSpec((1, NSA_KV_HEADS, VT_ROWS, tt), lambda i: (i // nt, 0, 0, i % nt))

    return pl.pallas_call(
        _nsa_prep_kernel,
        grid=(m // tt,),
        in_specs=[col(OFF_Q, NSA_WIDTH), col(OFF_KS, kvw), col(OFF_VS, kvw), col(OFF_KW, kvw), col(OFF_VW, kvw),
                  tab(), tab(), tab(), vec(), vec(), vec()],
        out_specs=[pl.BlockSpec((tt, NSA_WIDTH), lambda i: (i, 0)),
                   pl.BlockSpec((tt, NSA_WIDTH), lambda i: (i, 0)),
                   pl.BlockSpec((tt, kvw), lambda i: (i, 0)),
                   vt_spec(),
                   pl.BlockSpec((tt, kvw), lambda i: (i, 0)),
                   vt_spec()],
        out_shape=[jax.ShapeDtypeStruct((m, NSA_WIDTH), BF16),
                   jax.ShapeDtypeStruct((m, NSA_WIDTH), BF16),
                   jax.ShapeDtypeStruct((m, kvw), BF16),
                   jax.ShapeDtypeStruct((m // seq, NSA_KV_HEADS, VT_ROWS, seq), BF16),
                   jax.ShapeDtypeStruct((m, kvw), BF16),
                   jax.ShapeDtypeStruct((m // seq, NSA_KV_HEADS, VT_ROWS, seq), BF16)],
        compiler_params=_cparams(("parallel",)),
        name="nsa_prep",
    )(p, p, p, p, p, cosf, sina, sinb, qn, ksn, kwn)


def _nsa_cmp_kernel(k_ref, v_ref, pek_ref, w1k_ref, w2k_ref, pev_ref, w1v_ref, w2v_ref, kn_ref,
                    kc_ref, vc_ref):
    ng = k_ref.shape[0] // CMP_STRIDE
    rowid = lax.broadcasted_iota(jnp.int32, (ng, HEAD_DIM), 0)

    def compress(x_ref, pe_ref, w1_ref, w2_ref):
        first = jnp.zeros((ng, HEAD_DIM), F32)
        second = jnp.zeros((ng, HEAD_DIM), F32)
        for j in range(CMP_STRIDE):
            xj = x_ref[pl.ds(j, ng, stride=CMP_STRIDE), :]
            lo = (xj + pe_ref[j:j + 1, :]).astype(BF16)
            hi = (xj + pe_ref[CMP_STRIDE + j:CMP_STRIDE + j + 1, :]).astype(BF16)
            first += _dot(lo, w1_ref[j * HEAD_DIM:(j + 1) * HEAD_DIM, :])
            second += _dot(hi, w1_ref[(CMP_STRIDE + j) * HEAD_DIM:(CMP_STRIDE + j + 1) * HEAD_DIM, :])
        pre = first + pltpu.roll(second, ng - 1, 0)
        return _dot(_gelu_tanh(pre).astype(BF16), w2_ref[...])

    kc = _rms(compress(k_ref, pek_ref, w1k_ref, w2k_ref), kn_ref[...])
    vc = compress(v_ref, pev_ref, w1v_ref, w2v_ref)
    valid = rowid < ng - 1
    kc_ref[0, 0] = jnp.where(valid, kc, 0.0).astype(BF16)
    vc_ref[0, 0] = jnp.where(valid, vc, 0.0).T.astype(BF16)


def _nsa_cmp(p, batch, seq, pek, w1k, w2k, pev, w1v, w2v, kn):
    ng = seq // CMP_STRIDE
    hd = HEAD_DIM
    kvh = NSA_KV_HEADS

    def full(shape):
        return pl.BlockSpec(shape, lambda b, h: (0,) * len(shape))

    return pl.pallas_call(
        _nsa_cmp_kernel,
        grid=(batch, kvh),
        in_specs=[
            pl.BlockSpec((seq, hd), lambda b, h: (b, OFF_KC // hd + h)),
            pl.BlockSpec((seq, hd), lambda b, h: (b, OFF_VC // hd + h)),
            full((CMP_BLOCK, hd)), full((CMP_BLOCK * hd, hd)), full((hd, hd)),
            full((CMP_BLOCK, hd)), full((CMP_BLOCK * hd, hd)), full((hd, hd)),
            full((1, hd)),
        ],
        out_specs=[pl.BlockSpec((1, 1, ng, hd), lambda b, h: (b, h, 0, 0)),
                   pl.BlockSpec((1, 1, hd, ng), lambda b, h: (b, h, 0, 0))],
        out_shape=[jax.ShapeDtypeStruct((batch, kvh, ng, hd), BF16),
                   jax.ShapeDtypeStruct((batch, kvh, hd, ng), BF16)],
        compiler_params=_cparams(("parallel", "parallel")),
        name="nsa_cmp",
    )(p, p, pek, w1k, w2k, pev, w1v, w2v, kn)


def _nsa_attn_kernel(qc_ref, qr_ref, kc_ref, vct_ref, ks_ref, vst_ref, kw_ref, vwt_ref, gl_ref, gb_ref, ovl_ref,
                     o_ref, selb_ref, rk_ref, gt_ref, acc_ref, z_ref, p_ref, zc_ref, pcb_ref, zw_ref, pw_ref, ob_ref, *, tq, tk, group, n_sel):
    seq = ks_ref.shape[0]
    ncp = kc_ref.shape[2]
    ns = seq // SEL_BLOCK
    grp = NSA_GROUP
    hd = HEAD_DIM
    gq = grp * tq
    nblk = tk // SEL_BLOCK
    kv = pl.program_id(1)
    s0 = pl.program_id(2) * tq
    wspan = WINDOW + tq

    qc = jnp.concatenate([qc_ref[:, g * hd:(g + 1) * hd] for g in range(grp)], axis=0)
    qr = jnp.concatenate([qr_ref[:, g * hd:(g + 1) * hd] for g in range(grp)], axis=0)
    t_row = s0 + lax.broadcasted_iota(jnp.int32, (1, tq), 1)
    t_all = s0 + lax.broadcasted_iota(jnp.int32, (1, gq), 1) % tq

    def heads(x):
        return jnp.concatenate([x] * grp, axis=1)

    srow = lax.broadcasted_iota(jnp.int32, (SEL_BLOCK, 1), 0)
    chunk = SEL_BLOCK

    def biased_max(z, nrows, bias_fn):
        mx = jnp.full((8, gq), -MASK_VALUE, F32)
        for c in range(nrows // chunk):
            rs = slice(c * chunk, (c + 1) * chunk)
            zb = z[rs, :]
            if bias_fn is not None:
                zb = zb + heads(bias_fn(c))
                z[rs, :] = zb
            mx = jnp.maximum(mx, jnp.max(zb.reshape(chunk // 8, 8, gq), axis=0))
        return jnp.max(mx, axis=0, keepdims=True)

    start = pl.multiple_of(jnp.maximum(s0 - WINDOW, 0), tq)
    zc_ref[...] = _dot_nt(kc_ref[0, 0], qc)
    zw_ref[...] = _dot_nt(kw_ref[pl.ds(start, wspan), :], qr)

    def scores(kt):
        return _dot_nt(ks_ref[pl.ds(pl.multiple_of(kt * tk, tk), tk), :], qr)

    z_ref[0] = scores(0)

    def cmp_bias(c):
        cend = (c * chunk + srow) * CMP_STRIDE + (CMP_BLOCK - 1)
        return jnp.where(cend <= t_row, 0.0, -MASK_VALUE)

    def win_bias(c):
        kpos = start + c * chunk + srow
        return jnp.where((kpos <= t_row) & (kpos > t_row - WINDOW), 0.0, -MASK_VALUE)

    mc = biased_max(zc_ref, ncp, cmp_bias)
    mw = biased_max(zw_ref, wspan, win_bias)
    den8 = jnp.zeros((8, gq), F32)
    for c in range(ncp // chunk):
        rs = slice(c * chunk, (c + 1) * chunk)
        e = jnp.exp2(zc_ref[rs, :] - mc)
        zc_ref[rs, :] = e
        pcb_ref[rs, :] = e.astype(BF16)
        den8 = den8 + jnp.sum(e.reshape(chunk // 8, 8, gq), axis=0)
    for c in range(wspan // chunk):
        rs = slice(c * chunk, (c + 1) * chunk)
        pw_ref[rs, :] = jnp.exp2(zw_ref[rs, :] - mw).astype(BF16)
    inv_c = jnp.where(t_all >= CMP_BLOCK - 1, 1.0 / jnp.sum(den8, axis=0, keepdims=True), 0.0)
    ob_ref[0] = _dot(vct_ref[0, 0], pcb_ref[...]) * inv_c
    ow = _dot(vwt_ref[0, 0, :, pl.ds(start, wspan)], pw_ref[...])
    ob_ref[1] = ow[0:hd, :] * (1.0 / ow[hd:hd + 1, :])

    pcs = []
    for c in range(ncp // chunk):
        pn = zc_ref[c * chunk:(c + 1) * chunk, :] * inv_c
        pcs.append(sum(pn[:, g * tq:(g + 1) * tq] for g in range(1, grp)) + pn[:, 0:tq])
    pcs = jnp.concatenate(pcs, axis=0)
    pcs_hi = pcs.astype(BF16)
    pcs_lo = (pcs - pcs_hi.astype(F32)).astype(BF16)
    imp = _dot(ovl_ref[...], pcs_hi) + _dot(ovl_ref[...], pcs_lo)
    jrow = lax.broadcasted_iota(jnp.int32, (ns, tq), 0)
    cur = (s0 + lax.broadcasted_iota(jnp.int32, (ns, tq), 1)) // SEL_BLOCK
    forced = (jrow == 0) | (jrow == cur) | (jrow == cur - 1)
    imp = jnp.where(forced, MASK_VALUE, jnp.where(jrow <= cur, imp, -MASK_VALUE))
    ngrp = ns // 8
    imp_g = [imp[8 * r:8 * r + 8, :] for r in range(ngrp)]
    sub = lax.broadcasted_iota(jnp.int32, (8, tq), 0)
    rk_ref[...] = jnp.zeros_like(rk_ref)
    for ib in range(ngrp):
        @pl.when(ib * 8 * SEL_BLOCK < s0 + tq)
        def _():
            add = [jnp.zeros((8, tq), F32) for _ in range(ngrp)]
            for i in range(8 * ib, 8 * ib + 8):
                other = jnp.broadcast_to(imp[i:i + 1, :], (8, tq))
                for r in range(ngrp):
                    if r > ib:
                        ahead = jnp.where(other >= imp_g[r], 1.0, 0.0)
                    elif r < ib:
                        ahead = jnp.where(other > imp_g[r], 1.0, 0.0)
                    else:
                        ahead = jnp.where(sub > i - 8 * r, jnp.where(other >= imp_g[r], 1.0, 0.0),
                                          jnp.where(other > imp_g[r], 1.0, 0.0))
                    add[r] = add[r] + ahead
            for r in range(ngrp):
                rk_ref[8 * r:8 * r + 8, :] += add[r]
    selb_ref[...] = jnp.where(rk_ref[...] < n_sel, 0.0, -MASK_VALUE)

    acc_ref[...] = jnp.zeros_like(acc_ref)

    n_kt = (s0 + tq + tk - 1) // tk
    n_groups = (n_kt + group - 1) // group

    def tile_step(kt, m, slot):
        z_ref[1 - slot] = scores(jnp.minimum(kt + 1, group * n_groups - 1))
        koff = pl.multiple_of(kt * tk, tk)
        zs = z_ref.at[slot]

        def sel_bias(j):
            return (jnp.broadcast_to(selb_ref[pl.ds(kt * nblk + j, 1), :], (chunk, tq))
                    + jnp.where(koff + j * chunk + srow <= t_row, 0.0, -MASK_VALUE))

        m_new = jnp.maximum(m, biased_max(zs, tk, sel_bias))
        for j in range(nblk):
            rs = slice(j * chunk, (j + 1) * chunk)
            p_ref[rs, :] = jnp.exp2(zs[rs, :] - m_new).astype(BF16)
        acc_ref[...] = acc_ref[...] * jnp.exp2(m - m_new) + _dot(vst_ref[0, 0, :, pl.ds(koff, tk)], p_ref[...])
        return m_new

    def group_body(gi, m):
        for u in range(group):
            m = tile_step(group * gi + u, m, u % 2)
        return m

    lax.fori_loop(0, n_groups, group_body, jnp.full((1, gq), -MASK_VALUE, F32))
    o_sel = acc_ref[0:hd, :] * (1.0 / acc_ref[hd:hd + 1, :])

    gt_ref[...] = _sigmoid(gl_ref[...] + gb_ref[...]).T
    for g in range(grp):
        base = MISC_GATE + (kv * grp + g) * 3
        cs = slice(g * tq, (g + 1) * tq)
        og = (gt_ref[pl.ds(base, 1), :] * ob_ref[0, :, cs] + gt_ref[pl.ds(base + 1, 1), :] * o_sel[:, cs]
              + gt_ref[pl.ds(base + 2, 1), :] * ob_ref[1, :, cs])
        o_ref[:, g * hd:(g + 1) * hd] = og.T


def _nsa_attn(p, qc, qr, kc, vct, ks, vst, kw, vwt, gate_b, ovl, batch, seq, tq, tk, group):
    hd = HEAD_DIM
    gw = NSA_GROUP * hd
    nq = seq // tq
    ns = seq // SEL_BLOCK
    ncp = kc.shape[2]
    n_sel = min(N_SELECT, ns)
    assert seq % (group * tk) == 0 and group % 2 == 0 and tk % SEL_BLOCK == 0
    assert seq >= WINDOW + tq and WINDOW % tq == 0

    def qspec():
        return pl.BlockSpec((tq, gw), lambda b, h, i: (b * nq + i, h))

    def kspec(width):
        return pl.BlockSpec((seq, width), lambda b, h, i: (b, h))

    def vtspec():
        return pl.BlockSpec((1, 1, VT_ROWS, seq), lambda b, h, i: (b, h, 0, 0))

    return pl.pallas_call(
        functools.partial(_nsa_attn_kernel, tq=tq, tk=tk, group=group, n_sel=n_sel),
        grid=(batch, NSA_KV_HEADS, nq),
        in_specs=[qspec(), qspec(),
                  pl.BlockSpec((1, 1, ncp, hd), lambda b, h, i: (b, h, 0, 0)),
                  pl.BlockSpec((1, 1, hd, ncp), lambda b, h, i: (b, h, 0, 0)),
                  kspec(hd), vtspec(), kspec(hd), vtspec(),
                  pl.BlockSpec((tq, LANES), lambda b, h, i: (b * nq + i, OFF_MISC // LANES)),
                  pl.BlockSpec((1, LANES), lambda b, h, i: (0, 0)),
                  pl.BlockSpec((ns, ncp), lambda b, h, i: (0, 0))],
        out_specs=pl.BlockSpec((tq, gw), lambda b, h, i: (b * nq + i, h)),
        out_shape=jax.ShapeDtypeStruct((batch * seq, NSA_WIDTH), F32),
        scratch_shapes=[pltpu.VMEM((ns, tq), F32), pltpu.VMEM((ns, tq), F32), pltpu.VMEM((LANES, tq), F32),
                        pltpu.VMEM((VT_ROWS, NSA_GROUP * tq), F32),
                        pltpu.VMEM((2, tk, NSA_GROUP * tq), F32), pltpu.VMEM((tk, NSA_GROUP * tq), BF16),
                        pltpu.VMEM((ncp, NSA_GROUP * tq), F32), pltpu.VMEM((ncp, NSA_GROUP * tq), BF16),
                        pltpu.VMEM((WINDOW + tq, NSA_GROUP * tq), F32),
                        pltpu.VMEM((WINDOW + tq, NSA_GROUP * tq), BF16),
                        pltpu.VMEM((2, hd, NSA_GROUP * tq), F32)],
        compiler_params=_cparams(("parallel", "parallel", "arbitrary")),
        name="nsa_attn",
    )(qc, qr, kc, vct, ks, vst, kw, vwt, p, gate_b, ovl)


def _gla_kernel(q_ref, k_ref, v_ref, g_ref, misc_ref, a2_ref, ab_ref, gn_ref, ex_ref, o_ref,
                st_ref, qe_ref, kd_ref, vb_ref, ds_ref, oi_ref, od_ref):
    nb, tt, wq = q_ref.shape
    wv = v_ref.shape[2]
    step = GLA_STEP
    ng = tt // step
    nh = GLA_HEADS
    hl = LANES
    per = hl // GLA_DK

    @pl.when(pl.program_id(1) == 0)
    def _():
        st_ref[...] = jnp.zeros_like(st_ref)

    pos = lax.broadcasted_iota(jnp.int32, (tt, wq), 0) % step
    pos3 = pos.reshape(ng, step, wq)
    lane = lax.broadcasted_iota(jnp.int32, (tt, hl), 1)

    def prepare(s_i, carry):
        x = _dot(misc_ref[s_i].astype(BF16), a2_ref[...]) + ab_ref[...]
        b = (-_softplus(-x)) * np.float32(np.log2(np.e) / GLA_TAU)
        s = 1
        while s < step:
            b = b + jnp.where(pos >= s, pltpu.roll(b, s, 0), 0.0)
            s *= 2

        q3 = (q_ref[s_i] * np.float32(GLA_DK ** -0.5)).reshape(ng, step, wq)
        k3 = k_ref[s_i].reshape(ng, step, wq)
        v3 = v_ref[s_i].reshape(ng, step, wv)
        b3 = b.reshape(ng, step, wq)
        b_last = b3[:, step - 1:step, :]
        qe_ref[s_i] = (q3 * jnp.exp2(b3)).reshape(tt, wq).astype(BF16)
        kd = (k3 * jnp.exp2(b_last - b3)).reshape(tt, wq)
        for h in range(nh):
            grp_sl = slice((h // per) * hl, (h // per + 1) * hl)
            mine = (lane >= (h % per) * GLA_DK) & (lane < (h % per + 1) * GLA_DK)
            kd_ref[s_i * nh + h] = jnp.where(mine, kd[:, grp_sl], 0.0).astype(BF16)
        vb_ref[s_i] = v_ref[s_i].astype(BF16)
        ds_ref[s_i] = jnp.exp2(b_last)

        odiag = jnp.zeros((tt, wv), F32)
        for j in range(step):
            dec = jnp.exp2(jnp.where(pos3 >= j, b3 - b3[:, j:j + 1, :], -MASK_VALUE))
            term = (q3 * k3[:, j:j + 1, :] * dec).reshape(tt, wq).astype(BF16)
            vj = jnp.broadcast_to(v3[:, j:j + 1, :], (ng, step, wv)).reshape(tt, wv)
            odiag = odiag + _dot(term, ex_ref[...]) * vj
        od_ref[s_i] = odiag
        return carry

    lax.fori_loop(0, nb, prepare, 0)

    def body(gi, carry):
        r0 = pl.multiple_of(gi * step, step)
        for s_i in range(nb):
            dsg = ds_ref[s_i, gi]
            for h in range(nh):
                sl = slice(h * hl, (h + 1) * hl)
                grp_sl = slice((h // per) * hl, (h // per + 1) * hl)
                st = st_ref[s_i * nh + h]
                oi_ref[s_i, pl.ds(r0, step), sl] = _dot_nt(qe_ref[s_i, pl.ds(r0, step), grp_sl], st.astype(BF16))
                upd = _dot_tn(vb_ref[s_i, pl.ds(r0, step), sl], kd_ref[s_i * nh + h, pl.ds(r0, step), :])
                st_ref[s_i * nh + h] = st * dsg[:, grp_sl] + upd
        return carry

    lax.fori_loop(0, ng, body, 0)

    def finish(s_i, carry):
        o = oi_ref[s_i] + od_ref[s_i]
        gate = g_ref[s_i]
        gate = gate * _sigmoid(gate)
        for h in range(nh):
            sl = slice(h * hl, (h + 1) * hl)
            o_ref[s_i, :, sl] = (_rms(o[:, sl], gn_ref[...]) * gate[:, sl]).astype(o_ref.dtype)
        return carry

    lax.fori_loop(0, nb, finish, 0)


def _gla(p, batch, seq, a2, ab, gn, tt, nb):
    wq = GLA_QK_WIDTH
    wv = GLA_WIDTH
    nt = seq // tt
    ng = tt // GLA_STEP
    expand = jnp.asarray(np.kron(np.eye(GLA_HEADS), np.ones((GLA_DK, GLA_DV))), dtype=BF16)
    p3 = p.reshape(batch, seq, p.shape[1])

    def col(off, width):
        return pl.BlockSpec((nb, tt, width), lambda b, t: (b, t, off // width))

    out = pl.pallas_call(
        _gla_kernel,
        grid=(batch // nb, nt),
        in_specs=[col(OFF_GQ, wq), col(OFF_GK, wq), col(OFF_GV, wv), col(OFF_GG, wv), col(OFF_MISC, LANES),
                  pl.BlockSpec((LANES, wq), lambda b, t: (0, 0)),
                  pl.BlockSpec((1, wq), lambda b, t: (0, 0)),
                  pl.BlockSpec((1, GLA_DV), lambda b, t: (0, 0)),
                  pl.BlockSpec((wq, wv), lambda b, t: (0, 0))],
        out_specs=pl.BlockSpec((nb, tt, wv), lambda b, t: (b, t, 0)),
        out_shape=jax.ShapeDtypeStruct((batch, seq, wv), BF16),
        scratch_shapes=[
            pltpu.VMEM((nb * GLA_HEADS, GLA_DV, LANES), F32),
            pltpu.VMEM((nb, tt, wq), BF16),
            pltpu.VMEM((nb * GLA_HEADS, tt, LANES), BF16),
            pltpu.VMEM((nb, tt, wv), BF16),
            pltpu.VMEM((nb, ng, 1, wq), F32),
            pltpu.VMEM((nb, tt, wv), F32),
            pltpu.VMEM((nb, tt, wv), F32),
        ],
        compiler_params=_cparams(("parallel", "arbitrary")),
        name="gla",
    )(p3, p3, p3, p3, p3, a2, ab, gn, expand)
    return out.reshape(batch * seq, wv)


def _pack_w_in(w_in):
    d = w_in.shape[0]
    o_gate = OFF_VW + NSA_KV_WIDTH
    o_gq = o_gate + NSA_HEADS * 3
    o_ga = o_gq + 2 * GLA_QK_WIDTH + 2 * GLA_WIDTH
    tail = jnp.zeros((d, N_PACKED - OFF_MISC - NSA_HEADS * 3 - GLA_RANK), w_in.dtype)
    packed = jnp.concatenate([w_in[:, :o_gate], w_in[:, o_gq:o_ga],
                              w_in[:, o_gate:o_gq], w_in[:, o_ga:o_ga + GLA_RANK], tail], axis=1)
    assert packed.shape[1] == N_PACKED
    return packed.astype(BF16)


def _block_diag(w):
    nb, bi, bo = w.shape
    eye = jnp.eye(nb, dtype=w.dtype)
    return (eye[:, None, :, None] * w[:, :, None, :]).reshape(nb * bi, nb * bo)


def _rope_lane_tables(seq):
    inv = 1.0 / (ROPE_THETA ** (jnp.arange(0, ROPE_DIM, 2, dtype=F32) / ROPE_DIM))
    ang = jnp.arange(seq, dtype=F32)[:, None] * inv[None, :]
    cos, sin = jnp.cos(ang), jnp.sin(ang)
    rest = LANES - ROPE_DIM
    zeros_h = jnp.zeros((seq, ROPE_HALF), F32)
    cosf = jnp.concatenate([cos, cos, jnp.ones((seq, rest), F32)], axis=1)
    sina = jnp.concatenate([-sin, zeros_h, jnp.zeros((seq, rest), F32)], axis=1)
    sinb = jnp.concatenate([zeros_h, sin, jnp.zeros((seq, rest), F32)], axis=1)
    return cosf, sina, sinb


def _overlap_t(seq, ncp):
    ns = seq // SEL_BLOCK
    nc = (seq - CMP_BLOCK) // CMP_STRIDE + 1
    cs = np.arange(ncp) * CMP_STRIDE
    ss = np.arange(ns) * SEL_BLOCK
    ov = (cs[None, :] < ss[:, None] + SEL_BLOCK) & (cs[None, :] + CMP_BLOCK > ss[:, None]) & (np.arange(ncp) < nc)[None, :]
    return jnp.asarray(ov.astype(np.float32), dtype=BF16)


def _row(v):
    return v.reshape(1, -1)


def _mixers(p, batch, seq, l, prm, tables, ovl_t):
    cosf, sina, sinb = tables
    wg = jnp.concatenate([_block_diag(prm['lru_gate_a_w'][l]), _block_diag(prm['lru_gate_x_w'][l])], axis=1).astype(BF16)
    bg = jnp.concatenate([prm['lru_gate_a_b'][l], prm['lru_gate_x_b'][l]]).reshape(1, -1)
    ya = _lru(p, batch, seq, prm['lru_conv_w'][l], _row(prm['lru_conv_b'][l]), wg, bg,
              _row(prm['lru_lambda'][l]), _row(prm['lru_out_norm'][l]), tt=min(512, seq))
    qc, qr, ks, vs, kw, vw = _nsa_prep(p, seq, cosf, sina, sinb, _row(prm['nsa_q_norm'][l]),
                                       _row(prm['nsa_k_sel_norm'][l]), _row(prm['nsa_k_win_norm'][l]),
                                       tt=min(512, seq))
    kc, vc = _nsa_cmp(p, batch, seq, prm['nsa_cmp_pe_k'][l], prm['nsa_cmp_w1_k'][l].astype(BF16),
                      prm['nsa_cmp_w2_k'][l].astype(BF16), prm['nsa_cmp_pe_v'][l],
                      prm['nsa_cmp_w1_v'][l].astype(BF16), prm['nsa_cmp_w2_v'][l].astype(BF16),
                      _row(prm['nsa_k_cmp_norm'][l]))
    gate_b = jnp.pad(prm['nsa_gate_b'][l], (MISC_GATE, LANES - MISC_GATE - NSA_HEADS * 3)).reshape(1, LANES)
    yb = _nsa_attn(p, qc, qr, kc, vc, ks, vs, kw, vw, gate_b, ovl_t, batch, seq, tq=256, tk=min(512, seq // 2), group=2)
    a2 = jnp.zeros((LANES, GLA_QK_WIDTH), F32).at[MISC_GA:MISC_GA + GLA_RANK].set(prm['gla_a_w2'][l]).astype(BF16)
    ab = _row(prm['gla_a_b'][l])
    yc = _gla(p, batch, seq, a2, ab, _row(prm['gla_out_norm'][l]), tt=min(256, seq),
              nb=next(n for n in (8, 4, 2, 1) if batch % n == 0))
    return ya, yb, yc


def kernel(x, ffn1_norm, ffn1_w_gate, ffn1_w_up, ffn1_w_down, mix_norm, w_in, lru_conv_w, lru_conv_b, lru_gate_a_w, lru_gate_a_b, lru_gate_x_w, lru_gate_x_b, lru_lambda, lru_out_norm, nsa_q_norm, nsa_k_cmp_norm, nsa_k_sel_norm, nsa_k_win_norm, nsa_cmp_pe_k, nsa_cmp_w1_k, nsa_cmp_w2_k, nsa_cmp_pe_v, nsa_cmp_w1_v, nsa_cmp_w2_v, nsa_gate_b, nsa_out_norm, gla_a_w2, gla_a_b, gla_out_norm, w_out, ffn2_norm, ffn2_w_gate, ffn2_w_up, ffn2_w_down):
    prm = dict(lru_conv_w=lru_conv_w, lru_conv_b=lru_conv_b, lru_gate_a_w=lru_gate_a_w, lru_gate_a_b=lru_gate_a_b,
               lru_gate_x_w=lru_gate_x_w, lru_gate_x_b=lru_gate_x_b, lru_lambda=lru_lambda, lru_out_norm=lru_out_norm,
               nsa_q_norm=nsa_q_norm, nsa_k_cmp_norm=nsa_k_cmp_norm, nsa_k_sel_norm=nsa_k_sel_norm,
               nsa_k_win_norm=nsa_k_win_norm, nsa_cmp_pe_k=nsa_cmp_pe_k, nsa_cmp_w1_k=nsa_cmp_w1_k,
               nsa_cmp_w2_k=nsa_cmp_w2_k, nsa_cmp_pe_v=nsa_cmp_pe_v, nsa_cmp_w1_v=nsa_cmp_w1_v,
               nsa_cmp_w2_v=nsa_cmp_w2_v, nsa_gate_b=nsa_gate_b, gla_a_w2=gla_a_w2, gla_a_b=gla_a_b,
               gla_out_norm=gla_out_norm)
    batch, seq, d = x.shape
    depth = w_in.shape[0]
    m = batch * seq
    tables = _rope_lane_tables(seq)
    ncp = seq // CMP_STRIDE
    ovl_t = _overlap_t(seq, ncp)
    tm = min(512, m)
    tm_ffn = min(1024, m)
    tf = 512
    xf = x.reshape(m, d)
    ffn1 = [w.astype(BF16) for w in (ffn1_w_gate, ffn1_w_up, ffn1_w_down)]
    ffn2 = [w.astype(BF16) for w in (ffn2_w_gate, ffn2_w_up, ffn2_w_down)]
    for l in range(depth):
        xf = _ffn(xf, _row(ffn1_norm[l]), *ffn1, l, tm_ffn, tf)
        p = _mix_in(xf, _row(mix_norm[l]), _pack_w_in(w_in[l]), tm_ffn, N_PACKED // MIX_IN_TILES)
        ya, yb, yc = _mixers(p, batch, seq, l, prm, tables, ovl_t)
        xf = _mix_out(xf, ya, yb, yc, _row(nsa_out_norm[l]), w_out[l].astype(BF16), tm)
        xf = _ffn(xf, _row(ffn2_norm[l]), *ffn2, l, tm_ffn, tf)
    return xf.reshape(batch, seq, d)
```

```python
import functools

import numpy as np
import jax
import jax.numpy as jnp
from jax import lax
from jax.experimental import pallas as pl
from jax.experimental.pallas import tpu as pltpu

F32 = jnp.float32
BF16 = jnp.bfloat16

NORM_EPS = 1e-6
D_MODEL = 2048
D_FF = 5632

LRU_WIDTH = 512
LRU_BLOCKS = 8
LRU_BLOCK_SIZE = LRU_WIDTH // LRU_BLOCKS
CONV_WIDTH = 4
LRU_C = 8.0

HEAD_DIM = 128
NSA_HEADS = 8
NSA_KV_HEADS = 2
NSA_GROUP = NSA_HEADS // NSA_KV_HEADS
NSA_WIDTH = NSA_HEADS * HEAD_DIM
NSA_KV_WIDTH = NSA_KV_HEADS * HEAD_DIM
ROPE_DIM = HEAD_DIM // 4
ROPE_HALF = ROPE_DIM // 2
ROPE_THETA = 500000.0
CMP_BLOCK = 32
CMP_STRIDE = 16
SEL_BLOCK = 64
N_SELECT = 16
WINDOW = 512
MASK_VALUE = 1e30
QK_LOG2_SCALE = float(HEAD_DIM ** -0.5 * np.log2(np.e))
VT_ROWS = HEAD_DIM + 16

GLA_DV = 128
GLA_HEADS = 4
GLA_DK = 64
GLA_WIDTH = GLA_HEADS * GLA_DV
GLA_QK_WIDTH = GLA_HEADS * GLA_DK
GLA_RANK = 16
GLA_TAU = 16.0
GLA_STEP = 16

LANES = 128

OFF_LRU_X = 0
OFF_LRU_Y = OFF_LRU_X + LRU_WIDTH
OFF_Q = OFF_LRU_Y + LRU_WIDTH
OFF_KC = OFF_Q + NSA_WIDTH
OFF_VC = OFF_KC + NSA_KV_WIDTH
OFF_KS = OFF_VC + NSA_KV_WIDTH
OFF_VS = OFF_KS + NSA_KV_WIDTH
OFF_KW = OFF_VS + NSA_KV_WIDTH
OFF_VW = OFF_KW + NSA_KV_WIDTH
OFF_GQ = OFF_VW + NSA_KV_WIDTH
OFF_GK = OFF_GQ + GLA_QK_WIDTH
OFF_GV = OFF_GK + GLA_QK_WIDTH
OFF_GG = OFF_GV + GLA_WIDTH
OFF_MISC = OFF_GG + GLA_WIDTH
MIX_IN_TILES = 3
N_PACKED = -(-(OFF_MISC + LANES) // (2 * LANES * MIX_IN_TILES)) * (2 * LANES * MIX_IN_TILES)
MISC_GATE = 0
MISC_GA = NSA_HEADS * 3

VMEM_LIMIT = 56 * 1024 * 1024
FFN_SUB = 256


def _cparams(sem):
    return pltpu.CompilerParams(dimension_semantics=sem, vmem_limit_bytes=VMEM_LIMIT)


def _sigmoid(x):
    return 0.5 * jnp.tanh(0.5 * x) + 0.5


def _softplus(x):
    return jnp.maximum(x, 0.0) + jnp.log1p(jnp.exp(-jnp.abs(x)))


def _gelu_tanh(x):
    c = np.float32(np.sqrt(2.0 / np.pi))
    return x * (0.5 * (1.0 + jnp.tanh(c * (x + 0.044715 * (x * x * x)))))


def _rms(x, g):
    return x * lax.rsqrt(jnp.mean(x * x, axis=-1, keepdims=True) + NORM_EPS) * g


def _dot(a, b):
    return jnp.dot(a, b, preferred_element_type=F32)


def _dot_nt(a, b):
    return lax.dot_general(a, b, (((1,), (1,)), ((), ())), preferred_element_type=F32)


def _dot_tn(a, b):
    return lax.dot_general(a, b, (((0,), (0,)), ((), ())), preferred_element_type=F32)


def _ffn_kernel(x_ref, g_ref, wg_ref, wu_ref, wd_ref, o_ref, h_ref):
    j = pl.program_id(1)
    tf = wg_ref.shape[1]

    def half_swiglu():
        h = h_ref[...]
        acc = None
        for c in range(tf // FFN_SUB):
            cs = slice(c * FFN_SUB, (c + 1) * FFN_SUB)
            gate = _dot(h, wg_ref[:, cs])
            up = _dot(h, wu_ref[:, cs])
            act = (gate * _sigmoid(gate)) * (0.5 * up)
            part = _dot(act.astype(BF16), wd_ref[cs, :])
            acc = part if acc is None else acc + part
        return acc

    @pl.when(j == 0)
    def _():
        x = x_ref[...]
        h_ref[...] = _rms(x, g_ref[...]).astype(BF16)
        o_ref[...] = x + half_swiglu()

    @pl.when(j > 0)
    def _():
        o_ref[...] += half_swiglu()


def _ffn(x, g, wg, wu, wd, layer, tm, tf):
    m, d = x.shape
    f = wg.shape[2]
    return pl.pallas_call(
        _ffn_kernel,
        grid=(m // tm, f // tf),
        in_specs=[
            pl.BlockSpec((tm, d), lambda i, j: (i, 0)),
            pl.BlockSpec((1, d), lambda i, j: (0, 0)),
            pl.BlockSpec((None, d, tf), lambda i, j: (layer, 0, j)),
            pl.BlockSpec((None, d, tf), lambda i, j: (layer, 0, j)),
            pl.BlockSpec((None, tf, d), lambda i, j: (layer, j, 0)),
        ],
        out_specs=pl.BlockSpec((tm, d), lambda i, j: (i, 0)),
        out_shape=jax.ShapeDtypeStruct((m, d), F32),
        scratch_shapes=[pltpu.VMEM((tm, d), BF16)],
        compiler_params=_cparams(("parallel", "arbitrary")),
        name="ffn",
    )(x, g, wg, wu, wd)


def _mix_in_kernel(x_ref, g_ref, w_ref, o_ref, h_ref):
    @pl.when(pl.program_id(1) == 0)
    def _():
        h_ref[...] = _rms(x_ref[...], g_ref[...]).astype(BF16)

    o_ref[...] = _dot(h_ref[...], w_ref[...])


def _mix_in(x, g, w, layer, tm, tn):
    m, d = x.shape
    n = w.shape[2]
    return pl.pallas_call(
        _mix_in_kernel,
        grid=(m // tm, n // tn),
        in_specs=[
            pl.BlockSpec((tm, d), lambda i, j: (i, 0)),
            pl.BlockSpec((1, d), lambda i, j: (0, 0)),
            pl.BlockSpec((None, d, tn), lambda i, j: (layer, 0, j)),
        ],
        out_specs=pl.BlockSpec((tm, tn), lambda i, j: (i, j)),
        out_shape=jax.ShapeDtypeStruct((m, n), F32),
        scratch_shapes=[pltpu.VMEM((tm, d), BF16)],
        compiler_params=_cparams(("parallel", "arbitrary")),
        name="mix_in",
    )(x, g, w)


def _mix_out_kernel(x_ref, ya_ref, yb_ref, yc_ref, gb_ref, w_ref, o_ref):
    wa, wb = ya_ref.shape[1], yb_ref.shape[1]
    nb = _rms(yb_ref[...], gb_ref[...]).astype(BF16)
    acc = _dot(ya_ref[...], w_ref[0:wa, :])
    acc += _dot(nb, w_ref[wa:wa + wb, :])
    acc += _dot(yc_ref[...], w_ref[wa + wb:, :])
    o_ref[...] = x_ref[...] + acc


def _mix_out(x, ya, yb, yc, gb, w_out, layer, tm):
    m, d = x.shape
    wa, wb, wc = LRU_WIDTH, NSA_WIDTH, GLA_WIDTH
    return pl.pallas_call(
        _mix_out_kernel,
        grid=(m // tm,),
        in_specs=[
            pl.BlockSpec((tm, d), lambda i: (i, 0)),
            pl.BlockSpec((tm, wa), lambda i: (i, 0)),
            pl.BlockSpec((tm, wb), lambda i: (i, 0)),
            pl.BlockSpec((tm, wc), lambda i: (i, 0)),
            pl.BlockSpec((1, wb), lambda i: (0, 0)),
            pl.BlockSpec((None, wa + wb + wc, d), lambda i: (layer, 0, 0)),
        ],
        out_specs=pl.BlockSpec((tm, d), lambda i: (i, 0)),
        out_shape=jax.ShapeDtypeStruct((m, d), F32),
        compiler_params=_cparams(("parallel",)),
        name="mix_out",
    )(x, ya, yb, yc, gb, w_out)


def _lru_kernel(u_ref, y_ref, cw_ref, cb_ref, wg_ref, bg_ref, lam_ref, gn_ref, o_ref,
                ext_ref, a_ref, b_ref, hc_ref):
    tt, w = u_ref.shape

    @pl.when(pl.program_id(1) == 0)
    def _():
        ext_ref[0:8, :] = jnp.zeros((8, w), F32)
        hc_ref[...] = jnp.zeros_like(hc_ref)

    u = u_ref[...]
    ext_ref[8:8 + tt, :] = u
    xc = cb_ref[...] + cw_ref[CONV_WIDTH - 1:CONV_WIDTH, :] * u
    for k in range(CONV_WIDTH - 1):
        sh = CONV_WIDTH - 1 - k
        xc = xc + cw_ref[k:k + 1, :] * ext_ref[8 - sh:8 - sh + tt, :]
    ext_ref[0:8, :] = u[tt - 8:tt, :]

    gates = _dot(xc.astype(BF16), wg_ref[...]) + bg_ref[...]
    r = _sigmoid(gates[:, :w])
    i = _sigmoid(gates[:, w:])
    log_a = (-LRU_C) * r * _softplus(-lam_ref[...])
    a = jnp.exp(log_a)
    a_ref[...] = a
    b_ref[...] = jnp.sqrt(-jnp.tanh(log_a) * (a * a + 1.0)) * i * xc

    row = lax.broadcasted_iota(jnp.int32, (8, w), 0)

    def body(gi, hprev):
        off = pl.multiple_of(gi * 8, 8)
        a8 = a_ref[pl.ds(off, 8), :]
        b8 = b_ref[pl.ds(off, 8), :]
        for s in (1, 2, 4):
            keep = row >= s
            b8 = b8 + a8 * jnp.where(keep, pltpu.roll(b8, s, 0), 0.0)
            a8 = a8 * jnp.where(keep, pltpu.roll(a8, s, 0), 1.0)
        h8 = b8 + a8 * hprev
        b_ref[pl.ds(off, 8), :] = h8
        return jnp.broadcast_to(h8[7:8, :], (8, w))

    hc_ref[...] = lax.fori_loop(0, tt // 8, body, hc_ref[...])

    ya = b_ref[...] * _gelu_tanh(y_ref[...])
    o_ref[...] = _rms(ya, gn_ref[...]).astype(o_ref.dtype)


def _lru(p, batch, seq, cw, cb, wg, bg, lam, gn, tt):
    w = LRU_WIDTH
    nt = seq // tt
    return pl.pallas_call(
        _lru_kernel,
        grid=(batch, nt),
        in_specs=[
            pl.BlockSpec((tt, w), lambda b, t: (b * nt + t, OFF_LRU_X // w)),
            pl.BlockSpec((tt, w), lambda b, t: (b * nt + t, OFF_LRU_Y // w)),
            pl.BlockSpec((CONV_WIDTH, w), lambda b, t: (0, 0)),
            pl.BlockSpec((1, w), lambda b, t: (0, 0)),
            pl.BlockSpec((w, 2 * w), lambda b, t: (0, 0)),
            pl.BlockSpec((1, 2 * w), lambda b, t: (0, 0)),
            pl.BlockSpec((1, w), lambda b, t: (0, 0)),
            pl.BlockSpec((1, w), lambda b, t: (0, 0)),
        ],
        out_specs=pl.BlockSpec((tt, w), lambda b, t: (b * nt + t, 0)),
        out_shape=jax.ShapeDtypeStruct((batch * seq, w), BF16),
        scratch_shapes=[
            pltpu.VMEM((tt + 8, w), F32),
            pltpu.VMEM((tt, w), F32),
            pltpu.VMEM((tt, w), F32),
            pltpu.VMEM((8, w), F32),
        ],
        compiler_params=_cparams(("parallel", "arbitrary")),
        name="lru",
    )(p, p, cw, cb, wg, bg, lam, gn)


def _rope(x, cosf, sina, sinb):
    return x * cosf + pltpu.roll(x, LANES - ROPE_HALF, 1) * sina + pltpu.roll(x, ROPE_HALF, 1) * sinb


def _nsa_prep_kernel(q_ref, ks_ref, vs_ref, kw_ref, vw_ref, cos_ref, sina_ref, sinb_ref,
                     qn_ref, ksn_ref, kwn_ref,
                     qc_ref, qr_ref, kso_ref, vso_ref, kwo_ref, vwo_ref):
    cosf, sina, sinb = cos_ref[...], sina_ref[...], sinb_ref[...]
    for h in range(NSA_HEADS):
        sl = slice(h * HEAD_DIM, (h + 1) * HEAD_DIM)
        qh = _rms(q_ref[:, sl], qn_ref[...])
        qc_ref[:, sl] = (qh * QK_LOG2_SCALE).astype(BF16)
        qr_ref[:, sl] = (_rope(qh, cosf, sina, sinb) * QK_LOG2_SCALE).astype(BF16)
    for h in range(NSA_KV_HEADS):
        sl = slice(h * HEAD_DIM, (h + 1) * HEAD_DIM)
        kso_ref[:, sl] = _rope(_rms(ks_ref[:, sl], ksn_ref[...]), cosf, sina, sinb).astype(BF16)
        kwo_ref[:, sl] = _rope(_rms(kw_ref[:, sl], kwn_ref[...]), cosf, sina, sinb).astype(BF16)
        ones = jnp.ones((VT_ROWS - HEAD_DIM, vs_ref.shape[0]), BF16)
        vso_ref[0, h, 0:HEAD_DIM, :] = vs_ref[:, sl].T.astype(BF16)
        vso_ref[0, h, HEAD_DIM:VT_ROWS, :] = ones
        vwo_ref[0, h, 0:HEAD_DIM, :] = vw_ref[:, sl].T.astype(BF16)
        vwo_ref[0, h, HEAD_DIM:VT_ROWS, :] = ones


def _nsa_prep(p, seq, cosf, sina, sinb, qn, ksn, kwn, tt):
    m = p.shape[0]
    nt = seq // tt
    kvw = NSA_KV_WIDTH

    def col(off, width):
        return pl.BlockSpec((tt, width), lambda i: (i, off // width))

    def tab():
        return pl.BlockSpec((tt, LANES), lambda i: (i % nt, 0))

    def vec():
        return pl.BlockSpec((1, HEAD_DIM), lambda i: (0, 0))

    def vt_spec():
        return pl.BlockSpec((1, NSA_KV_HEADS, VT_ROWS, tt), lambda i: (i // nt, 0, 0, i % nt))

    return pl.pallas_call(
        _nsa_prep_kernel,
        grid=(m // tt,),
        in_specs=[col(OFF_Q, NSA_WIDTH), col(OFF_KS, kvw), col(OFF_VS, kvw), col(OFF_KW, kvw), col(OFF_VW, kvw),
                  tab(), tab(), tab(), vec(), vec(), vec()],
        out_specs=[pl.BlockSpec((tt, NSA_WIDTH), lambda i: (i, 0)),
                   pl.BlockSpec((tt, NSA_WIDTH), lambda i: (i, 0)),
                   pl.BlockSpec((tt, kvw), lambda i: (i, 0)),
                   vt_spec(),
                   pl.BlockSpec((tt, kvw), lambda i: (i, 0)),
                   vt_spec()],
        out_shape=[jax.ShapeDtypeStruct((m, NSA_WIDTH), BF16),
                   jax.ShapeDtypeStruct((m, NSA_WIDTH), BF16),
                   jax.ShapeDtypeStruct((m, kvw), BF16),
                   jax.ShapeDtypeStruct((m // seq, NSA_KV_HEADS, VT_ROWS, seq), BF16),
                   jax.ShapeDtypeStruct((m, kvw), BF16),
                   jax.ShapeDtypeStruct((m // seq, NSA_KV_HEADS, VT_ROWS, seq), BF16)],
        compiler_params=_cparams(("parallel",)),
        name="nsa_prep",
    )(p, p, p, p, p, cosf, sina, sinb, qn, ksn, kwn)


def _nsa_cmp_kernel(k_ref, v_ref, pek_ref, w1k_ref, w2k_ref, pev_ref, w1v_ref, w2v_ref, kn_ref,
                    kc_ref, vc_ref):
    ng = k_ref.shape[0] // CMP_STRIDE
    rowid = lax.broadcasted_iota(jnp.int32, (ng, HEAD_DIM), 0)

    def compress(x_ref, pe_ref, w1_ref, w2_ref):
        first = jnp.zeros((ng, HEAD_DIM), F32)
        second = jnp.zeros((ng, HEAD_DIM), F32)
        for j in range(CMP_STRIDE):
            xj = x_ref[pl.ds(j, ng, stride=CMP_STRIDE), :]
            lo = (xj + pe_ref[j:j + 1, :]).astype(BF16)
            hi = (xj + pe_ref[CMP_STRIDE + j:CMP_STRIDE + j + 1, :]).astype(BF16)
            first += _dot(lo, w1_ref[j * HEAD_DIM:(j + 1) * HEAD_DIM, :])
            second += _dot(hi, w1_ref[(CMP_STRIDE + j) * HEAD_DIM:(CMP_STRIDE + j + 1) * HEAD_DIM, :])
        pre = first + pltpu.roll(second, ng - 1, 0)
        return _dot(_gelu_tanh(pre).astype(BF16), w2_ref[...])

    kc = _rms(compress(k_ref, pek_ref, w1k_ref, w2k_ref), kn_ref[...])
    vc = compress(v_ref, pev_ref, w1v_ref, w2v_ref)
    valid = rowid < ng - 1
    kc_ref[0, 0] = jnp.where(valid, kc, 0.0).astype(BF16)
    vc_ref[0, 0] = jnp.where(valid, vc, 0.0).T.astype(BF16)


def _nsa_cmp(p, batch, seq, pek, w1k, w2k, pev, w1v, w2v, kn):
    ng = seq // CMP_STRIDE
    hd = HEAD_DIM
    kvh = NSA_KV_HEADS

    def full(shape):
        return pl.BlockSpec(shape, lambda b, h: (0,) * len(shape))

    return pl.pallas_call(
        _nsa_cmp_kernel,
        grid=(batch, kvh),
        in_specs=[
            pl.BlockSpec((seq, hd), lambda b, h: (b, OFF_KC // hd + h)),
            pl.BlockSpec((seq, hd), lambda b, h: (b, OFF_VC // hd + h)),
            full((CMP_BLOCK, hd)), full((CMP_BLOCK * hd, hd)), full((hd, hd)),
            full((CMP_BLOCK, hd)), full((CMP_BLOCK * hd, hd)), full((hd, hd)),
            full((1, hd)),
        ],
        out_specs=[pl.BlockSpec((1, 1, ng, hd), lambda b, h: (b, h, 0, 0)),
                   pl.BlockSpec((1, 1, hd, ng), lambda b, h: (b, h, 0, 0))],
        out_shape=[jax.ShapeDtypeStruct((batch, kvh, ng, hd), BF16),
                   jax.ShapeDtypeStruct((batch, kvh, hd, ng), BF16)],
        compiler_params=_cparams(("parallel", "parallel")),
        name="nsa_cmp",
    )(p, p, pek, w1k, w2k, pev, w1v, w2v, kn)


def _nsa_attn_kernel(qc_ref, qr_ref, kc_ref, vct_ref, ks_ref, vst_ref, kw_ref, vwt_ref, gl_ref, gb_ref, ovl_ref,
                     o_ref, selb_ref, rk_ref, gt_ref, acc_ref, z_ref, p_ref, zc_ref, pcb_ref, zw_ref, pw_ref, ob_ref, *, tq, tk, group, n_sel):
    seq = ks_ref.shape[0]
    ncp = kc_ref.shape[2]
    ns = seq // SEL_BLOCK
    grp = NSA_GROUP
    hd = HEAD_DIM
    gq = grp * tq
    nblk = tk // SEL_BLOCK
    kv = pl.program_id(1)
    s0 = pl.program_id(2) * tq
    wspan = WINDOW + tq

    qc = jnp.concatenate([qc_ref[:, g * hd:(g + 1) * hd] for g in range(grp)], axis=0)
    qr = jnp.concatenate([qr_ref[:, g * hd:(g + 1) * hd] for g in range(grp)], axis=0)
    t_row = s0 + lax.broadcasted_iota(jnp.int32, (1, tq), 1)
    t_all = s0 + lax.broadcasted_iota(jnp.int32, (1, gq), 1) % tq

    def heads(x):
        return jnp.concatenate([x] * grp, axis=1)

    srow = lax.broadcasted_iota(jnp.int32, (SEL_BLOCK, 1), 0)
    chunk = SEL_BLOCK

    def biased_max(z, nrows, bias_fn):
        mx = jnp.full((8, gq), -MASK_VALUE, F32)
        for c in range(nrows // chunk):
            rs = slice(c * chunk, (c + 1) * chunk)
            zb = z[rs, :]
            if bias_fn is not None:
                zb = zb + heads(bias_fn(c))
                z[rs, :] = zb
            mx = jnp.maximum(mx, jnp.max(zb.reshape(chunk // 8, 8, gq), axis=0))
        return jnp.max(mx, axis=0, keepdims=True)

    start = pl.multiple_of(jnp.maximum(s0 - WINDOW, 0), tq)
    zc_ref[...] = _dot_nt(kc_ref[0, 0], qc)
    zw_ref[...] = _dot_nt(kw_ref[pl.ds(start, wspan), :], qr)

    def scores(kt):
        return _dot_nt(ks_ref[pl.ds(pl.multiple_of(kt * tk, tk), tk), :], qr)

    z_ref[0] = scores(0)

    def cmp_bias(c):
        cend = (c * chunk + srow) * CMP_STRIDE + (CMP_BLOCK - 1)
        return jnp.where(cend <= t_row, 0.0, -MASK_VALUE)

    def win_bias(c):
        kpos = start + c * chunk + srow
        return jnp.where((kpos <= t_row) & (kpos > t_row - WINDOW), 0.0, -MASK_VALUE)

    mc = biased_max(zc_ref, ncp, cmp_bias)
    mw = biased_max(zw_ref, wspan, win_bias)
    den8 = jnp.zeros((8, gq), F32)
    for c in range(ncp // chunk):
        rs = slice(c * chunk, (c + 1) * chunk)
        e = jnp.exp2(zc_ref[rs, :] - mc)
        zc_ref[rs, :] = e
        pcb_ref[rs, :] = e.astype(BF16)
        den8 = den8 + jnp.sum(e.reshape(chunk // 8, 8, gq), axis=0)
    for c in range(wspan // chunk):
        rs = slice(c * chunk, (c + 1) * chunk)
        pw_ref[rs, :] = jnp.exp2(zw_ref[rs, :] - mw).astype(BF16)
    inv_c = jnp.where(t_all >= CMP_BLOCK - 1, 1.0 / jnp.sum(den8, axis=0, keepdims=True), 0.0)
    ob_ref[0] = _dot(vct_ref[0, 0], pcb_ref[...]) * inv_c
    ow = _dot(vwt_ref[0, 0, :, pl.ds(start, wspan)], pw_ref[...])
    ob_ref[1] = ow[0:hd, :] * (1.0 / ow[hd:hd + 1, :])

    pcs = []
    for c in range(ncp // chunk):
        pn = zc_ref[c * chunk:(c + 1) * chunk, :] * inv_c
        pcs.append(sum(pn[:, g * tq:(g + 1) * tq] for g in range(1, grp)) + pn[:, 0:tq])
    pcs = jnp.concatenate(pcs, axis=0)
    pcs_hi = pcs.astype(BF16)
    pcs_lo = (pcs - pcs_hi.astype(F32)).astype(BF16)
    imp = _dot(ovl_ref[...], pcs_hi) + _dot(ovl_ref[...], pcs_lo)
    jrow = lax.broadcasted_iota(jnp.int32, (ns, tq), 0)
    cur = (s0 + lax.broadcasted_iota(jnp.int32, (ns, tq), 1)) // SEL_BLOCK
    forced = (jrow == 0) | (jrow == cur) | (jrow == cur - 1)
    imp = jnp.where(forced, MASK_VALUE, jnp.where(jrow <= cur, imp, -MASK_VALUE))
    ngrp = ns // 8
    imp_g = [imp[8 * r:8 * r + 8, :] for r in range(ngrp)]
    sub = lax.broadcasted_iota(jnp.int32, (8, tq), 0)
    rk_ref[...] = jnp.zeros_like(rk_ref)
    for ib in range(ngrp):
        @pl.when(ib * 8 * SEL_BLOCK < s0 + tq)
        def _():
            add = [jnp.zeros((8, tq), F32) for _ in range(ngrp)]
            for i in range(8 * ib, 8 * ib + 8):
                other = jnp.broadcast_to(imp[i:i + 1, :], (8, tq))
                for r in range(ngrp):
                    if r > ib:
                        ahead = jnp.where(other >= imp_g[r], 1.0, 0.0)
                    elif r < ib:
                        ahead = jnp.where(other > imp_g[r], 1.0, 0.0)
                    else:
                        ahead = jnp.where(sub > i - 8 * r, jnp.where(other >= imp_g[r], 1.0, 0.0),
                                          jnp.where(other > imp_g[r], 1.0, 0.0))
                    add[r] = add[r] + ahead
            for r in range(ngrp):
                rk_ref[8 * r:8 * r + 8, :] += add[r]
    selb_ref[...] = jnp.where(rk_ref[...] < n_sel, 0.0, -MASK_VALUE)

    acc_ref[...] = jnp.zeros_like(acc_ref)

    n_kt = (s0 + tq + tk - 1) // tk
    n_groups = (n_kt + group - 1) // group

    def tile_step(kt, m, slot):
        z_ref[1 - slot] = scores(jnp.minimum(kt + 1, group * n_groups - 1))
        koff = pl.multiple_of(kt * tk, tk)
        zs = z_ref.at[slot]

        def sel_bias(j):
            return (jnp.broadcast_to(selb_ref[pl.ds(kt * nblk + j, 1), :], (chunk, tq))
                    + jnp.where(koff + j * chunk + srow <= t_row, 0.0, -MASK_VALUE))

        m_new = jnp.maximum(m, biased_max(zs, tk, sel_bias))
        for j in range(nblk):
            rs = slice(j * chunk, (j + 1) * chunk)
            p_ref[rs, :] = jnp.exp2(zs[rs, :] - m_new).astype(BF16)
        acc_ref[...] = acc_ref[...] * jnp.exp2(m - m_new) + _dot(vst_ref[0, 0, :, pl.ds(koff, tk)], p_ref[...])
        return m_new

    def group_body(gi, m):
        for u in range(group):
            m = tile_step(group * gi + u, m, u % 2)
        return m

    lax.fori_loop(0, n_groups, group_body, jnp.full((1, gq), -MASK_VALUE, F32))
    o_sel = acc_ref[0:hd, :] * (1.0 / acc_ref[hd:hd + 1, :])

    gt_ref[...] = _sigmoid(gl_ref[...] + gb_ref[...]).T
    for g in range(grp):
        base = MISC_GATE + (kv * grp + g) * 3
        cs = slice(g * tq, (g + 1) * tq)
        og = (gt_ref[pl.ds(base, 1), :] * ob_ref[0, :, cs] + gt_ref[pl.ds(base + 1, 1), :] * o_sel[:, cs]
              + gt_ref[pl.ds(base + 2, 1), :] * ob_ref[1, :, cs])
        o_ref[:, g * hd:(g + 1) * hd] = og.T


def _nsa_attn(p, qc, qr, kc, vct, ks, vst, kw, vwt, gate_b, ovl, batch, seq, tq, tk, group):
    hd = HEAD_DIM
    gw = NSA_GROUP * hd
    nq = seq // tq
    ns = seq // SEL_BLOCK
    ncp = kc.shape[2]
    n_sel = min(N_SELECT, ns)
    assert seq % (group * tk) == 0 and group % 2 == 0 and tk % SEL_BLOCK == 0
    assert seq >= WINDOW + tq and WINDOW % tq == 0

    def qspec():
        return pl.BlockSpec((tq, gw), lambda b, h, i: (b * nq + i, h))

    def kspec(width):
        return pl.BlockSpec((seq, width), lambda b, h, i: (b, h))

    def vtspec():
        return pl.BlockSpec((1, 1, VT_ROWS, seq), lambda b, h, i: (b, h, 0, 0))

    return pl.pallas_call(
        functools.partial(_nsa_attn_kernel, tq=tq, tk=tk, group=group, n_sel=n_sel),
        grid=(batch, NSA_KV_HEADS, nq),
        in_specs=[qspec(), qspec(),
                  pl.BlockSpec((1, 1, ncp, hd), lambda b, h, i: (b, h, 0, 0)),
                  pl.BlockSpec((1, 1, hd, ncp), lambda b, h, i: (b, h, 0, 0)),
                  kspec(hd), vtspec(), kspec(hd), vtspec(),
                  pl.BlockSpec((tq, LANES), lambda b, h, i: (b * nq + i, OFF_MISC // LANES)),
                  pl.BlockSpec((1, LANES), lambda b, h, i: (0, 0)),
                  pl.BlockSpec((ns, ncp), lambda b, h, i: (0, 0))],
        out_specs=pl.BlockSpec((tq, gw), lambda b, h, i: (b * nq + i, h)),
        out_shape=jax.ShapeDtypeStruct((batch * seq, NSA_WIDTH), F32),
        scratch_shapes=[pltpu.VMEM((ns, tq), F32), pltpu.VMEM((ns, tq), F32), pltpu.VMEM((LANES, tq), F32),
                        pltpu.VMEM((VT_ROWS, NSA_GROUP * tq), F32),
                        pltpu.VMEM((2, tk, NSA_GROUP * tq), F32), pltpu.VMEM((tk, NSA_GROUP * tq), BF16),
                        pltpu.VMEM((ncp, NSA_GROUP * tq), F32), pltpu.VMEM((ncp, NSA_GROUP * tq), BF16),
                        pltpu.VMEM((WINDOW + tq, NSA_GROUP * tq), F32),
                        pltpu.VMEM((WINDOW + tq, NSA_GROUP * tq), BF16),
                        pltpu.VMEM((2, hd, NSA_GROUP * tq), F32)],
        compiler_params=_cparams(("parallel", "parallel", "arbitrary")),
        name="nsa_attn",
    )(qc, qr, kc, vct, ks, vst, kw, vwt, p, gate_b, ovl)


def _gla_kernel(q_ref, k_ref, v_ref, g_ref, misc_ref, a2_ref, ab_ref, gn_ref, ex_ref, o_ref,
                st_ref, qe_ref, kd_ref, vb_ref, ds_ref, oi_ref, od_ref):
    nb, tt, wq = q_ref.shape
    wv = v_ref.shape[2]
    step = GLA_STEP
    ng = tt // step
    nh = GLA_HEADS
    hl = LANES
    per = hl // GLA_DK

    @pl.when(pl.program_id(1) == 0)
    def _():
        st_ref[...] = jnp.zeros_like(st_ref)

    pos = lax.broadcasted_iota(jnp.int32, (tt, wq), 0) % step
    pos3 = pos.reshape(ng, step, wq)
    lane = lax.broadcasted_iota(jnp.int32, (tt, hl), 1)

    def prepare(s_i, carry):
        x = _dot(misc_ref[s_i].astype(BF16), a2_ref[...]) + ab_ref[...]
        b = (-_softplus(-x)) * np.float32(np.log2(np.e) / GLA_TAU)
        s = 1
        while s < step:
            b = b + jnp.where(pos >= s, pltpu.roll(b, s, 0), 0.0)
            s *= 2

        q3 = (q_ref[s_i] * np.float32(GLA_DK ** -0.5)).reshape(ng, step, wq)
        k3 = k_ref[s_i].reshape(ng, step, wq)
        v3 = v_ref[s_i].reshape(ng, step, wv)
        b3 = b.reshape(ng, step, wq)
        b_last = b3[:, step - 1:step, :]
        qe_ref[s_i] = (q3 * jnp.exp2(b3)).reshape(tt, wq).astype(BF16)
        kd = (k3 * jnp.exp2(b_last - b3)).reshape(tt, wq)
        for h in range(nh):
            grp_sl = slice((h // per) * hl, (h // per + 1) * hl)
            mine = (lane >= (h % per) * GLA_DK) & (lane < (h % per + 1) * GLA_DK)
            kd_ref[s_i * nh + h] = jnp.where(mine, kd[:, grp_sl], 0.0).astype(BF16)
        vb_ref[s_i] = v_ref[s_i].astype(BF16)
        ds_ref[s_i] = jnp.exp2(b_last)

        half = step // 2
        odiag = jnp.zeros((ng, step, wv), F32)
        odiag_hi = jnp.zeros((ng, half, wv), F32)
        q3h, b3h = q3[:, half:, :], b3[:, half:, :]
        pos3h = half + lax.broadcasted_iota(jnp.int32, (ng, half, wq), 1)
        for j in range(step):
            if j < half:
                dec = jnp.exp2(jnp.where(pos3 >= j, b3 - b3[:, j:j + 1, :], -MASK_VALUE))
                term = (q3 * k3[:, j:j + 1, :] * dec).reshape(tt, wq).astype(BF16)
                odiag = odiag + _dot(term, ex_ref[...]).reshape(ng, step, wv) * v3[:, j:j + 1, :]
            else:
                dec = jnp.exp2(jnp.where(pos3h >= j, b3h - b3[:, j:j + 1, :], -MASK_VALUE))
                term = (q3h * k3[:, j:j + 1, :] * dec).reshape(tt // 2, wq).astype(BF16)
                odiag_hi = odiag_hi + _dot(term, ex_ref[...]).reshape(ng, half, wv) * v3[:, j:j + 1, :]
        odiag = jnp.concatenate([odiag[:, :half, :], odiag[:, half:, :] + odiag_hi], axis=1)
        od_ref[s_i] = odiag.reshape(tt, wv)
        return carry

    lax.fori_loop(0, nb, prepare, 0)

    def body(gi, carry):
        r0 = pl.multiple_of(gi * step, step)
        for s_i in range(nb):
            dsg = ds_ref[s_i, gi]
            for h in range(nh):
                sl = slice(h * hl, (h + 1) * hl)
                grp_sl = slice((h // per) * hl, (h // per + 1) * hl)
                st = st_ref[s_i * nh + h]
                oi_ref[s_i, pl.ds(r0, step), sl] = _dot_nt(qe_ref[s_i, pl.ds(r0, step), grp_sl], st.astype(BF16))
                upd = _dot_tn(vb_ref[s_i, pl.ds(r0, step), sl], kd_ref[s_i * nh + h, pl.ds(r0, step), :])
                st_ref[s_i * nh + h] = st * dsg[:, grp_sl] + upd
        return carry

    lax.fori_loop(0, ng, body, 0)

    def finish(s_i, carry):
        o = oi_ref[s_i] + od_ref[s_i]
        gate = g_ref[s_i]
        gate = gate * _sigmoid(gate)
        for h in range(nh):
            sl = slice(h * hl, (h + 1) * hl)
            o_ref[s_i, :, sl] = (_rms(o[:, sl], gn_ref[...]) * gate[:, sl]).astype(o_ref.dtype)
        return carry

    lax.fori_loop(0, nb, finish, 0)


def _gla(p, batch, seq, a2, ab, gn, tt, nb):
    wq = GLA_QK_WIDTH
    wv = GLA_WIDTH
    nt = seq // tt
    ng = tt // GLA_STEP
    expand = jnp.asarray(np.kron(np.eye(GLA_HEADS), np.ones((GLA_DK, GLA_DV))), dtype=BF16)
    p3 = p.reshape(batch, seq, p.shape[1])

    def col(off, width):
        return pl.BlockSpec((nb, tt, width), lambda b, t: (b, t, off // width))

    out = pl.pallas_call(
        _gla_kernel,
        grid=(batch // nb, nt),
        in_specs=[col(OFF_GQ, wq), col(OFF_GK, wq), col(OFF_GV, wv), col(OFF_GG, wv), col(OFF_MISC, LANES),
                  pl.BlockSpec((LANES, wq), lambda b, t: (0, 0)),
                  pl.BlockSpec((1, wq), lambda b, t: (0, 0)),
                  pl.BlockSpec((1, GLA_DV), lambda b, t: (0, 0)),
                  pl.BlockSpec((wq, wv), lambda b, t: (0, 0))],
        out_specs=pl.BlockSpec((nb, tt, wv), lambda b, t: (b, t, 0)),
        out_shape=jax.ShapeDtypeStruct((batch, seq, wv), BF16),
        scratch_shapes=[
            pltpu.VMEM((nb * GLA_HEADS, GLA_DV, LANES), F32),
            pltpu.VMEM((nb, tt, wq), BF16),
            pltpu.VMEM((nb * GLA_HEADS, tt, LANES), BF16),
            pltpu.VMEM((nb, tt, wv), BF16),
            pltpu.VMEM((nb, ng, 1, wq), F32),
            pltpu.VMEM((nb, tt, wv), F32),
            pltpu.VMEM((nb, tt, wv), F32),
        ],
        compiler_params=_cparams(("parallel", "arbitrary")),
        name="gla",
    )(p3, p3, p3, p3, p3, a2, ab, gn, expand)
    return out.reshape(batch * seq, wv)


def _pack_w_in(w_in):
    o_gate = OFF_VW + NSA_KV_WIDTH
    o_gq = o_gate + NSA_HEADS * 3
    o_ga = o_gq + 2 * GLA_QK_WIDTH + 2 * GLA_WIDTH
    tail = jnp.zeros(w_in.shape[:2] + (N_PACKED - OFF_MISC - NSA_HEADS * 3 - GLA_RANK,), w_in.dtype)
    packed = jnp.concatenate([w_in[..., :o_gate], w_in[..., o_gq:o_ga],
                              w_in[..., o_gate:o_gq], w_in[..., o_ga:o_ga + GLA_RANK], tail], axis=-1)
    assert packed.shape[-1] == N_PACKED
    return packed.astype(BF16)


def _block_diag(w):
    nb, bi, bo = w.shape
    eye = jnp.eye(nb, dtype=w.dtype)
    return (eye[:, None, :, None] * w[:, :, None, :]).reshape(nb * bi, nb * bo)


def _rope_lane_tables(seq):
    inv = 1.0 / (ROPE_THETA ** (jnp.arange(0, ROPE_DIM, 2, dtype=F32) / ROPE_DIM))
    ang = jnp.arange(seq, dtype=F32)[:, None] * inv[None, :]
    cos, sin = jnp.cos(ang), jnp.sin(ang)
    rest = LANES - ROPE_DIM
    zeros_h = jnp.zeros((seq, ROPE_HALF), F32)
    cosf = jnp.concatenate([cos, cos, jnp.ones((seq, rest), F32)], axis=1)
    sina = jnp.concatenate([-sin, zeros_h, jnp.zeros((seq, rest), F32)], axis=1)
    sinb = jnp.concatenate([zeros_h, sin, jnp.zeros((seq, rest), F32)], axis=1)
    return cosf, sina, sinb


def _overlap_t(seq, ncp):
    ns = seq // SEL_BLOCK
    nc = (seq - CMP_BLOCK) // CMP_STRIDE + 1
    cs = np.arange(ncp) * CMP_STRIDE
    ss = np.arange(ns) * SEL_BLOCK
    ov = (cs[None, :] < ss[:, None] + SEL_BLOCK) & (cs[None, :] + CMP_BLOCK > ss[:, None]) & (np.arange(ncp) < nc)[None, :]
    return jnp.asarray(ov.astype(np.float32), dtype=BF16)


def _row(v):
    return v.reshape(1, -1)


def _mixers(p, batch, seq, l, prm, tables, ovl_t):
    cosf, sina, sinb = tables
    wg = jnp.concatenate([_block_diag(prm['lru_gate_a_w'][l]), _block_diag(prm['lru_gate_x_w'][l])], axis=1).astype(BF16)
    bg = jnp.concatenate([prm['lru_gate_a_b'][l], prm['lru_gate_x_b'][l]]).reshape(1, -1)
    ya = _lru(p, batch, seq, prm['lru_conv_w'][l], _row(prm['lru_conv_b'][l]), wg, bg,
              _row(prm['lru_lambda'][l]), _row(prm['lru_out_norm'][l]), tt=min(1024, seq))
    qc, qr, ks, vs, kw, vw = _nsa_prep(p, seq, cosf, sina, sinb, _row(prm['nsa_q_norm'][l]),
                                       _row(prm['nsa_k_sel_norm'][l]), _row(prm['nsa_k_win_norm'][l]),
                                       tt=min(1024, seq))
    kc, vc = _nsa_cmp(p, batch, seq, prm['nsa_cmp_pe_k'][l], prm['nsa_cmp_w1_k'][l].astype(BF16),
                      prm['nsa_cmp_w2_k'][l].astype(BF16), prm['nsa_cmp_pe_v'][l],
                      prm['nsa_cmp_w1_v'][l].astype(BF16), prm['nsa_cmp_w2_v'][l].astype(BF16),
                      _row(prm['nsa_k_cmp_norm'][l]))
    gate_b = jnp.pad(prm['nsa_gate_b'][l], (MISC_GATE, LANES - MISC_GATE - NSA_HEADS * 3)).reshape(1, LANES)
    yb = _nsa_attn(p, qc, qr, kc, vc, ks, vs, kw, vw, gate_b, ovl_t, batch, seq, tq=256, tk=min(512, seq // 2), group=2)
    a2 = jnp.zeros((LANES, GLA_QK_WIDTH), F32).at[MISC_GA:MISC_GA + GLA_RANK].set(prm['gla_a_w2'][l]).astype(BF16)
    ab = _row(prm['gla_a_b'][l])
    yc = _gla(p, batch, seq, a2, ab, _row(prm['gla_out_norm'][l]), tt=min(256, seq),
              nb=next(n for n in (8, 4, 2, 1) if batch % n == 0))
    return ya, yb, yc


def kernel(x, ffn1_norm, ffn1_w_gate, ffn1_w_up, ffn1_w_down, mix_norm, w_in, lru_conv_w, lru_conv_b, lru_gate_a_w, lru_gate_a_b, lru_gate_x_w, lru_gate_x_b, lru_lambda, lru_out_norm, nsa_q_norm, nsa_k_cmp_norm, nsa_k_sel_norm, nsa_k_win_norm, nsa_cmp_pe_k, nsa_cmp_w1_k, nsa_cmp_w2_k, nsa_cmp_pe_v, nsa_cmp_w1_v, nsa_cmp_w2_v, nsa_gate_b, nsa_out_norm, gla_a_w2, gla_a_b, gla_out_norm, w_out, ffn2_norm, ffn2_w_gate, ffn2_w_up, ffn2_w_down):
    prm = dict(lru_conv_w=lru_conv_w, lru_conv_b=lru_conv_b, lru_gate_a_w=lru_gate_a_w, lru_gate_a_b=lru_gate_a_b,
               lru_gate_x_w=lru_gate_x_w, lru_gate_x_b=lru_gate_x_b, lru_lambda=lru_lambda, lru_out_norm=lru_out_norm,
               nsa_q_norm=nsa_q_norm, nsa_k_cmp_norm=nsa_k_cmp_norm, nsa_k_sel_norm=nsa_k_sel_norm,
               nsa_k_win_norm=nsa_k_win_norm, nsa_cmp_pe_k=nsa_cmp_pe_k, nsa_cmp_w1_k=nsa_cmp_w1_k,
               nsa_cmp_w2_k=nsa_cmp_w2_k, nsa_cmp_pe_v=nsa_cmp_pe_v, nsa_cmp_w1_v=nsa_cmp_w1_v,
               nsa_cmp_w2_v=nsa_cmp_w2_v, nsa_gate_b=nsa_gate_b, gla_a_w2=gla_a_w2, gla_a_b=gla_a_b,
               gla_out_norm=gla_out_norm)
    batch, seq, d = x.shape
    depth = w_in.shape[0]
    m = batch * seq
    tables = _rope_lane_tables(seq)
    ncp = seq // CMP_STRIDE
    ovl_t = _overlap_t(seq, ncp)
    tm = min(512, m)
    tm_ffn = min(1024, m)
    tf = 512
    xf = x.reshape(m, d)
    ffn1 = [w.astype(BF16) for w in (ffn1_w_gate, ffn1_w_up, ffn1_w_down)]
    ffn2 = [w.astype(BF16) for w in (ffn2_w_gate, ffn2_w_up, ffn2_w_down)]
    w_in_packed = _pack_w_in(w_in)
    w_out_bf = w_out.astype(BF16)
    for l in range(depth):
        xf = _ffn(xf, _row(ffn1_norm[l]), *ffn1, l, tm_ffn, tf)
        p = _mix_in(xf, _row(mix_norm[l]), w_in_packed, l, tm_ffn, N_PACKED // MIX_IN_TILES)
        ya, yb, yc = _mixers(p, batch, seq, l, prm, tables, ovl_t)
        xf = _mix_out(xf, ya, yb, yc, _row(nsa_out_norm[l]), w_out_bf, l, tm)
        xf = _ffn(xf, _row(ffn2_norm[l]), *ffn2, l, tm_ffn, tf)
    return xf.reshape(batch, seq, d)
```

```python
import functools

import numpy as np
import jax
import jax.numpy as jnp
from jax import lax
from jax.experimental import pallas as pl
from jax.experimental.pallas import tpu as pltpu

F32 = jnp.float32
BF16 = jnp.bfloat16

NORM_EPS = 1e-6
D_MODEL = 2048
D_FF = 5632

LRU_WIDTH = 512
LRU_BLOCKS = 8
LRU_BLOCK_SIZE = LRU_WIDTH // LRU_BLOCKS
CONV_WIDTH = 4
LRU_C = 8.0

HEAD_DIM = 128
NSA_HEADS = 8
NSA_KV_HEADS = 2
NSA_GROUP = NSA_HEADS // NSA_KV_HEADS
NSA_WIDTH = NSA_HEADS * HEAD_DIM
NSA_KV_WIDTH = NSA_KV_HEADS * HEAD_DIM
ROPE_DIM = HEAD_DIM // 4
ROPE_HALF = ROPE_DIM // 2
ROPE_THETA = 500000.0
CMP_BLOCK = 32
CMP_STRIDE = 16
SEL_BLOCK = 64
N_SELECT = 16
WINDOW = 512
MASK_VALUE = 1e30
QK_LOG2_SCALE = float(HEAD_DIM ** -0.5 * np.log2(np.e))
VT_ROWS = HEAD_DIM + 16

GLA_DV = 128
GLA_HEADS = 4
GLA_DK = 64
GLA_WIDTH = GLA_HEADS * GLA_DV
GLA_QK_WIDTH = GLA_HEADS * GLA_DK
GLA_RANK = 16
GLA_TAU = 16.0
GLA_STEP = 16

LANES = 128

OFF_LRU_X = 0
OFF_LRU_Y = OFF_LRU_X + LRU_WIDTH
OFF_Q = OFF_LRU_Y + LRU_WIDTH
OFF_KC = OFF_Q + NSA_WIDTH
OFF_VC = OFF_KC + NSA_KV_WIDTH
OFF_KS = OFF_VC + NSA_KV_WIDTH
OFF_VS = OFF_KS + NSA_KV_WIDTH
OFF_KW = OFF_VS + NSA_KV_WIDTH
OFF_VW = OFF_KW + NSA_KV_WIDTH
OFF_GQ = OFF_VW + NSA_KV_WIDTH
OFF_GK = OFF_GQ + GLA_QK_WIDTH
OFF_GV = OFF_GK + GLA_QK_WIDTH
OFF_GG = OFF_GV + GLA_WIDTH
OFF_MISC = OFF_GG + GLA_WIDTH
MIX_IN_TILES = 3
N_PACKED = -(-(OFF_MISC + LANES) // (2 * LANES * MIX_IN_TILES)) * (2 * LANES * MIX_IN_TILES)
MISC_GATE = 0
MISC_GA = NSA_HEADS * 3

VMEM_LIMIT = 56 * 1024 * 1024
FFN_SUB = 256


def _cparams(sem):
    return pltpu.CompilerParams(dimension_semantics=sem, vmem_limit_bytes=VMEM_LIMIT)


def _sigmoid(x):
    return 0.5 * jnp.tanh(0.5 * x) + 0.5


def _softplus(x):
    return jnp.maximum(x, 0.0) + jnp.log1p(jnp.exp(-jnp.abs(x)))


def _gelu_tanh(x):
    c = np.float32(np.sqrt(2.0 / np.pi))
    return x * (0.5 * (1.0 + jnp.tanh(c * (x + 0.044715 * (x * x * x)))))


def _rms(x, g):
    return x * lax.rsqrt(jnp.mean(x * x, axis=-1, keepdims=True) + NORM_EPS) * g


def _dot(a, b):
    return jnp.dot(a, b, preferred_element_type=F32)


def _dot_nt(a, b):
    return lax.dot_general(a, b, (((1,), (1,)), ((), ())), preferred_element_type=F32)


def _dot_tn(a, b):
    return lax.dot_general(a, b, (((0,), (0,)), ((), ())), preferred_element_type=F32)


def _ffn_kernel(x_ref, g_ref, wg_ref, wu_ref, wd_ref, o_ref, h_ref):
    j = pl.program_id(1)
    tf = wg_ref.shape[1]

    def half_swiglu():
        h = h_ref[...]
        acc = None
        for c in range(tf // FFN_SUB):
            cs = slice(c * FFN_SUB, (c + 1) * FFN_SUB)
            gate = _dot(h, wg_ref[:, cs])
            up = _dot(h, wu_ref[:, cs])
            act = (gate * _sigmoid(gate)) * (0.5 * up)
            part = _dot(act.astype(BF16), wd_ref[cs, :])
            acc = part if acc is None else acc + part
        return acc

    @pl.when(j == 0)
    def _():
        x = x_ref[...]
        h_ref[...] = _rms(x, g_ref[...]).astype(BF16)
        o_ref[...] = x + half_swiglu()

    @pl.when(j > 0)
    def _():
        o_ref[...] += half_swiglu()


def _ffn(x, g, wg, wu, wd, layer, tm, tf):
    m, d = x.shape
    f = wg.shape[2]
    return pl.pallas_call(
        _ffn_kernel,
        grid=(m // tm, f // tf),
        in_specs=[
            pl.BlockSpec((tm, d), lambda i, j: (i, 0)),
            pl.BlockSpec((1, d), lambda i, j: (0, 0)),
            pl.BlockSpec((None, d, tf), lambda i, j: (layer, 0, j)),
            pl.BlockSpec((None, d, tf), lambda i, j: (layer, 0, j)),
            pl.BlockSpec((None, tf, d), lambda i, j: (layer, j, 0)),
        ],
        out_specs=pl.BlockSpec((tm, d), lambda i, j: (i, 0)),
        out_shape=jax.ShapeDtypeStruct((m, d), F32),
        scratch_shapes=[pltpu.VMEM((tm, d), BF16)],
        compiler_params=_cparams(("parallel", "arbitrary")),
        name="ffn",
    )(x, g, wg, wu, wd)


def _mix_in_kernel(x_ref, g_ref, w_ref, o_ref, h_ref):
    @pl.when(pl.program_id(1) == 0)
    def _():
        h_ref[...] = _rms(x_ref[...], g_ref[...]).astype(BF16)
        o_ref[...] = _dot(h_ref[...], w_ref[...])

    @pl.when(pl.program_id(1) > 0)
    def _():
        o_ref[...] = _dot(h_ref[...], w_ref[...])


def _mix_in(x, g, w, layer, tm, tn):
    m, d = x.shape
    n = w.shape[2]
    return pl.pallas_call(
        _mix_in_kernel,
        grid=(m // tm, n // tn),
        in_specs=[
            pl.BlockSpec((tm, d), lambda i, j: (i, 0)),
            pl.BlockSpec((1, d), lambda i, j: (0, 0)),
            pl.BlockSpec((None, d, tn), lambda i, j: (layer, 0, j)),
        ],
        out_specs=pl.BlockSpec((tm, tn), lambda i, j: (i, j)),
        out_shape=jax.ShapeDtypeStruct((m, n), F32),
        scratch_shapes=[pltpu.VMEM((tm, d), BF16)],
        compiler_params=_cparams(("parallel", "arbitrary")),
        name="mix_in",
    )(x, g, w)


def _mix_out_kernel(x_ref, ya_ref, yb_ref, yc_ref, gb_ref, w_ref, o_ref):
    wa, wb = ya_ref.shape[1], yb_ref.shape[1]
    nb = _rms(yb_ref[...], gb_ref[...]).astype(BF16)
    acc = _dot(ya_ref[...], w_ref[0:wa, :])
    acc += _dot(nb, w_ref[wa:wa + wb, :])
    acc += _dot(yc_ref[...], w_ref[wa + wb:, :])
    o_ref[...] = x_ref[...] + acc


def _mix_out(x, ya, yb, yc, gb, w_out, layer, tm):
    m, d = x.shape
    wa, wb, wc = LRU_WIDTH, NSA_WIDTH, GLA_WIDTH
    return pl.pallas_call(
        _mix_out_kernel,
        grid=(m // tm,),
        in_specs=[
            pl.BlockSpec((tm, d), lambda i: (i, 0)),
            pl.BlockSpec((tm, wa), lambda i: (i, 0)),
            pl.BlockSpec((tm, wb), lambda i: (i, 0)),
            pl.BlockSpec((tm, wc), lambda i: (i, 0)),
            pl.BlockSpec((1, wb), lambda i: (0, 0)),
            pl.BlockSpec((None, wa + wb + wc, d), lambda i: (layer, 0, 0)),
        ],
        out_specs=pl.BlockSpec((tm, d), lambda i: (i, 0)),
        out_shape=jax.ShapeDtypeStruct((m, d), F32),
        compiler_params=_cparams(("parallel",)),
        name="mix_out",
    )(x, ya, yb, yc, gb, w_out)


def _lru_kernel(u_ref, y_ref, cw_ref, cb_ref, wg_ref, bg_ref, lam_ref, gn_ref, o_ref,
                ext_ref, a_ref, b_ref, hc_ref):
    tt, w = u_ref.shape

    @pl.when(pl.program_id(1) == 0)
    def _():
        ext_ref[0:8, :] = jnp.zeros((8, w), F32)
        hc_ref[...] = jnp.zeros_like(hc_ref)

    u = u_ref[...]
    ext_ref[8:8 + tt, :] = u
    xc = cb_ref[...] + cw_ref[CONV_WIDTH - 1:CONV_WIDTH, :] * u
    for k in range(CONV_WIDTH - 1):
        sh = CONV_WIDTH - 1 - k
        xc = xc + cw_ref[k:k + 1, :] * ext_ref[8 - sh:8 - sh + tt, :]
    ext_ref[0:8, :] = u[tt - 8:tt, :]

    gates = _dot(xc.astype(BF16), wg_ref[...]) + bg_ref[...]
    r = _sigmoid(gates[:, :w])
    i = _sigmoid(gates[:, w:])
    log_a = (-LRU_C) * r * _softplus(-lam_ref[...])
    a = jnp.exp(log_a)
    a_ref[...] = a
    b_ref[...] = jnp.sqrt(-jnp.tanh(log_a) * (a * a + 1.0)) * i * xc

    row = lax.broadcasted_iota(jnp.int32, (8, w), 0)

    def body(gi, hprev):
        off = pl.multiple_of(gi * 8, 8)
        a8 = a_ref[pl.ds(off, 8), :]
        b8 = b_ref[pl.ds(off, 8), :]
        for s in (1, 2, 4):
            keep = row >= s
            b8 = b8 + a8 * jnp.where(keep, pltpu.roll(b8, s, 0), 0.0)
            a8 = a8 * jnp.where(keep, pltpu.roll(a8, s, 0), 1.0)
        h8 = b8 + a8 * hprev
        b_ref[pl.ds(off, 8), :] = h8
        return jnp.broadcast_to(h8[7:8, :], (8, w))

    hc_ref[...] = lax.fori_loop(0, tt // 8, body, hc_ref[...])

    ya = b_ref[...] * _gelu_tanh(y_ref[...])
    o_ref[...] = _rms(ya, gn_ref[...]).astype(o_ref.dtype)


def _lru(p, batch, seq, cw, cb, wg, bg, lam, gn, tt):
    w = LRU_WIDTH
    nt = seq // tt
    return pl.pallas_call(
        _lru_kernel,
        grid=(batch, nt),
        in_specs=[
            pl.BlockSpec((tt, w), lambda b, t: (b * nt + t, OFF_LRU_X // w)),
            pl.BlockSpec((tt, w), lambda b, t: (b * nt + t, OFF_LRU_Y // w)),
            pl.BlockSpec((CONV_WIDTH, w), lambda b, t: (0, 0)),
            pl.BlockSpec((1, w), lambda b, t: (0, 0)),
            pl.BlockSpec((w, 2 * w), lambda b, t: (0, 0)),
            pl.BlockSpec((1, 2 * w), lambda b, t: (0, 0)),
            pl.BlockSpec((1, w), lambda b, t: (0, 0)),
            pl.BlockSpec((1, w), lambda b, t: (0, 0)),
        ],
        out_specs=pl.BlockSpec((tt, w), lambda b, t: (b * nt + t, 0)),
        out_shape=jax.ShapeDtypeStruct((batch * seq, w), BF16),
        scratch_shapes=[
            pltpu.VMEM((tt + 8, w), F32),
            pltpu.VMEM((tt, w), F32),
            pltpu.VMEM((tt, w), F32),
            pltpu.VMEM((8, w), F32),
        ],
        compiler_params=_cparams(("parallel", "arbitrary")),
        name="lru",
    )(p, p, cw, cb, wg, bg, lam, gn)


def _rope(x, cosf, sina, sinb):
    return x * cosf + pltpu.roll(x, LANES - ROPE_HALF, 1) * sina + pltpu.roll(x, ROPE_HALF, 1) * sinb


def _nsa_prep_kernel(q_ref, ks_ref, vs_ref, kw_ref, vw_ref, cos_ref, sina_ref, sinb_ref,
                     qn_ref, ksn_ref, kwn_ref,
                     qc_ref, qr_ref, kso_ref, vso_ref, kwo_ref, vwo_ref):
    cosf, sina, sinb = cos_ref[...], sina_ref[...], sinb_ref[...]
    for h in range(NSA_HEADS):
        sl = slice(h * HEAD_DIM, (h + 1) * HEAD_DIM)
        qh = _rms(q_ref[:, sl], qn_ref[...])
        qc_ref[:, sl] = (qh * QK_LOG2_SCALE).astype(BF16)
        qr_ref[:, sl] = (_rope(qh, cosf, sina, sinb) * QK_LOG2_SCALE).astype(BF16)
    for h in range(NSA_KV_HEADS):
        sl = slice(h * HEAD_DIM, (h + 1) * HEAD_DIM)
        kso_ref[:, sl] = _rope(_rms(ks_ref[:, sl], ksn_ref[...]), cosf, sina, sinb).astype(BF16)
        kwo_ref[:, sl] = _rope(_rms(kw_ref[:, sl], kwn_ref[...]), cosf, sina, sinb).astype(BF16)
        ones = jnp.ones((VT_ROWS - HEAD_DIM, vs_ref.shape[0]), BF16)
        vso_ref[0, h, 0:HEAD_DIM, :] = vs_ref[:, sl].T.astype(BF16)
        vso_ref[0, h, HEAD_DIM:VT_ROWS, :] = ones
        vwo_ref[0, h, 0:HEAD_DIM, :] = vw_ref[:, sl].T.astype(BF16)
        vwo_ref[0, h, HEAD_DIM:VT_ROWS, :] = ones


def _nsa_prep(p, seq, cosf, sina, sinb, qn, ksn, kwn, tt):
    m = p.shape[0]
    nt = seq // tt
    kvw = NSA_KV_WIDTH

    def col(off, width):
        return pl.BlockSpec((tt, width), lambda i: (i, off // width))

    def tab():
        return pl.BlockSpec((tt, LANES), lambda i: (i % nt, 0))

    def vec():
        return pl.BlockSpec((1, HEAD_DIM), lambda i: (0, 0))

    def vt_spec():
        return pl.BlockSpec((1, NSA_KV_HEADS, VT_ROWS, tt), lambda i: (i // nt, 0, 0, i % nt))

    return pl.pallas_call(
        _nsa_prep_kernel,
        grid=(m // tt,),
        in_specs=[col(OFF_Q, NSA_WIDTH), col(OFF_KS, kvw), col(OFF_VS, kvw), col(OFF_KW, kvw), col(OFF_VW, kvw),
                  tab(), tab(), tab(), vec(), vec(), vec()],
        out_specs=[pl.BlockSpec((tt, NSA_WIDTH), lambda i: (i, 0)),
                   pl.BlockSpec((tt, NSA_WIDTH), lambda i: (i, 0)),
                   pl.BlockSpec((tt, kvw), lambda i: (i, 0)),
                   vt_spec(),
                   pl.BlockSpec((tt, kvw), lambda i: (i, 0)),
                   vt_spec()],
        out_shape=[jax.ShapeDtypeStruct((m, NSA_WIDTH), BF16),
                   jax.ShapeDtypeStruct((m, NSA_WIDTH), BF16),
                   jax.ShapeDtypeStruct((m, kvw), BF16),
                   jax.ShapeDtypeStruct((m // seq, NSA_KV_HEADS, VT_ROWS, seq), BF16),
                   jax.ShapeDtypeStruct((m, kvw), BF16),
                   jax.ShapeDtypeStruct((m // seq, NSA_KV_HEADS, VT_ROWS, seq), BF16)],
        compiler_params=_cparams(("parallel",)),
        name="nsa_prep",
    )(p, p, p, p, p, cosf, sina, sinb, qn, ksn, kwn)


def _nsa_cmp_kernel(k_ref, v_ref, pek_ref, w1k_ref, w2k_ref, pev_ref, w1v_ref, w2v_ref, kn_ref,
                    kc_ref, vc_ref):
    ng = k_ref.shape[0] // CMP_STRIDE
    rowid = lax.broadcasted_iota(jnp.int32, (ng, HEAD_DIM), 0)

    def compress(x_ref, pe_ref, w1_ref, w2_ref):
        first = jnp.zeros((ng, HEAD_DIM), F32)
        second = jnp.zeros((ng, HEAD_DIM), F32)
        for j in range(CMP_STRIDE):
            xj = x_ref[pl.ds(j, ng, stride=CMP_STRIDE), :]
            lo = (xj + pe_ref[j:j + 1, :]).astype(BF16)
            hi = (xj + pe_ref[CMP_STRIDE + j:CMP_STRIDE + j + 1, :]).astype(BF16)
            first += _dot(lo, w1_ref[j * HEAD_DIM:(j + 1) * HEAD_DIM, :])
            second += _dot(hi, w1_ref[(CMP_STRIDE + j) * HEAD_DIM:(CMP_STRIDE + j + 1) * HEAD_DIM, :])
        pre = first + pltpu.roll(second, ng - 1, 0)
        return _dot(_gelu_tanh(pre).astype(BF16), w2_ref[...])

    kc = _rms(compress(k_ref, pek_ref, w1k_ref, w2k_ref), kn_ref[...])
    vc = compress(v_ref, pev_ref, w1v_ref, w2v_ref)
    valid = rowid < ng - 1
    kc_ref[0, 0] = jnp.where(valid, kc, 0.0).astype(BF16)
    vc_ref[0, 0] = jnp.where(valid, vc, 0.0).T.astype(BF16)


def _nsa_cmp(p, batch, seq, pek, w1k, w2k, pev, w1v, w2v, kn):
    ng = seq // CMP_STRIDE
    hd = HEAD_DIM
    kvh = NSA_KV_HEADS

    def full(shape):
        return pl.BlockSpec(shape, lambda b, h: (0,) * len(shape))

    return pl.pallas_call(
        _nsa_cmp_kernel,
        grid=(batch, kvh),
        in_specs=[
            pl.BlockSpec((seq, hd), lambda b, h: (b, OFF_KC // hd + h)),
            pl.BlockSpec((seq, hd), lambda b, h: (b, OFF_VC // hd + h)),
            full((CMP_BLOCK, hd)), full((CMP_BLOCK * hd, hd)), full((hd, hd)),
            full((CMP_BLOCK, hd)), full((CMP_BLOCK * hd, hd)), full((hd, hd)),
            full((1, hd)),
        ],
        out_specs=[pl.BlockSpec((1, 1, ng, hd), lambda b, h: (b, h, 0, 0)),
                   pl.BlockSpec((1, 1, hd, ng), lambda b, h: (b, h, 0, 0))],
        out_shape=[jax.ShapeDtypeStruct((batch, kvh, ng, hd), BF16),
                   jax.ShapeDtypeStruct((batch, kvh, hd, ng), BF16)],
        compiler_params=_cparams(("parallel", "parallel")),
        name="nsa_cmp",
    )(p, p, pek, w1k, w2k, pev, w1v, w2v, kn)


def _nsa_attn_kernel(qc_ref, qr_ref, kc_ref, vct_ref, ks_ref, vst_ref, kw_ref, vwt_ref, gl_ref, gb_ref, ovl_ref,
                     o_ref, selb_ref, rk_ref, gt_ref, acc_ref, z_ref, p_ref, zc_ref, pcb_ref, zw_ref, pw_ref, ob_ref, *, tq, tk, group, n_sel):
    seq = ks_ref.shape[0]
    ncp = kc_ref.shape[2]
    ns = seq // SEL_BLOCK
    grp = NSA_GROUP
    hd = HEAD_DIM
    gq = grp * tq
    nblk = tk // SEL_BLOCK
    kv = pl.program_id(1)
    s0 = pl.program_id(2) * tq
    wspan = WINDOW + tq

    qc = jnp.concatenate([qc_ref[:, g * hd:(g + 1) * hd] for g in range(grp)], axis=0)
    qr = jnp.concatenate([qr_ref[:, g * hd:(g + 1) * hd] for g in range(grp)], axis=0)
    t_row = s0 + lax.broadcasted_iota(jnp.int32, (1, tq), 1)
    t_all = s0 + lax.broadcasted_iota(jnp.int32, (1, gq), 1) % tq

    def heads(x):
        return jnp.concatenate([x] * grp, axis=1)

    srow = lax.broadcasted_iota(jnp.int32, (SEL_BLOCK, 1), 0)
    chunk = SEL_BLOCK

    def biased_max(z, nrows, bias_fn):
        mx = jnp.full((8, gq), -MASK_VALUE, F32)
        for c in range(nrows // chunk):
            rs = slice(c * chunk, (c + 1) * chunk)
            zb = z[rs, :]
            if bias_fn is not None:
                zb = zb + heads(bias_fn(c))
                z[rs, :] = zb
            mx = jnp.maximum(mx, jnp.max(zb.reshape(chunk // 8, 8, gq), axis=0))
        return jnp.max(mx, axis=0, keepdims=True)

    start = pl.multiple_of(jnp.maximum(s0 - WINDOW, 0), tq)
    zc_ref[...] = _dot_nt(kc_ref[0, 0], qc)
    zw_ref[...] = _dot_nt(kw_ref[pl.ds(start, wspan), :], qr)

    def scores(kt):
        return _dot_nt(ks_ref[pl.ds(pl.multiple_of(kt * tk, tk), tk), :], qr)

    z_ref[0] = scores(0)

    def cmp_bias(c):
        cend = (c * chunk + srow) * CMP_STRIDE + (CMP_BLOCK - 1)
        return jnp.where(cend <= t_row, 0.0, -MASK_VALUE)

    def win_bias(c):
        kpos = start + c * chunk + srow
        return jnp.where((kpos <= t_row) & (kpos > t_row - WINDOW), 0.0, -MASK_VALUE)

    mc = biased_max(zc_ref, ncp, cmp_bias)
    mw = biased_max(zw_ref, wspan, win_bias)
    den8 = jnp.zeros((8, gq), F32)
    for c in range(ncp // chunk):
        rs = slice(c * chunk, (c + 1) * chunk)
        e = jnp.exp2(zc_ref[rs, :] - mc)
        zc_ref[rs, :] = e
        pcb_ref[rs, :] = e.astype(BF16)
        den8 = den8 + jnp.sum(e.reshape(chunk // 8, 8, gq), axis=0)
    for c in range(wspan // chunk):
        rs = slice(c * chunk, (c + 1) * chunk)
        pw_ref[rs, :] = jnp.exp2(zw_ref[rs, :] - mw).astype(BF16)
    inv_c = jnp.where(t_all >= CMP_BLOCK - 1, 1.0 / jnp.sum(den8, axis=0, keepdims=True), 0.0)
    ob_ref[0] = _dot(vct_ref[0, 0], pcb_ref[...]) * inv_c
    ow = _dot(vwt_ref[0, 0, :, pl.ds(start, wspan)], pw_ref[...])
    ob_ref[1] = ow[0:hd, :] * (1.0 / ow[hd:hd + 1, :])

    pcs = []
    for c in range(ncp // chunk):
        pn = zc_ref[c * chunk:(c + 1) * chunk, :] * inv_c
        pcs.append(sum(pn[:, g * tq:(g + 1) * tq] for g in range(1, grp)) + pn[:, 0:tq])
    pcs = jnp.concatenate(pcs, axis=0)
    pcs_hi = pcs.astype(BF16)
    pcs_lo = (pcs - pcs_hi.astype(F32)).astype(BF16)
    imp = _dot(ovl_ref[...], pcs_hi) + _dot(ovl_ref[...], pcs_lo)
    jrow = lax.broadcasted_iota(jnp.int32, (ns, tq), 0)
    cur = (s0 + lax.broadcasted_iota(jnp.int32, (ns, tq), 1)) // SEL_BLOCK
    forced = (jrow == 0) | (jrow == cur) | (jrow == cur - 1)
    imp = jnp.where(forced, MASK_VALUE, jnp.where(jrow <= cur, imp, -MASK_VALUE))
    ngrp = ns // 8
    imp_g = [imp[8 * r:8 * r + 8, :] for r in range(ngrp)]
    sub = lax.broadcasted_iota(jnp.int32, (8, tq), 0)
    rk_ref[...] = jnp.zeros_like(rk_ref)
    for ib in range(ngrp):
        @pl.when(ib * 8 * SEL_BLOCK < s0 + tq)
        def _():
            add = [jnp.zeros((8, tq), F32) for _ in range(ngrp)]
            for i in range(8 * ib, 8 * ib + 8):
                other = jnp.broadcast_to(imp[i:i + 1, :], (8, tq))
                for r in range(ngrp):
                    if r > ib:
                        ahead = jnp.where(other >= imp_g[r], 1.0, 0.0)
                    elif r < ib:
                        ahead = jnp.where(other > imp_g[r], 1.0, 0.0)
                    else:
                        ahead = jnp.where(sub > i - 8 * r, jnp.where(other >= imp_g[r], 1.0, 0.0),
                                          jnp.where(other > imp_g[r], 1.0, 0.0))
                    add[r] = add[r] + ahead
            for r in range(ngrp):
                rk_ref[8 * r:8 * r + 8, :] += add[r]
    selb_ref[...] = jnp.where(rk_ref[...] < n_sel, 0.0, -MASK_VALUE)

    acc_ref[...] = jnp.zeros_like(acc_ref)

    n_kt = (s0 + tq + tk - 1) // tk
    n_groups = (n_kt + group - 1) // group

    def tile_step(kt, m, slot):
        z_ref[1 - slot] = scores(jnp.minimum(kt + 1, group * n_groups - 1))
        koff = pl.multiple_of(kt * tk, tk)
        zs = z_ref.at[slot]

        def sel_bias(j):
            return (jnp.broadcast_to(selb_ref[pl.ds(kt * nblk + j, 1), :], (chunk, tq))
                    + jnp.where(koff + j * chunk + srow <= t_row, 0.0, -MASK_VALUE))

        m_new = jnp.maximum(m, biased_max(zs, tk, sel_bias))
        for j in range(nblk):
            rs = slice(j * chunk, (j + 1) * chunk)
            p_ref[rs, :] = jnp.exp2(zs[rs, :] - m_new).astype(BF16)
        acc_ref[...] = acc_ref[...] * jnp.exp2(m - m_new) + _dot(vst_ref[0, 0, :, pl.ds(koff, tk)], p_ref[...])
        return m_new

    def group_body(gi, m):
        for u in range(group):
            m = tile_step(group * gi + u, m, u % 2)
        return m

    lax.fori_loop(0, n_groups, group_body, jnp.full((1, gq), -MASK_VALUE, F32))
    o_sel = acc_ref[0:hd, :] * (1.0 / acc_ref[hd:hd + 1, :])

    gt_ref[...] = _sigmoid(gl_ref[...] + gb_ref[...]).T
    for g in range(grp):
        base = MISC_GATE + (kv * grp + g) * 3
        cs = slice(g * tq, (g + 1) * tq)
        og = (gt_ref[pl.ds(base, 1), :] * ob_ref[0, :, cs] + gt_ref[pl.ds(base + 1, 1), :] * o_sel[:, cs]
              + gt_ref[pl.ds(base + 2, 1), :] * ob_ref[1, :, cs])
        o_ref[:, g * hd:(g + 1) * hd] = og.T


def _nsa_attn(p, qc, qr, kc, vct, ks, vst, kw, vwt, gate_b, ovl, batch, seq, tq, tk, group):
    hd = HEAD_DIM
    gw = NSA_GROUP * hd
    nq = seq // tq
    ns = seq // SEL_BLOCK
    ncp = kc.shape[2]
    n_sel = min(N_SELECT, ns)
    assert seq % (group * tk) == 0 and group % 2 == 0 and tk % SEL_BLOCK == 0
    assert seq >= WINDOW + tq and WINDOW % tq == 0

    def qspec():
        return pl.BlockSpec((tq, gw), lambda b, h, i: (b * nq + i, h))

    def kspec(width):
        return pl.BlockSpec((seq, width), lambda b, h, i: (b, h))

    def vtspec():
        return pl.BlockSpec((1, 1, VT_ROWS, seq), lambda b, h, i: (b, h, 0, 0))

    return pl.pallas_call(
        functools.partial(_nsa_attn_kernel, tq=tq, tk=tk, group=group, n_sel=n_sel),
        grid=(batch, NSA_KV_HEADS, nq),
        in_specs=[qspec(), qspec(),
                  pl.BlockSpec((1, 1, ncp, hd), lambda b, h, i: (b, h, 0, 0)),
                  pl.BlockSpec((1, 1, hd, ncp), lambda b, h, i: (b, h, 0, 0)),
                  kspec(hd), vtspec(), kspec(hd), vtspec(),
                  pl.BlockSpec((tq, LANES), lambda b, h, i: (b * nq + i, OFF_MISC // LANES)),
                  pl.BlockSpec((1, LANES), lambda b, h, i: (0, 0)),
                  pl.BlockSpec((ns, ncp), lambda b, h, i: (0, 0))],
        out_specs=pl.BlockSpec((tq, gw), lambda b, h, i: (b * nq + i, h)),
        out_shape=jax.ShapeDtypeStruct((batch * seq, NSA_WIDTH), F32),
        scratch_shapes=[pltpu.VMEM((ns, tq), F32), pltpu.VMEM((ns, tq), F32), pltpu.VMEM((LANES, tq), F32),
                        pltpu.VMEM((VT_ROWS, NSA_GROUP * tq), F32),
                        pltpu.VMEM((2, tk, NSA_GROUP * tq), F32), pltpu.VMEM((tk, NSA_GROUP * tq), BF16),
                        pltpu.VMEM((ncp, NSA_GROUP * tq), F32), pltpu.VMEM((ncp, NSA_GROUP * tq), BF16),
                        pltpu.VMEM((WINDOW + tq, NSA_GROUP * tq), F32),
                        pltpu.VMEM((WINDOW + tq, NSA_GROUP * tq), BF16),
                        pltpu.VMEM((2, hd, NSA_GROUP * tq), F32)],
        compiler_params=_cparams(("parallel", "parallel", "arbitrary")),
        name="nsa_attn",
    )(qc, qr, kc, vct, ks, vst, kw, vwt, p, gate_b, ovl)


def _gla_kernel(q_ref, k_ref, v_ref, g_ref, misc_ref, a2_ref, ab_ref, gn_ref, ex_ref, o_ref,
                st_ref, qe_ref, kd_ref, vb_ref, ds_ref, oi_ref, od_ref):
    nb, tt, wq = q_ref.shape
    wv = v_ref.shape[2]
    step = GLA_STEP
    ng = tt // step
    nh = GLA_HEADS
    hl = LANES
    per = hl // GLA_DK

    @pl.when(pl.program_id(1) == 0)
    def _():
        st_ref[...] = jnp.zeros_like(st_ref)

    pos = lax.broadcasted_iota(jnp.int32, (tt, wq), 0) % step
    pos3 = pos.reshape(ng, step, wq)
    lane = lax.broadcasted_iota(jnp.int32, (tt, hl), 1)

    def prepare(s_i, carry):
        x = _dot(misc_ref[s_i].astype(BF16), a2_ref[...]) + ab_ref[...]
        b = (-_softplus(-x)) * np.float32(np.log2(np.e) / GLA_TAU)
        s = 1
        while s < step:
            b = b + jnp.where(pos >= s, pltpu.roll(b, s, 0), 0.0)
            s *= 2

        q3 = (q_ref[s_i] * np.float32(GLA_DK ** -0.5)).reshape(ng, step, wq)
        k3 = k_ref[s_i].reshape(ng, step, wq)
        v3 = v_ref[s_i].reshape(ng, step, wv)
        b3 = b.reshape(ng, step, wq)
        b_last = b3[:, step - 1:step, :]
        qe_ref[s_i] = (q3 * jnp.exp2(b3)).reshape(tt, wq).astype(BF16)
        kd = (k3 * jnp.exp2(b_last - b3)).reshape(tt, wq)
        for h in range(nh):
            grp_sl = slice((h // per) * hl, (h // per + 1) * hl)
            mine = (lane >= (h % per) * GLA_DK) & (lane < (h % per + 1) * GLA_DK)
            kd_ref[s_i * nh + h] = jnp.where(mine, kd[:, grp_sl], 0.0).astype(BF16)
        vb_ref[s_i] = v_ref[s_i].astype(BF16)
        ds_ref[s_i] = jnp.exp2(b_last)

        half = step // 2
        odiag = jnp.zeros((ng, step, wv), F32)
        odiag_hi = jnp.zeros((ng, half, wv), F32)
        q3h, b3h = q3[:, half:, :], b3[:, half:, :]
        pos3h = half + lax.broadcasted_iota(jnp.int32, (ng, half, wq), 1)
        for j in range(step):
            if j < half:
                dec = jnp.exp2(jnp.where(pos3 >= j, b3 - b3[:, j:j + 1, :], -MASK_VALUE))
                term = (q3 * k3[:, j:j + 1, :] * dec).reshape(tt, wq).astype(BF16)
                odiag = odiag + _dot(term, ex_ref[...]).reshape(ng, step, wv) * v3[:, j:j + 1, :]
            else:
                dec = jnp.exp2(jnp.where(pos3h >= j, b3h - b3[:, j:j + 1, :], -MASK_VALUE))
                term = (q3h * k3[:, j:j + 1, :] * dec).reshape(tt // 2, wq).astype(BF16)
                odiag_hi = odiag_hi + _dot(term, ex_ref[...]).reshape(ng, half, wv) * v3[:, j:j + 1, :]
        odiag = jnp.concatenate([odiag[:, :half, :], odiag[:, half:, :] + odiag_hi], axis=1)
        od_ref[s_i] = odiag.reshape(tt, wv)
        return carry

    lax.fori_loop(0, nb, prepare, 0)

    def body(gi, carry):
        r0 = pl.multiple_of(gi * step, step)
        for s_i in range(nb):
            dsg = ds_ref[s_i, gi]
            for h in range(nh):
                sl = slice(h * hl, (h + 1) * hl)
                grp_sl = slice((h // per) * hl, (h // per + 1) * hl)
                st = st_ref[s_i * nh + h]
                oi_ref[s_i, pl.ds(r0, step), sl] = _dot_nt(qe_ref[s_i, pl.ds(r0, step), grp_sl], st.astype(BF16))
                upd = _dot_tn(vb_ref[s_i, pl.ds(r0, step), sl], kd_ref[s_i * nh + h, pl.ds(r0, step), :])
                st_ref[s_i * nh + h] = st * dsg[:, grp_sl] + upd
        return carry

    lax.fori_loop(0, ng, body, 0)

    def finish(s_i, carry):
        o = oi_ref[s_i] + od_ref[s_i]
        gate = g_ref[s_i]
        gate = gate * _sigmoid(gate)
        for h in range(nh):
            sl = slice(h * hl, (h + 1) * hl)
            o_ref[s_i, :, sl] = (_rms(o[:, sl], gn_ref[...]) * gate[:, sl]).astype(o_ref.dtype)
        return carry

    lax.fori_loop(0, nb, finish, 0)


def _gla(p, batch, seq, a2, ab, gn, tt, nb):
    wq = GLA_QK_WIDTH
    wv = GLA_WIDTH
    nt = seq // tt
    ng = tt // GLA_STEP
    expand = jnp.asarray(np.kron(np.eye(GLA_HEADS), np.ones((GLA_DK, GLA_DV))), dtype=BF16)
    p3 = p.reshape(batch, seq, p.shape[1])

    def col(off, width):
        return pl.BlockSpec((nb, tt, width), lambda b, t: (b, t, off // width))

    out = pl.pallas_call(
        _gla_kernel,
        grid=(batch // nb, nt),
        in_specs=[col(OFF_GQ, wq), col(OFF_GK, wq), col(OFF_GV, wv), col(OFF_GG, wv), col(OFF_MISC, LANES),
                  pl.BlockSpec((LANES, wq), lambda b, t: (0, 0)),
                  pl.BlockSpec((1, wq), lambda b, t: (0, 0)),
                  pl.BlockSpec((1, GLA_DV), lambda b, t: (0, 0)),
                  pl.BlockSpec((wq, wv), lambda b, t: (0, 0))],
        out_specs=pl.BlockSpec((nb, tt, wv), lambda b, t: (b, t, 0)),
        out_shape=jax.ShapeDtypeStruct((batch, seq, wv), BF16),
        scratch_shapes=[
            pltpu.VMEM((nb * GLA_HEADS, GLA_DV, LANES), F32),
            pltpu.VMEM((nb, tt, wq), BF16),
            pltpu.VMEM((nb * GLA_HEADS, tt, LANES), BF16),
            pltpu.VMEM((nb, tt, wv), BF16),
            pltpu.VMEM((nb, ng, 1, wq), F32),
            pltpu.VMEM((nb, tt, wv), F32),
            pltpu.VMEM((nb, tt, wv), F32),
        ],
        compiler_params=_cparams(("parallel", "arbitrary")),
        name="gla",
    )(p3, p3, p3, p3, p3, a2, ab, gn, expand)
    return out.reshape(batch * seq, wv)


def _pack_w_in(w_in):
    o_gate = OFF_VW + NSA_KV_WIDTH
    o_gq = o_gate + NSA_HEADS * 3
    o_ga = o_gq + 2 * GLA_QK_WIDTH + 2 * GLA_WIDTH
    tail = jnp.zeros(w_in.shape[:2] + (N_PACKED - OFF_MISC - NSA_HEADS * 3 - GLA_RANK,), w_in.dtype)
    packed = jnp.concatenate([w_in[..., :o_gate], w_in[..., o_gq:o_ga],
                              w_in[..., o_gate:o_gq], w_in[..., o_ga:o_ga + GLA_RANK], tail], axis=-1)
    assert packed.shape[-1] == N_PACKED
    return packed.astype(BF16)


def _block_diag(w):
    nb, bi, bo = w.shape
    eye = jnp.eye(nb, dtype=w.dtype)
    return (eye[:, None, :, None] * w[:, :, None, :]).reshape(nb * bi, nb * bo)


def _rope_lane_tables(seq):
    inv = 1.0 / (ROPE_THETA ** (jnp.arange(0, ROPE_DIM, 2, dtype=F32) / ROPE_DIM))
    ang = jnp.arange(seq, dtype=F32)[:, None] * inv[None, :]
    cos, sin = jnp.cos(ang), jnp.sin(ang)
    rest = LANES - ROPE_DIM
    zeros_h = jnp.zeros((seq, ROPE_HALF), F32)
    cosf = jnp.concatenate([cos, cos, jnp.ones((seq, rest), F32)], axis=1)
    sina = jnp.concatenate([-sin, zeros_h, jnp.zeros((seq, rest), F32)], axis=1)
    sinb = jnp.concatenate([zeros_h, sin, jnp.zeros((seq, rest), F32)], axis=1)
    return cosf, sina, sinb


def _overlap_t(seq, ncp):
    ns = seq // SEL_BLOCK
    nc = (seq - CMP_BLOCK) // CMP_STRIDE + 1
    cs = np.arange(ncp) * CMP_STRIDE
    ss = np.arange(ns) * SEL_BLOCK
    ov = (cs[None, :] < ss[:, None] + SEL_BLOCK) & (cs[None, :] + CMP_BLOCK > ss[:, None]) & (np.arange(ncp) < nc)[None, :]
    return jnp.asarray(ov.astype(np.float32), dtype=BF16)


def _row(v):
    return v.reshape(1, -1)


def _mixers(p, batch, seq, l, prm, tables, ovl_t):
    cosf, sina, sinb = tables
    wg = jnp.concatenate([_block_diag(prm['lru_gate_a_w'][l]), _block_diag(prm['lru_gate_x_w'][l])], axis=1).astype(BF16)
    bg = jnp.concatenate([prm['lru_gate_a_b'][l], prm['lru_gate_x_b'][l]]).reshape(1, -1)
    ya = _lru(p, batch, seq, prm['lru_conv_w'][l], _row(prm['lru_conv_b'][l]), wg, bg,
              _row(prm['lru_lambda'][l]), _row(prm['lru_out_norm'][l]), tt=min(1024, seq))
    qc, qr, ks, vs, kw, vw = _nsa_prep(p, seq, cosf, sina, sinb, _row(prm['nsa_q_norm'][l]),
                                       _row(prm['nsa_k_sel_norm'][l]), _row(prm['nsa_k_win_norm'][l]),
                                       tt=min(1024, seq))
    kc, vc = _nsa_cmp(p, batch, seq, prm['nsa_cmp_pe_k'][l], prm['nsa_cmp_w1_k'][l].astype(BF16),
                      prm['nsa_cmp_w2_k'][l].astype(BF16), prm['nsa_cmp_pe_v'][l],
                      prm['nsa_cmp_w1_v'][l].astype(BF16), prm['nsa_cmp_w2_v'][l].astype(BF16),
                      _row(prm['nsa_k_cmp_norm'][l]))
    gate_b = jnp.pad(prm['nsa_gate_b'][l], (MISC_GATE, LANES - MISC_GATE - NSA_HEADS * 3)).reshape(1, LANES)
    yb = _nsa_attn(p, qc, qr, kc, vc, ks, vs, kw, vw, gate_b, ovl_t, batch, seq, tq=256, tk=min(512, seq // 2), group=2)
    a2 = jnp.zeros((LANES, GLA_QK_WIDTH), F32).at[MISC_GA:MISC_GA + GLA_RANK].set(prm['gla_a_w2'][l]).astype(BF16)
    ab = _row(prm['gla_a_b'][l])
    yc = _gla(p, batch, seq, a2, ab, _row(prm['gla_out_norm'][l]), tt=min(256, seq),
              nb=next(n for n in (8, 4, 2, 1) if batch % n == 0))
    return ya, yb, yc


def kernel(x, ffn1_norm, ffn1_w_gate, ffn1_w_up, ffn1_w_down, mix_norm, w_in, lru_conv_w, lru_conv_b, lru_gate_a_w, lru_gate_a_b, lru_gate_x_w, lru_gate_x_b, lru_lambda, lru_out_norm, nsa_q_norm, nsa_k_cmp_norm, nsa_k_sel_norm, nsa_k_win_norm, nsa_cmp_pe_k, nsa_cmp_w1_k, nsa_cmp_w2_k, nsa_cmp_pe_v, nsa_cmp_w1_v, nsa_cmp_w2_v, nsa_gate_b, nsa_out_norm, gla_a_w2, gla_a_b, gla_out_norm, w_out, ffn2_norm, ffn2_w_gate, ffn2_w_up, ffn2_w_down):
    prm = dict(lru_conv_w=lru_conv_w, lru_conv_b=lru_conv_b, lru_gate_a_w=lru_gate_a_w, lru_gate_a_b=lru_gate_a_b,
               lru_gate_x_w=lru_gate_x_w, lru_gate_x_b=lru_gate_x_b, lru_lambda=lru_lambda, lru_out_norm=lru_out_norm,
               nsa_q_norm=nsa_q_norm, nsa_k_cmp_norm=nsa_k_cmp_norm, nsa_k_sel_norm=nsa_k_sel_norm,
               nsa_k_win_norm=nsa_k_win_norm, nsa_cmp_pe_k=nsa_cmp_pe_k, nsa_cmp_w1_k=nsa_cmp_w1_k,
               nsa_cmp_w2_k=nsa_cmp_w2_k, nsa_cmp_pe_v=nsa_cmp_pe_v, nsa_cmp_w1_v=nsa_cmp_w1_v,
               nsa_cmp_w2_v=nsa_cmp_w2_v, nsa_gate_b=nsa_gate_b, gla_a_w2=gla_a_w2, gla_a_b=gla_a_b,
               gla_out_norm=gla_out_norm)
    batch, seq, d = x.shape
    depth = w_in.shape[0]
    m = batch * seq
    tables = _rope_lane_tables(seq)
    ncp = seq // CMP_STRIDE
    ovl_t = _overlap_t(seq, ncp)
    tm = min(512, m)
    tm_ffn = min(1024, m)
    tf = 512
    xf = x.reshape(m, d)
    ffn1 = [w.astype(BF16) for w in (ffn1_w_gate, ffn1_w_up, ffn1_w_down)]
    ffn2 = [w.astype(BF16) for w in (ffn2_w_gate, ffn2_w_up, ffn2_w_down)]
    w_in_packed = _pack_w_in(w_in)
    w_out_bf = w_out.astype(BF16)
    for l in range(depth):
        xf = _ffn(xf, _row(ffn1_norm[l]), *ffn1, l, tm_ffn, tf)
        p = _mix_in(xf, _row(mix_norm[l]), w_in_packed, l, tm_ffn, N_PACKED // MIX_IN_TILES)
        ya, yb, yc = _mixers(p, batch, seq, l, prm, tables, ovl_t)
        xf = _mix_out(xf, ya, yb, yc, _row(nsa_out_norm[l]), w_out_bf, l, tm)
        xf = _ffn(xf, _row(ffn2_norm[l]), *ffn2, l, tm_ffn, tf)
    return xf.reshape(batch, seq, d)
```

```python
import functools

import numpy as np
import jax
import jax.numpy as jnp
from jax import lax
from jax.experimental import pallas as pl
from jax.experimental.pallas import tpu as pltpu

F32 = jnp.float32
BF16 = jnp.bfloat16

NORM_EPS = 1e-6
D_MODEL = 2048
D_FF = 5632

LRU_WIDTH = 512
LRU_BLOCKS = 8
LRU_BLOCK_SIZE = LRU_WIDTH // LRU_BLOCKS
CONV_WIDTH = 4
LRU_C = 8.0

HEAD_DIM = 128
NSA_HEADS = 8
NSA_KV_HEADS = 2
NSA_GROUP = NSA_HEADS // NSA_KV_HEADS
NSA_WIDTH = NSA_HEADS * HEAD_DIM
NSA_KV_WIDTH = NSA_KV_HEADS * HEAD_DIM
ROPE_DIM = HEAD_DIM // 4
ROPE_HALF = ROPE_DIM // 2
ROPE_THETA = 500000.0
CMP_BLOCK = 32
CMP_STRIDE = 16
SEL_BLOCK = 64
N_SELECT = 16
WINDOW = 512
MASK_VALUE = 1e30
QK_LOG2_SCALE = float(HEAD_DIM ** -0.5 * np.log2(np.e))
VT_ROWS = HEAD_DIM + 16

GLA_DV = 128
GLA_HEADS = 4
GLA_DK = 64
GLA_WIDTH = GLA_HEADS * GLA_DV
GLA_QK_WIDTH = GLA_HEADS * GLA_DK
GLA_RANK = 16
GLA_TAU = 16.0
GLA_STEP = 16

LANES = 128

OFF_LRU_X = 0
OFF_LRU_Y = OFF_LRU_X + LRU_WIDTH
OFF_Q = OFF_LRU_Y + LRU_WIDTH
OFF_KC = OFF_Q + NSA_WIDTH
OFF_VC = OFF_KC + NSA_KV_WIDTH
OFF_KS = OFF_VC + NSA_KV_WIDTH
OFF_VS = OFF_KS + NSA_KV_WIDTH
OFF_KW = OFF_VS + NSA_KV_WIDTH
OFF_VW = OFF_KW + NSA_KV_WIDTH
OFF_GQ = OFF_VW + NSA_KV_WIDTH
OFF_GK = OFF_GQ + GLA_QK_WIDTH
OFF_GV = OFF_GK + GLA_QK_WIDTH
OFF_GG = OFF_GV + GLA_WIDTH
OFF_MISC = OFF_GG + GLA_WIDTH
MIX_IN_TILES = 3
N_PACKED = -(-(OFF_MISC + LANES) // (2 * LANES * MIX_IN_TILES)) * (2 * LANES * MIX_IN_TILES)
MISC_GATE = 0
MISC_GA = NSA_HEADS * 3

VMEM_LIMIT = 56 * 1024 * 1024
FFN_SUB = 256


def _cparams(sem):
    return pltpu.CompilerParams(dimension_semantics=sem, vmem_limit_bytes=VMEM_LIMIT)


def _sigmoid(x):
    return 0.5 * jnp.tanh(0.5 * x) + 0.5


def _softplus(x):
    return jnp.maximum(x, 0.0) + jnp.log1p(jnp.exp(-jnp.abs(x)))


def _gelu_tanh(x):
    c = np.float32(np.sqrt(2.0 / np.pi))
    return x * (0.5 * (1.0 + jnp.tanh(c * (x + 0.044715 * (x * x * x)))))


def _rms(x, g):
    return x * lax.rsqrt(jnp.mean(x * x, axis=-1, keepdims=True) + NORM_EPS) * g


def _dot(a, b):
    return jnp.dot(a, b, preferred_element_type=F32)


def _dot_nt(a, b):
    return lax.dot_general(a, b, (((1,), (1,)), ((), ())), preferred_element_type=F32)


def _dot_tn(a, b):
    return lax.dot_general(a, b, (((0,), (0,)), ((), ())), preferred_element_type=F32)


def _ffn_kernel(x_ref, g_ref, wg_ref, wu_ref, wd_ref, o_ref, h_ref):
    j = pl.program_id(1)
    tf = wg_ref.shape[1]

    def half_swiglu():
        h = h_ref[...]
        acc = None
        for c in range(tf // FFN_SUB):
            cs = slice(c * FFN_SUB, (c + 1) * FFN_SUB)
            gate = _dot(h, wg_ref[:, cs])
            up = _dot(h, wu_ref[:, cs])
            act = (gate * _sigmoid(gate)) * (0.5 * up)
            part = _dot(act.astype(BF16), wd_ref[cs, :])
            acc = part if acc is None else acc + part
        return acc

    @pl.when(j == 0)
    def _():
        x = x_ref[...]
        h_ref[...] = _rms(x, g_ref[...]).astype(BF16)
        o_ref[...] = x + half_swiglu()

    @pl.when(j > 0)
    def _():
        o_ref[...] += half_swiglu()


def _ffn(x, g, wg, wu, wd, layer, tm, tf):
    m, d = x.shape
    f = wg.shape[2]
    return pl.pallas_call(
        _ffn_kernel,
        grid=(m // tm, f // tf),
        in_specs=[
            pl.BlockSpec((tm, d), lambda i, j: (i, 0)),
            pl.BlockSpec((1, d), lambda i, j: (0, 0)),
            pl.BlockSpec((None, d, tf), lambda i, j: (layer, 0, j)),
            pl.BlockSpec((None, d, tf), lambda i, j: (layer, 0, j)),
            pl.BlockSpec((None, tf, d), lambda i, j: (layer, j, 0)),
        ],
        out_specs=pl.BlockSpec((tm, d), lambda i, j: (i, 0)),
        out_shape=jax.ShapeDtypeStruct((m, d), F32),
        scratch_shapes=[pltpu.VMEM((tm, d), BF16)],
        compiler_params=_cparams(("parallel", "arbitrary")),
        name="ffn",
    )(x, g, wg, wu, wd)


def _mix_in_kernel(x_ref, g_ref, w_ref, o_ref, h_ref):
    @pl.when(pl.program_id(1) == 0)
    def _():
        h_ref[...] = _rms(x_ref[...], g_ref[...]).astype(BF16)
        o_ref[...] = _dot(h_ref[...], w_ref[...])

    @pl.when(pl.program_id(1) > 0)
    def _():
        o_ref[...] = _dot(h_ref[...], w_ref[...])


def _mix_in(x, g, w, layer, tm, tn):
    m, d = x.shape
    n = w.shape[2]
    return pl.pallas_call(
        _mix_in_kernel,
        grid=(m // tm, n // tn),
        in_specs=[
            pl.BlockSpec((tm, d), lambda i, j: (i, 0)),
            pl.BlockSpec((1, d), lambda i, j: (0, 0)),
            pl.BlockSpec((None, d, tn), lambda i, j: (layer, 0, j)),
        ],
        out_specs=pl.BlockSpec((tm, tn), lambda i, j: (i, j)),
        out_shape=jax.ShapeDtypeStruct((m, n), F32),
        scratch_shapes=[pltpu.VMEM((tm, d), BF16)],
        compiler_params=_cparams(("parallel", "arbitrary")),
        name="mix_in",
    )(x, g, w)


def _mix_out_kernel(x_ref, ya_ref, yb_ref, yc_ref, gb_ref, w_ref, o_ref):
    wa, wb = ya_ref.shape[1], yb_ref.shape[1]
    nb = _rms(yb_ref[...], gb_ref[...]).astype(BF16)
    acc = _dot(ya_ref[...], w_ref[0:wa, :])
    acc += _dot(nb, w_ref[wa:wa + wb, :])
    acc += _dot(yc_ref[...], w_ref[wa + wb:, :])
    o_ref[...] = x_ref[...] + acc


def _mix_out(x, ya, yb, yc, gb, w_out, layer, tm):
    m, d = x.shape
    wa, wb, wc = LRU_WIDTH, NSA_WIDTH, GLA_WIDTH
    return pl.pallas_call(
        _mix_out_kernel,
        grid=(m // tm,),
        in_specs=[
            pl.BlockSpec((tm, d), lambda i: (i, 0)),
            pl.BlockSpec((tm, wa), lambda i: (i, 0)),
            pl.BlockSpec((tm, wb), lambda i: (i, 0)),
            pl.BlockSpec((tm, wc), lambda i: (i, 0)),
            pl.BlockSpec((1, wb), lambda i: (0, 0)),
            pl.BlockSpec((None, wa + wb + wc, d), lambda i: (layer, 0, 0)),
        ],
        out_specs=pl.BlockSpec((tm, d), lambda i: (i, 0)),
        out_shape=jax.ShapeDtypeStruct((m, d), F32),
        compiler_params=_cparams(("parallel",)),
        name="mix_out",
    )(x, ya, yb, yc, gb, w_out)


def _lru_kernel(u_ref, y_ref, cw_ref, cb_ref, wg_ref, bg_ref, lam_ref, gn_ref, o_ref,
                ext_ref, a_ref, b_ref, hc_ref):
    tt, w = u_ref.shape

    @pl.when(pl.program_id(1) == 0)
    def _():
        ext_ref[0:8, :] = jnp.zeros((8, w), F32)
        hc_ref[...] = jnp.zeros_like(hc_ref)

    u = u_ref[...]
    ext_ref[8:8 + tt, :] = u
    xc = cb_ref[...] + cw_ref[CONV_WIDTH - 1:CONV_WIDTH, :] * u
    for k in range(CONV_WIDTH - 1):
        sh = CONV_WIDTH - 1 - k
        xc = xc + cw_ref[k:k + 1, :] * ext_ref[8 - sh:8 - sh + tt, :]
    ext_ref[0:8, :] = u[tt - 8:tt, :]

    gates = _dot(xc.astype(BF16), wg_ref[...]) + bg_ref[...]
    r = _sigmoid(gates[:, :w])
    i = _sigmoid(gates[:, w:])
    log_a = (-LRU_C) * r * _softplus(-lam_ref[...])
    a = jnp.exp(log_a)
    a_ref[...] = a
    b_ref[...] = jnp.sqrt(-jnp.tanh(log_a) * (a * a + 1.0)) * i * xc

    row = lax.broadcasted_iota(jnp.int32, (8, w), 0)

    def body(gi, hprev):
        off = pl.multiple_of(gi * 8, 8)
        a8 = a_ref[pl.ds(off, 8), :]
        b8 = b_ref[pl.ds(off, 8), :]
        for s in (1, 2, 4):
            keep = row >= s
            b8 = b8 + a8 * jnp.where(keep, pltpu.roll(b8, s, 0), 0.0)
            a8 = a8 * jnp.where(keep, pltpu.roll(a8, s, 0), 1.0)
        h8 = b8 + a8 * hprev
        b_ref[pl.ds(off, 8), :] = h8
        return jnp.broadcast_to(h8[7:8, :], (8, w))

    hc_ref[...] = lax.fori_loop(0, tt // 8, body, hc_ref[...])

    ya = b_ref[...] * _gelu_tanh(y_ref[...])
    o_ref[...] = _rms(ya, gn_ref[...]).astype(o_ref.dtype)


def _lru(p, batch, seq, cw, cb, wg, bg, lam, gn, tt):
    w = LRU_WIDTH
    nt = seq // tt
    return pl.pallas_call(
        _lru_kernel,
        grid=(batch, nt),
        in_specs=[
            pl.BlockSpec((tt, w), lambda b, t: (b * nt + t, OFF_LRU_X // w)),
            pl.BlockSpec((tt, w), lambda b, t: (b * nt + t, OFF_LRU_Y // w)),
            pl.BlockSpec((CONV_WIDTH, w), lambda b, t: (0, 0)),
            pl.BlockSpec((1, w), lambda b, t: (0, 0)),
            pl.BlockSpec((w, 2 * w), lambda b, t: (0, 0)),
            pl.BlockSpec((1, 2 * w), lambda b, t: (0, 0)),
            pl.BlockSpec((1, w), lambda b, t: (0, 0)),
            pl.BlockSpec((1, w), lambda b, t: (0, 0)),
        ],
        out_specs=pl.BlockSpec((tt, w), lambda b, t: (b * nt + t, 0)),
        out_shape=jax.ShapeDtypeStruct((batch * seq, w), BF16),
        scratch_shapes=[
            pltpu.VMEM((tt + 8, w), F32),
            pltpu.VMEM((tt, w), F32),
            pltpu.VMEM((tt, w), F32),
            pltpu.VMEM((8, w), F32),
        ],
        compiler_params=_cparams(("parallel", "arbitrary")),
        name="lru",
    )(p, p, cw, cb, wg, bg, lam, gn)


def _rope(x, cosf, sina, sinb):
    return x * cosf + pltpu.roll(x, LANES - ROPE_HALF, 1) * sina + pltpu.roll(x, ROPE_HALF, 1) * sinb


def _nsa_prep_kernel(q_ref, ks_ref, vs_ref, kw_ref, vw_ref, cos_ref, sina_ref, sinb_ref,
                     qn_ref, ksn_ref, kwn_ref,
                     qc_ref, qr_ref, kso_ref, vso_ref, kwo_ref, vwo_ref):
    cosf, sina, sinb = cos_ref[...], sina_ref[...], sinb_ref[...]
    for h in range(NSA_HEADS):
        sl = slice(h * HEAD_DIM, (h + 1) * HEAD_DIM)
        qh = _rms(q_ref[:, sl], qn_ref[...])
        qc_ref[:, sl] = (qh * QK_LOG2_SCALE).astype(BF16)
        qr_ref[:, sl] = (_rope(qh, cosf, sina, sinb) * QK_LOG2_SCALE).astype(BF16)
    for h in range(NSA_KV_HEADS):
        sl = slice(h * HEAD_DIM, (h + 1) * HEAD_DIM)
        kso_ref[:, sl] = _rope(_rms(ks_ref[:, sl], ksn_ref[...]), cosf, sina, sinb).astype(BF16)
        kwo_ref[:, sl] = _rope(_rms(kw_ref[:, sl], kwn_ref[...]), cosf, sina, sinb).astype(BF16)
        ones = jnp.ones((VT_ROWS - HEAD_DIM, vs_ref.shape[0]), BF16)
        vso_ref[0, h, 0:HEAD_DIM, :] = vs_ref[:, sl].T.astype(BF16)
        vso_ref[0, h, HEAD_DIM:VT_ROWS, :] = ones
        vwo_ref[0, h, 0:HEAD_DIM, :] = vw_ref[:, sl].T.astype(BF16)
        vwo_ref[0, h, HEAD_DIM:VT_ROWS, :] = ones


def _nsa_prep(p, seq, cosf, sina, sinb, qn, ksn, kwn, tt):
    m = p.shape[0]
    nt = seq // tt
    kvw = NSA_KV_WIDTH

    def col(off, width):
        return pl.BlockSpec((tt, width), lambda i: (i, off // width))

    def tab():
        return pl.BlockSpec((tt, LANES), lambda i: (i % nt, 0))

    def vec():
        return pl.BlockSpec((1, HEAD_DIM), lambda i: (0, 0))

    def vt_spec():
        return pl.BlockSpec((1, NSA_KV_HEADS, VT_ROWS, tt), lambda i: (i // nt, 0, 0, i % nt))

    return pl.pallas_call(
        _nsa_prep_kernel,
        grid=(m // tt,),
        in_specs=[col(OFF_Q, NSA_WIDTH), col(OFF_KS, kvw), col(OFF_VS, kvw), col(OFF_KW, kvw), col(OFF_VW, kvw),
                  tab(), tab(), tab(), vec(), vec(), vec()],
        out_specs=[pl.BlockSpec((tt, NSA_WIDTH), lambda i: (i, 0)),
                   pl.BlockSpec((tt, NSA_WIDTH), lambda i: (i, 0)),
                   pl.BlockSpec((tt, kvw), lambda i: (i, 0)),
                   vt_spec(),
                   pl.BlockSpec((tt, kvw), lambda i: (i, 0)),
                   vt_spec()],
        out_shape=[jax.ShapeDtypeStruct((m, NSA_WIDTH), BF16),
                   jax.ShapeDtypeStruct((m, NSA_WIDTH), BF16),
                   jax.ShapeDtypeStruct((m, kvw), BF16),
                   jax.ShapeDtypeStruct((m // seq, NSA_KV_HEADS, VT_ROWS, seq), BF16),
                   jax.ShapeDtypeStruct((m, kvw), BF16),
                   jax.ShapeDtypeStruct((m // seq, NSA_KV_HEADS, VT_ROWS, seq), BF16)],
        compiler_params=_cparams(("parallel",)),
        name="nsa_prep",
    )(p, p, p, p, p, cosf, sina, sinb, qn, ksn, kwn)


def _nsa_cmp_kernel(k_ref, v_ref, pek_ref, w1k_ref, w2k_ref, pev_ref, w1v_ref, w2v_ref, kn_ref,
                    kc_ref, vc_ref):
    ng = k_ref.shape[0] // CMP_STRIDE
    rowid = lax.broadcasted_iota(jnp.int32, (ng, HEAD_DIM), 0)

    def compress(x_ref, pe_ref, w1_ref, w2_ref):
        first = jnp.zeros((ng, HEAD_DIM), F32)
        second = jnp.zeros((ng, HEAD_DIM), F32)
        for j in range(CMP_STRIDE):
            xj = x_ref[pl.ds(j, ng, stride=CMP_STRIDE), :]
            lo = (xj + pe_ref[j:j + 1, :]).astype(BF16)
            hi = (xj + pe_ref[CMP_STRIDE + j:CMP_STRIDE + j + 1, :]).astype(BF16)
            first += _dot(lo, w1_ref[j * HEAD_DIM:(j + 1) * HEAD_DIM, :])
            second += _dot(hi, w1_ref[(CMP_STRIDE + j) * HEAD_DIM:(CMP_STRIDE + j + 1) * HEAD_DIM, :])
        pre = first + pltpu.roll(second, ng - 1, 0)
        return _dot(_gelu_tanh(pre).astype(BF16), w2_ref[...])

    kc = _rms(compress(k_ref, pek_ref, w1k_ref, w2k_ref), kn_ref[...])
    vc = compress(v_ref, pev_ref, w1v_ref, w2v_ref)
    valid = rowid < ng - 1
    kc_ref[0, 0] = jnp.where(valid, kc, 0.0).astype(BF16)
    vc_ref[0, 0] = jnp.where(valid, vc, 0.0).T.astype(BF16)


def _nsa_cmp(p, batch, seq, pek, w1k, w2k, pev, w1v, w2v, kn):
    ng = seq // CMP_STRIDE
    hd = HEAD_DIM
    kvh = NSA_KV_HEADS

    def full(shape):
        return pl.BlockSpec(shape, lambda b, h: (0,) * len(shape))

    return pl.pallas_call(
        _nsa_cmp_kernel,
        grid=(batch, kvh),
        in_specs=[
            pl.BlockSpec((seq, hd), lambda b, h: (b, OFF_KC // hd + h)),
            pl.BlockSpec((seq, hd), lambda b, h: (b, OFF_VC // hd + h)),
            full((CMP_BLOCK, hd)), full((CMP_BLOCK * hd, hd)), full((hd, hd)),
            full((CMP_BLOCK, hd)), full((CMP_BLOCK * hd, hd)), full((hd, hd)),
            full((1, hd)),
        ],
        out_specs=[pl.BlockSpec((1, 1, ng, hd), lambda b, h: (b, h, 0, 0)),
                   pl.BlockSpec((1, 1, hd, ng), lambda b, h: (b, h, 0, 0))],
        out_shape=[jax.ShapeDtypeStruct((batch, kvh, ng, hd), BF16),
                   jax.ShapeDtypeStruct((batch, kvh, hd, ng), BF16)],
        compiler_params=_cparams(("parallel", "parallel")),
        name="nsa_cmp",
    )(p, p, pek, w1k, w2k, pev, w1v, w2v, kn)


def _nsa_attn_kernel(qc_ref, qr_ref, kc_ref, vct_ref, ks_ref, vst_ref, kw_ref, vwt_ref, gl_ref, gb_ref, ovl_ref,
                     o_ref, selb_ref, rk_ref, gt_ref, acc_ref, z_ref, p_ref, zc_ref, pcb_ref, zw_ref, pw_ref, ob_ref, *, tq, tk, group, n_sel):
    seq = ks_ref.shape[0]
    ncp = kc_ref.shape[2]
    ns = seq // SEL_BLOCK
    grp = NSA_GROUP
    hd = HEAD_DIM
    gq = grp * tq
    nblk = tk // SEL_BLOCK
    kv = pl.program_id(1)
    s0 = pl.program_id(2) * tq
    wspan = WINDOW + tq

    qc = jnp.concatenate([qc_ref[:, g * hd:(g + 1) * hd] for g in range(grp)], axis=0)
    qr = jnp.concatenate([qr_ref[:, g * hd:(g + 1) * hd] for g in range(grp)], axis=0)
    t_row = s0 + lax.broadcasted_iota(jnp.int32, (1, tq), 1)
    t_all = s0 + lax.broadcasted_iota(jnp.int32, (1, gq), 1) % tq

    def heads(x):
        return jnp.concatenate([x] * grp, axis=1)

    srow = lax.broadcasted_iota(jnp.int32, (SEL_BLOCK, 1), 0)
    chunk = SEL_BLOCK

    def biased_max(z, nrows, bias_fn):
        mx = jnp.full((8, gq), -MASK_VALUE, F32)
        for c in range(nrows // chunk):
            rs = slice(c * chunk, (c + 1) * chunk)
            zb = z[rs, :]
            if bias_fn is not None:
                zb = zb + heads(bias_fn(c))
                z[rs, :] = zb
            mx = jnp.maximum(mx, jnp.max(zb.reshape(chunk // 8, 8, gq), axis=0))
        return jnp.max(mx, axis=0, keepdims=True)

    start = pl.multiple_of(jnp.maximum(s0 - WINDOW, 0), tq)
    zc_ref[...] = _dot_nt(kc_ref[0, 0], qc)
    zw_ref[...] = _dot_nt(kw_ref[pl.ds(start, wspan), :], qr)

    def scores(kt):
        return _dot_nt(ks_ref[pl.ds(pl.multiple_of(kt * tk, tk), tk), :], qr)

    z_ref[0] = scores(0)

    def cmp_bias(c):
        cend = (c * chunk + srow) * CMP_STRIDE + (CMP_BLOCK - 1)
        return jnp.where(cend <= t_row, 0.0, -MASK_VALUE)

    def win_bias(c):
        kpos = start + c * chunk + srow
        return jnp.where((kpos <= t_row) & (kpos > t_row - WINDOW), 0.0, -MASK_VALUE)

    mc = biased_max(zc_ref, ncp, cmp_bias)
    mw = biased_max(zw_ref, wspan, win_bias)
    den8 = jnp.zeros((8, gq), F32)
    for c in range(ncp // chunk):
        rs = slice(c * chunk, (c + 1) * chunk)
        e = jnp.exp2(zc_ref[rs, :] - mc)
        zc_ref[rs, :] = e
        pcb_ref[rs, :] = e.astype(BF16)
        den8 = den8 + jnp.sum(e.reshape(chunk // 8, 8, gq), axis=0)
    for c in range(wspan // chunk):
        rs = slice(c * chunk, (c + 1) * chunk)
        pw_ref[rs, :] = jnp.exp2(zw_ref[rs, :] - mw).astype(BF16)
    inv_c = jnp.where(t_all >= CMP_BLOCK - 1, 1.0 / jnp.sum(den8, axis=0, keepdims=True), 0.0)
    ob_ref[0] = _dot(vct_ref[0, 0], pcb_ref[...]) * inv_c
    ow = _dot(vwt_ref[0, 0, :, pl.ds(start, wspan)], pw_ref[...])
    ob_ref[1] = ow[0:hd, :] * (1.0 / ow[hd:hd + 1, :])

    pcs = []
    for c in range(ncp // chunk):
        pn = zc_ref[c * chunk:(c + 1) * chunk, :] * inv_c
        pcs.append(sum(pn[:, g * tq:(g + 1) * tq] for g in range(1, grp)) + pn[:, 0:tq])
    pcs = jnp.concatenate(pcs, axis=0)
    pcs_hi = pcs.astype(BF16)
    pcs_lo = (pcs - pcs_hi.astype(F32)).astype(BF16)
    imp = _dot(ovl_ref[...], pcs_hi) + _dot(ovl_ref[...], pcs_lo)
    jrow = lax.broadcasted_iota(jnp.int32, (ns, tq), 0)
    cur = (s0 + lax.broadcasted_iota(jnp.int32, (ns, tq), 1)) // SEL_BLOCK
    forced = (jrow == 0) | (jrow == cur) | (jrow == cur - 1)
    imp = jnp.where(forced, MASK_VALUE, jnp.where(jrow <= cur, imp, -MASK_VALUE))
    ngrp = ns // 8
    imp_g = [imp[8 * r:8 * r + 8, :] for r in range(ngrp)]
    sub = lax.broadcasted_iota(jnp.int32, (8, tq), 0)
    rk_ref[...] = jnp.zeros_like(rk_ref)
    for ib in range(ngrp):
        @pl.when(ib * 8 * SEL_BLOCK < s0 + tq)
        def _():
            add = [jnp.zeros((8, tq), F32) for _ in range(ngrp)]
            for i in range(8 * ib, 8 * ib + 8):
                other = jnp.broadcast_to(imp[i:i + 1, :], (8, tq))
                for r in range(ngrp):
                    if r > ib:
                        ahead = jnp.where(other >= imp_g[r], 1.0, 0.0)
                    elif r < ib:
                        ahead = jnp.where(other > imp_g[r], 1.0, 0.0)
                    else:
                        ahead = jnp.where(sub > i - 8 * r, jnp.where(other >= imp_g[r], 1.0, 0.0),
                                          jnp.where(other > imp_g[r], 1.0, 0.0))
                    add[r] = add[r] + ahead
            for r in range(ngrp):
                rk_ref[8 * r:8 * r + 8, :] += add[r]
    selb_ref[...] = jnp.where(rk_ref[...] < n_sel, 0.0, -MASK_VALUE)

    acc_ref[...] = jnp.zeros_like(acc_ref)

    n_kt = (s0 + tq + tk - 1) // tk
    n_groups = (n_kt + group - 1) // group

    def tile_step(kt, m, slot):
        z_ref[1 - slot] = scores(jnp.minimum(kt + 1, group * n_groups - 1))
        koff = pl.multiple_of(kt * tk, tk)
        zs = z_ref.at[slot]

        def sel_bias(j):
            return (jnp.broadcast_to(selb_ref[pl.ds(kt * nblk + j, 1), :], (chunk, tq))
                    + jnp.where(koff + j * chunk + srow <= t_row, 0.0, -MASK_VALUE))

        m_new = jnp.maximum(m, biased_max(zs, tk, sel_bias))
        for j in range(nblk):
            rs = slice(j * chunk, (j + 1) * chunk)
            p_ref[rs, :] = jnp.exp2(zs[rs, :] - m_new).astype(BF16)
        acc_ref[...] = acc_ref[...] * jnp.exp2(m - m_new) + _dot(vst_ref[0, 0, :, pl.ds(koff, tk)], p_ref[...])
        return m_new

    def group_body(gi, m):
        for u in range(group):
            m = tile_step(group * gi + u, m, u % 2)
        return m

    lax.fori_loop(0, n_groups, group_body, jnp.full((1, gq), -MASK_VALUE, F32))
    o_sel = acc_ref[0:hd, :] * (1.0 / acc_ref[hd:hd + 1, :])

    gt_ref[...] = _sigmoid(gl_ref[...] + gb_ref[...]).T
    for g in range(grp):
        base = MISC_GATE + (kv * grp + g) * 3
        cs = slice(g * tq, (g + 1) * tq)
        og = (gt_ref[pl.ds(base, 1), :] * ob_ref[0, :, cs] + gt_ref[pl.ds(base + 1, 1), :] * o_sel[:, cs]
              + gt_ref[pl.ds(base + 2, 1), :] * ob_ref[1, :, cs])
        o_ref[:, g * hd:(g + 1) * hd] = og.T


def _nsa_attn(p, qc, qr, kc, vct, ks, vst, kw, vwt, gate_b, ovl, batch, seq, tq, tk, group):
    hd = HEAD_DIM
    gw = NSA_GROUP * hd
    nq = seq // tq
    ns = seq // SEL_BLOCK
    ncp = kc.shape[2]
    n_sel = min(N_SELECT, ns)
    assert seq % (group * tk) == 0 and group % 2 == 0 and tk % SEL_BLOCK == 0
    assert seq >= WINDOW + tq and WINDOW % tq == 0

    def qspec():
        return pl.BlockSpec((tq, gw), lambda b, h, i: (b * nq + i, h))

    def kspec(width):
        return pl.BlockSpec((seq, width), lambda b, h, i: (b, h))

    def vtspec():
        return pl.BlockSpec((1, 1, VT_ROWS, seq), lambda b, h, i: (b, h, 0, 0))

    return pl.pallas_call(
        functools.partial(_nsa_attn_kernel, tq=tq, tk=tk, group=group, n_sel=n_sel),
        grid=(batch, NSA_KV_HEADS, nq),
        in_specs=[qspec(), qspec(),
                  pl.BlockSpec((1, 1, ncp, hd), lambda b, h, i: (b, h, 0, 0)),
                  pl.BlockSpec((1, 1, hd, ncp), lambda b, h, i: (b, h, 0, 0)),
                  kspec(hd), vtspec(), kspec(hd), vtspec(),
                  pl.BlockSpec((tq, LANES), lambda b, h, i: (b * nq + i, OFF_MISC // LANES)),
                  pl.BlockSpec((1, LANES), lambda b, h, i: (0, 0)),
                  pl.BlockSpec((ns, ncp), lambda b, h, i: (0, 0))],
        out_specs=pl.BlockSpec((tq, gw), lambda b, h, i: (b * nq + i, h)),
        out_shape=jax.ShapeDtypeStruct((batch * seq, NSA_WIDTH), F32),
        scratch_shapes=[pltpu.VMEM((ns, tq), F32), pltpu.VMEM((ns, tq), F32), pltpu.VMEM((LANES, tq), F32),
                        pltpu.VMEM((VT_ROWS, NSA_GROUP * tq), F32),
                        pltpu.VMEM((2, tk, NSA_GROUP * tq), F32), pltpu.VMEM((tk, NSA_GROUP * tq), BF16),
                        pltpu.VMEM((ncp, NSA_GROUP * tq), F32), pltpu.VMEM((ncp, NSA_GROUP * tq), BF16),
                        pltpu.VMEM((WINDOW + tq, NSA_GROUP * tq), F32),
                        pltpu.VMEM((WINDOW + tq, NSA_GROUP * tq), BF16),
                        pltpu.VMEM((2, hd, NSA_GROUP * tq), F32)],
        compiler_params=_cparams(("parallel", "parallel", "arbitrary")),
        name="nsa_attn",
    )(qc, qr, kc, vct, ks, vst, kw, vwt, p, gate_b, ovl)


def _gla_kernel(q_ref, k_ref, v_ref, g_ref, misc_ref, a2_ref, ab_ref, gn_ref, ex_ref, o_ref,
                st_ref, qe_ref, kd_ref, vb_ref, ds_ref, oi_ref, od_ref):
    nb, tt, wq = q_ref.shape
    wv = v_ref.shape[2]
    step = GLA_STEP
    ng = tt // step
    nh = GLA_HEADS
    hl = LANES
    per = hl // GLA_DK
    npair = nh // per

    @pl.when(pl.program_id(1) == 0)
    def _():
        st_ref[...] = jnp.zeros_like(st_ref)

    pos = lax.broadcasted_iota(jnp.int32, (tt, wq), 0) % step
    pos3 = pos.reshape(ng, step, wq)

    def prepare(s_i, carry):
        x = _dot(misc_ref[s_i].astype(BF16), a2_ref[...]) + ab_ref[...]
        b = (-_softplus(-x)) * np.float32(np.log2(np.e) / GLA_TAU)
        s = 1
        while s < step:
            b = b + jnp.where(pos >= s, pltpu.roll(b, s, 0), 0.0)
            s *= 2

        q3 = (q_ref[s_i] * np.float32(GLA_DK ** -0.5)).reshape(ng, step, wq)
        k3 = k_ref[s_i].reshape(ng, step, wq)
        v3 = v_ref[s_i].reshape(ng, step, wv)
        b3 = b.reshape(ng, step, wq)
        b_last = b3[:, step - 1:step, :]
        qe_ref[s_i] = (q3 * jnp.exp2(b3)).reshape(tt, wq).astype(BF16)
        kd = (k3 * jnp.exp2(b_last - b3)).reshape(tt, wq)
        for pr in range(npair):
            kd_ref[s_i * npair + pr] = kd[:, pr * hl:(pr + 1) * hl].astype(BF16)
        vb_ref[s_i] = v_ref[s_i].astype(BF16)
        ds_ref[s_i] = jnp.exp2(b_last)

        half = step // 2
        odiag = jnp.zeros((ng, step, wv), F32)
        odiag_hi = jnp.zeros((ng, half, wv), F32)
        q3h, b3h = q3[:, half:, :], b3[:, half:, :]
        pos3h = half + lax.broadcasted_iota(jnp.int32, (ng, half, wq), 1)
        for j in range(step):
            if j < half:
                dec = jnp.exp2(jnp.where(pos3 >= j, b3 - b3[:, j:j + 1, :], -MASK_VALUE))
                term = (q3 * k3[:, j:j + 1, :] * dec).reshape(tt, wq).astype(BF16)
                odiag = odiag + _dot(term, ex_ref[...]).reshape(ng, step, wv) * v3[:, j:j + 1, :]
            else:
                dec = jnp.exp2(jnp.where(pos3h >= j, b3h - b3[:, j:j + 1, :], -MASK_VALUE))
                term = (q3h * k3[:, j:j + 1, :] * dec).reshape(tt // 2, wq).astype(BF16)
                odiag_hi = odiag_hi + _dot(term, ex_ref[...]).reshape(ng, half, wv) * v3[:, j:j + 1, :]
        odiag = jnp.concatenate([odiag[:, :half, :], odiag[:, half:, :] + odiag_hi], axis=1)
        od_ref[s_i] = odiag.reshape(tt, wv)
        return carry

    lax.fori_loop(0, nb, prepare, 0)

    def body(gi, carry):
        r0 = pl.multiple_of(gi * step, step)
        own = (lax.broadcasted_iota(jnp.int32, (per * GLA_DV, hl), 0) // GLA_DV
               == lax.broadcasted_iota(jnp.int32, (per * GLA_DV, hl), 1) // GLA_DK)
        for s_i in range(nb):
            dsg = ds_ref[s_i, gi]
            for pr in range(npair):
                grp_sl = slice(pr * hl, (pr + 1) * hl)
                val_sl = slice(pr * per * GLA_DV, (pr + 1) * per * GLA_DV)
                st = st_ref[s_i * npair + pr]
                oi_ref[s_i, pl.ds(r0, step), val_sl] = _dot_nt(qe_ref[s_i, pl.ds(r0, step), grp_sl], st.astype(BF16))
                upd = _dot_tn(vb_ref[s_i, pl.ds(r0, step), val_sl], kd_ref[s_i * npair + pr, pl.ds(r0, step), :])
                st_ref[s_i * npair + pr] = st * dsg[:, grp_sl] + jnp.where(own, upd, 0.0)
        return carry

    lax.fori_loop(0, ng, body, 0)

    def finish(s_i, carry):
        o = oi_ref[s_i] + od_ref[s_i]
        gate = g_ref[s_i]
        gate = gate * _sigmoid(gate)
        for h in range(nh):
            sl = slice(h * hl, (h + 1) * hl)
            o_ref[s_i, :, sl] = (_rms(o[:, sl], gn_ref[...]) * gate[:, sl]).astype(o_ref.dtype)
        return carry

    lax.fori_loop(0, nb, finish, 0)


def _gla(p, batch, seq, a2, ab, gn, tt, nb):
    wq = GLA_QK_WIDTH
    wv = GLA_WIDTH
    nt = seq // tt
    ng = tt // GLA_STEP
    expand = jnp.asarray(np.kron(np.eye(GLA_HEADS), np.ones((GLA_DK, GLA_DV))), dtype=BF16)
    p3 = p.reshape(batch, seq, p.shape[1])

    def col(off, width):
        return pl.BlockSpec((nb, tt, width), lambda b, t: (b, t, off // width))

    out = pl.pallas_call(
        _gla_kernel,
        grid=(batch // nb, nt),
        in_specs=[col(OFF_GQ, wq), col(OFF_GK, wq), col(OFF_GV, wv), col(OFF_GG, wv), col(OFF_MISC, LANES),
                  pl.BlockSpec((LANES, wq), lambda b, t: (0, 0)),
                  pl.BlockSpec((1, wq), lambda b, t: (0, 0)),
                  pl.BlockSpec((1, GLA_DV), lambda b, t: (0, 0)),
                  pl.BlockSpec((wq, wv), lambda b, t: (0, 0))],
        out_specs=pl.BlockSpec((nb, tt, wv), lambda b, t: (b, t, 0)),
        out_shape=jax.ShapeDtypeStruct((batch, seq, wv), BF16),
        scratch_shapes=[
            pltpu.VMEM((nb * (wq // LANES), (LANES // GLA_DK) * GLA_DV, LANES), F32),
            pltpu.VMEM((nb, tt, wq), BF16),
            pltpu.VMEM((nb * (wq // LANES), tt, LANES), BF16),
            pltpu.VMEM((nb, tt, wv), BF16),
            pltpu.VMEM((nb, ng, 1, wq), F32),
            pltpu.VMEM((nb, tt, wv), F32),
            pltpu.VMEM((nb, tt, wv), F32),
        ],
        compiler_params=_cparams(("parallel", "arbitrary")),
        name="gla",
    )(p3, p3, p3, p3, p3, a2, ab, gn, expand)
    return out.reshape(batch * seq, wv)


def _pack_w_in(w_in):
    o_gate = OFF_VW + NSA_KV_WIDTH
    o_gq = o_gate + NSA_HEADS * 3
    o_ga = o_gq + 2 * GLA_QK_WIDTH + 2 * GLA_WIDTH
    tail = jnp.zeros(w_in.shape[:2] + (N_PACKED - OFF_MISC - NSA_HEADS * 3 - GLA_RANK,), w_in.dtype)
    packed = jnp.concatenate([w_in[..., :o_gate], w_in[..., o_gq:o_ga],
                              w_in[..., o_gate:o_gq], w_in[..., o_ga:o_ga + GLA_RANK], tail], axis=-1)
    assert packed.shape[-1] == N_PACKED
    return packed.astype(BF16)


def _block_diag(w):
    nb, bi, bo = w.shape
    eye = jnp.eye(nb, dtype=w.dtype)
    return (eye[:, None, :, None] * w[:, :, None, :]).reshape(nb * bi, nb * bo)


def _rope_lane_tables(seq):
    inv = 1.0 / (ROPE_THETA ** (jnp.arange(0, ROPE_DIM, 2, dtype=F32) / ROPE_DIM))
    ang = jnp.arange(seq, dtype=F32)[:, None] * inv[None, :]
    cos, sin = jnp.cos(ang), jnp.sin(ang)
    rest = LANES - ROPE_DIM
    zeros_h = jnp.zeros((seq, ROPE_HALF), F32)
    cosf = jnp.concatenate([cos, cos, jnp.ones((seq, rest), F32)], axis=1)
    sina = jnp.concatenate([-sin, zeros_h, jnp.zeros((seq, rest), F32)], axis=1)
    sinb = jnp.concatenate([zeros_h, sin, jnp.zeros((seq, rest), F32)], axis=1)
    return cosf, sina, sinb


def _overlap_t(seq, ncp):
    ns = seq // SEL_BLOCK
    nc = (seq - CMP_BLOCK) // CMP_STRIDE + 1
    cs = np.arange(ncp) * CMP_STRIDE
    ss = np.arange(ns) * SEL_BLOCK
    ov = (cs[None, :] < ss[:, None] + SEL_BLOCK) & (cs[None, :] + CMP_BLOCK > ss[:, None]) & (np.arange(ncp) < nc)[None, :]
    return jnp.asarray(ov.astype(np.float32), dtype=BF16)


def _row(v):
    return v.reshape(1, -1)


def _mixers(p, batch, seq, l, prm, tables, ovl_t):
    cosf, sina, sinb = tables
    wg = jnp.concatenate([_block_diag(prm['lru_gate_a_w'][l]), _block_diag(prm['lru_gate_x_w'][l])], axis=1).astype(BF16)
    bg = jnp.concatenate([prm['lru_gate_a_b'][l], prm['lru_gate_x_b'][l]]).reshape(1, -1)
    ya = _lru(p, batch, seq, prm['lru_conv_w'][l], _row(prm['lru_conv_b'][l]), wg, bg,
              _row(prm['lru_lambda'][l]), _row(prm['lru_out_norm'][l]), tt=min(1024, seq))
    qc, qr, ks, vs, kw, vw = _nsa_prep(p, seq, cosf, sina, sinb, _row(prm['nsa_q_norm'][l]),
                                       _row(prm['nsa_k_sel_norm'][l]), _row(prm['nsa_k_win_norm'][l]),
                                       tt=min(1024, seq))
    kc, vc = _nsa_cmp(p, batch, seq, prm['nsa_cmp_pe_k'][l], prm['nsa_cmp_w1_k'][l].astype(BF16),
                      prm['nsa_cmp_w2_k'][l].astype(BF16), prm['nsa_cmp_pe_v'][l],
                      prm['nsa_cmp_w1_v'][l].astype(BF16), prm['nsa_cmp_w2_v'][l].astype(BF16),
                      _row(prm['nsa_k_cmp_norm'][l]))
    gate_b = jnp.pad(prm['nsa_gate_b'][l], (MISC_GATE, LANES - MISC_GATE - NSA_HEADS * 3)).reshape(1, LANES)
    yb = _nsa_attn(p, qc, qr, kc, vc, ks, vs, kw, vw, gate_b, ovl_t, batch, seq, tq=256, tk=min(512, seq // 2), group=2)
    a2 = jnp.zeros((LANES, GLA_QK_WIDTH), F32).at[MISC_GA:MISC_GA + GLA_RANK].set(prm['gla_a_w2'][l]).astype(BF16)
    ab = _row(prm['gla_a_b'][l])
    yc = _gla(p, batch, seq, a2, ab, _row(prm['gla_out_norm'][l]), tt=min(256, seq),
              nb=next(n for n in (8, 4, 2, 1) if batch % n == 0))
    return ya, yb, yc


def kernel(x, ffn1_norm, ffn1_w_gate, ffn1_w_up, ffn1_w_down, mix_norm, w_in, lru_conv_w, lru_conv_b, lru_gate_a_w, lru_gate_a_b, lru_gate_x_w, lru_gate_x_b, lru_lambda, lru_out_norm, nsa_q_norm, nsa_k_cmp_norm, nsa_k_sel_norm, nsa_k_win_norm, nsa_cmp_pe_k, nsa_cmp_w1_k, nsa_cmp_w2_k, nsa_cmp_pe_v, nsa_cmp_w1_v, nsa_cmp_w2_v, nsa_gate_b, nsa_out_norm, gla_a_w2, gla_a_b, gla_out_norm, w_out, ffn2_norm, ffn2_w_gate, ffn2_w_up, ffn2_w_down):
    prm = dict(lru_conv_w=lru_conv_w, lru_conv_b=lru_conv_b, lru_gate_a_w=lru_gate_a_w, lru_gate_a_b=lru_gate_a_b,
               lru_gate_x_w=lru_gate_x_w, lru_gate_x_b=lru_gate_x_b, lru_lambda=lru_lambda, lru_out_norm=lru_out_norm,
               nsa_q_norm=nsa_q_norm, nsa_k_cmp_norm=nsa_k_cmp_norm, nsa_k_sel_norm=nsa_k_sel_norm,
               nsa_k_win_norm=nsa_k_win_norm, nsa_cmp_pe_k=nsa_cmp_pe_k, nsa_cmp_w1_k=nsa_cmp_w1_k,
               nsa_cmp_w2_k=nsa_cmp_w2_k, nsa_cmp_pe_v=nsa_cmp_pe_v, nsa_cmp_w1_v=nsa_cmp_w1_v,
               nsa_cmp_w2_v=nsa_cmp_w2_v, nsa_gate_b=nsa_gate_b, gla_a_w2=gla_a_w2, gla_a_b=gla_a_b,
               gla_out_norm=gla_out_norm)
    batch, seq, d = x.shape
    depth = w_in.shape[0]
    m = batch * seq
    tables = _rope_lane_tables(seq)
    ncp = seq // CMP_STRIDE
    ovl_t = _overlap_t(seq, ncp)
    tm = min(512, m)
    tm_ffn = min(1024, m)
    tf = 512
    xf = x.reshape(m, d)
    ffn1 = [w.astype(BF16) for w in (ffn1_w_gate, ffn1_w_up, ffn1_w_down)]
    ffn2 = [w.astype(BF16) for w in (ffn2_w_gate, ffn2_w_up, ffn2_w_down)]
    w_in_packed = _pack_w_in(w_in)
    w_out_bf = w_out.astype(BF16)
    for l in range(depth):
        xf = _ffn(xf, _row(ffn1_norm[l]), *ffn1, l, tm_ffn, tf)
        p = _mix_in(xf, _row(mix_norm[l]), w_in_packed, l, tm_ffn, N_PACKED // MIX_IN_TILES)
        ya, yb, yc = _mixers(p, batch, seq, l, prm, tables, ovl_t)
        xf = _mix_out(xf, ya, yb, yc, _row(nsa_out_norm[l]), w_out_bf, l, tm)
        xf = _ffn(xf, _row(ffn2_norm[l]), *ffn2, l, tm_ffn, tf)
    return xf.reshape(batch, seq, d)
```
